```python
import math
import jax, jax.numpy as jnp
from jax import lax
import numpy as np

D_MODEL = 1024
BATCH = 4
SEQ = 4096
DEPTH = 1

NSA_HEADS = 8
NSA_KV_HEADS = 2
HEAD_DIM = 64
NSA_WIDTH = NSA_HEADS * HEAD_DIM
KV_WIDTH = NSA_KV_HEADS * HEAD_DIM
POOL_WIDTH = D_MODEL - NSA_WIDTH
POOL_GROUPS = 4
POOL_GROUP_DIM = POOL_WIDTH // POOL_GROUPS
POOL_WINDOWS = (2, 4, 8, 16)
N_GATES = 3 * NSA_HEADS
IN_WIDTH = NSA_WIDTH + 6 * KV_WIDTH + N_GATES + POOL_WIDTH
CMP_BLOCK = 32
CMP_STRIDE = 16
CMP_HIDDEN = 256
SLC_BLOCK = 64
SLC_COUNT = 16
WINDOW = 512
Q_BLOCK = 128
NEG = -1e30
FORCE = 1e6
N_BUCKETS = 32
MAX_DISTANCE = 128
PEER_HEADS = 8
PEER_KEYS = 128
PEER_EXPERTS = PEER_KEYS * PEER_KEYS
PEER_QDIM = 256
PEER_TOPK = 16
TOKEN_CHUNK = 128
ALPHA = (2 * DEPTH) ** 0.25
BETA = (8 * DEPTH) ** -0.25
LN_EPS = 1e-5

kernel_name = "nsa_pool_peer_hybrid_deepnorm"


def layer_norm(x, g, b):
    xf = x.astype(jnp.float32)
    mu = jnp.mean(xf, axis=-1, keepdims=True)
    var = jnp.mean(jnp.square(xf - mu), axis=-1, keepdims=True)
    return ((xf - mu) * lax.rsqrt(var + LN_EPS)).astype(x.dtype) * g + b


def rel_bucket(dist):
    dist = jnp.maximum(dist, 0)
    max_exact = N_BUCKETS // 2
    large = max_exact + (jnp.log(jnp.maximum(dist, 1).astype(jnp.float32) / max_exact)
                         / math.log(MAX_DISTANCE / max_exact)
                         * (N_BUCKETS - max_exact)).astype(jnp.int32)
    large = jnp.minimum(large, N_BUCKETS - 1)
    return jnp.where(dist < max_exact, dist, large)


def compress(k_raw, pe, w1, w2):
    B, T, G, dh = k_raw.shape
    n_cmp = (T - CMP_BLOCK) // CMP_STRIDE + 1
    idx = jnp.arange(n_cmp)[:, None] * CMP_STRIDE + jnp.arange(CMP_BLOCK)[None, :]
    blocks = k_raw[:, idx] + pe[None, None, :, None, :]
    flat = blocks.transpose(0, 3, 1, 2, 4).reshape(B, G, n_cmp, CMP_BLOCK * dh)
    return jax.nn.gelu(flat @ w1, approximate=False) @ w2


def nsa_attention(q, k_cmp, v_cmp, k_slc, v_slc, k_win, v_win, rel_table):
    B, G, R, T, dh = q.shape
    n_cmp = k_cmp.shape[2]
    n_slc = T // SLC_BLOCK
    k_sel = min(SLC_COUNT, n_slc)
    scale = dh ** -0.5
    cmp_start = jnp.arange(n_cmp) * CMP_STRIDE
    cmp_end = cmp_start + CMP_BLOCK - 1
    slc_start = jnp.arange(n_slc) * SLC_BLOCK
    overlap = ((cmp_start[:, None] <= slc_start[None, :] + SLC_BLOCK - 1)
               & (cmp_end[:, None] >= slc_start[None, :])).astype(jnp.float32)
    pad_k = jnp.pad(k_win, ((0, 0), (0, 0), (WINDOW, 0), (0, 0)))
    pad_v = jnp.pad(v_win, ((0, 0), (0, 0), (WINDOW, 0), (0, 0)))
    tab_g = rel_table.reshape(N_BUCKETS, G, R).transpose(1, 0, 2)
    g_idx = jnp.arange(G)[None, :, None, None]
    b_idx = jnp.arange(B)[:, None, None, None]
    j_idx = jnp.arange(n_slc)
    n_win = Q_BLOCK + WINDOW

    def head_bias(bucket):
        bias = rel_table[bucket]
        return bias.reshape(bucket.shape + (G, R)).transpose(2, 3, 0, 1).astype(jnp.float32)

    def block(qb):
        q0 = qb * Q_BLOCK
        qblk = lax.dynamic_slice_in_dim(q, q0, Q_BLOCK, axis=3)
        t = q0 + jnp.arange(Q_BLOCK)
        s = jnp.einsum('bgrqd,bgnd->bgrqn', qblk, k_cmp).astype(jnp.float32) * scale
        s = s + head_bias(rel_bucket(t[:, None] - cmp_end[None, :]))
        valid = cmp_end[None, :] <= t[:, None]
        p_cmp = jax.nn.softmax(jnp.where(valid, s, NEG), axis=-1) * valid
        o_cmp = jnp.einsum('bgrqn,bgnd->bgrqd', p_cmp.astype(v_cmp.dtype), v_cmp)
        imp = jnp.einsum('bgrqn,nj->bgqj', p_cmp, overlap)
        cur = t // SLC_BLOCK
        forced = (j_idx[None, :] == 0) | (j_idx[None, :] == cur[:, None]) | (j_idx[None, :] == cur[:, None] - 1)
        imp = jnp.where(forced, FORCE, imp)
        imp = jnp.where(slc_start[None, :] <= t[:, None], imp, NEG)
        _, sel = lax.top_k(imp, k_sel)
        pos = (sel[..., None] * SLC_BLOCK + jnp.arange(SLC_BLOCK)).reshape(B, G, Q_BLOCK, k_sel * SLC_BLOCK)
        ks = k_slc[b_idx, g_idx, pos]
        vs = v_slc[b_idx, g_idx, pos]
        s = jnp.einsum('bgrqd,bgqkd->bgrqk', qblk, ks).astype(jnp.float32) * scale
        bias = tab_g[g_idx, rel_bucket(t[None, None, :, None] - pos)]
        s = s + jnp.moveaxis(bias, -1, 2).astype(jnp.float32)
        valid = (pos <= t[None, None, :, None])[:, :, None]
        p_slc = jax.nn.softmax(jnp.where(valid, s, NEG), axis=-1)
        o_slc = jnp.einsum('bgrqk,bgqkd->bgrqd', p_slc.astype(vs.dtype), vs)
        kw = lax.dynamic_slice_in_dim(pad_k, q0, n_win, axis=2)
        vw = lax.dynamic_slice_in_dim(pad_v, q0, n_win, axis=2)
        kpos = q0 - WINDOW + jnp.arange(n_win)
        dist = t[:, None] - kpos[None, :]
        s = jnp.einsum('bgrqd,bgkd->bgrqk', qblk, kw).astype(jnp.float32) * scale
        s = s + head_bias(rel_bucket(dist))
        valid = (dist >= 0) & (dist < WINDOW) & (kpos[None, :] >= 0)
        p_win = jax.nn.softmax(jnp.where(valid, s, NEG), axis=-1)
        o_win = jnp.einsum('bgrqk,bgkd->bgrqd', p_win.astype(vw.dtype), vw)
        return o_cmp, o_slc, o_win

    outs = lax.map(block, jnp.arange(T // Q_BLOCK))
    def to_bthd(o):
        return o.transpose(1, 0, 4, 2, 3, 5).reshape(B, T, G * R, dh)
    return to_bthd(outs[0]), to_bthd(outs[1]), to_bthd(outs[2])


def pool_mixer(z, w, scale):
    B, T, _ = z.shape
    zf = z.astype(jnp.float32).reshape(B, T, POOL_GROUPS, POOL_GROUP_DIM)
    c = jnp.pad(jnp.cumsum(zf, axis=1), ((0, 0), (1, 0), (0, 0), (0, 0)))
    t = jnp.arange(T)
    outs = []
    for gi, win in enumerate(POOL_WINDOWS):
        lo = jnp.maximum(t + 1 - win, 0)
        s = c[:, 1:, gi] - c[:, lo, gi]
        cnt = (t + 1 - lo).astype(jnp.float32)
        outs.append(s / cnt[None, :, None] - zf[:, :, gi])
    y = jnp.stack(outs, axis=2).astype(z.dtype)
    y = jnp.einsum('btgc,gcd->btgd', y, w).reshape(B, T, POOL_WIDTH)
    return y * scale


def peer(h, wq, subkeys, u, v):
    B, T, D = h.shape
    n = B * T
    xf = h.reshape(n, D)
    q = (xf @ wq).reshape(n, PEER_HEADS, 2, PEER_QDIM // 2)
    s1 = jnp.einsum('nhd,kd->nhk', q[:, :, 0], subkeys[0]).astype(jnp.float32)
    s2 = jnp.einsum('nhd,kd->nhk', q[:, :, 1], subkeys[1]).astype(jnp.float32)
    v1, i1 = lax.top_k(s1, PEER_TOPK)
    v2, i2 = lax.top_k(s2, PEER_TOPK)
    cand = (v1[..., :, None] + v2[..., None, :]).reshape(n, PEER_HEADS, PEER_TOPK * PEER_TOPK)
    cand_id = (i1[..., :, None] * PEER_KEYS + i2[..., None, :]).reshape(n, PEER_HEADS, PEER_TOPK * PEER_TOPK)
    sc, ci = lax.top_k(cand, PEER_TOPK)
    eid = jnp.take_along_axis(cand_id, ci, axis=-1)
    gate = jax.nn.softmax(sc, axis=-1)
    nc = n // TOKEN_CHUNK

    def chunk(args):
        xc, ec, gc = args
        act = jax.nn.gelu(jnp.einsum('cd,chkd->chk', xc, u[ec]).astype(jnp.float32), approximate=False)
        return jnp.einsum('chk,chkd->cd', (gc * act).astype(xc.dtype), v[ec])

    out = lax.map(chunk, (xf.reshape(nc, TOKEN_CHUNK, D),
                          eid.reshape(nc, TOKEN_CHUNK, PEER_HEADS, PEER_TOPK),
                          gate.reshape(nc, TOKEN_CHUNK, PEER_HEADS, PEER_TOPK)))
    return out.reshape(B, T, D)


def setup_inputs(seed: int = 0) -> dict:
    key = jax.random.key(seed)
    ks = jax.random.split(key, 24)
    nrm = jax.random.normal
    L, D = DEPTH, D_MODEL
    f = CMP_BLOCK * HEAD_DIM
    return {
        "x": nrm(ks[0], (BATCH, SEQ, D), jnp.float32),
        "w_in": nrm(ks[1], (L, D, IN_WIDTH), jnp.float32) * D ** -0.5,
        "rel_table": nrm(ks[2], (N_BUCKETS, NSA_HEADS), jnp.float32) * 0.5,
        "cmp_pe_k": nrm(ks[3], (L, CMP_BLOCK, HEAD_DIM), jnp.float32) * 0.1,
        "cmp_pe_v": nrm(ks[4], (L, CMP_BLOCK, HEAD_DIM), jnp.float32) * 0.1,
        "cmp_k_w1": nrm(ks[5], (L, f, CMP_HIDDEN), jnp.float32) * f ** -0.5,
        "cmp_k_w2": nrm(ks[6], (L, CMP_HIDDEN, HEAD_DIM), jnp.float32) * CMP_HIDDEN ** -0.5,
        "cmp_v_w1": nrm(ks[7], (L, f, CMP_HIDDEN), jnp.float32) * f ** -0.5,
        "cmp_v_w2": nrm(ks[8], (L, CMP_HIDDEN, HEAD_DIM), jnp.float32) * CMP_HIDDEN ** -0.5,
        "pool_w": nrm(ks[9], (L, POOL_GROUPS, POOL_GROUP_DIM, POOL_GROUP_DIM), jnp.float32) * POOL_GROUP_DIM ** -0.5,
        "pool_scale": 1.0 + 0.02 * nrm(ks[10], (L, POOL_WIDTH), jnp.float32),
        "w_out": nrm(ks[11], (L, D, D), jnp.float32) * (D ** -0.5) * BETA,
        "ln1_g": 1.0 + 0.02 * nrm(ks[12], (L, D), jnp.float32),
        "ln1_b": 0.02 * nrm(ks[13], (L, D), jnp.float32),
        "peer_wq": nrm(ks[14], (L, D, PEER_HEADS * PEER_QDIM), jnp.float32) * D ** -0.5,
        "peer_subkeys": nrm(ks[15], (L, 2, PEER_KEYS, PEER_QDIM // 2), jnp.float32) * (PEER_QDIM // 2) ** -0.5,
        "peer_u": nrm(ks[16], (L, PEER_EXPERTS, D), jnp.float32) * D ** -0.5,
        "peer_v": nrm(ks[17], (L, PEER_EXPERTS, D), jnp.float32) * BETA,
        "ln2_g": 1.0 + 0.02 * nrm(ks[18], (L, D), jnp.float32),
        "ln2_b": 0.02 * nrm(ks[19], (L, D), jnp.float32),
    }


def reference(x, w_in, rel_table, cmp_pe_k, cmp_pe_v, cmp_k_w1, cmp_k_w2, cmp_v_w1, cmp_v_w2,
              pool_w, pool_scale, w_out, ln1_g, ln1_b, peer_wq, peer_subkeys, peer_u, peer_v,
              ln2_g, ln2_b):
    B, T, _ = x.shape
    G, R = NSA_KV_HEADS, NSA_HEADS // NSA_KV_HEADS
    splits = np.cumsum([NSA_WIDTH] + [KV_WIDTH] * 6 + [N_GATES]).tolist()
    h = x
    for l in range(DEPTH):
        proj = h @ w_in[l]
        q, kc, vc, kslc, vslc, kwin, vwin, gates, z = jnp.split(proj, splits, axis=-1)
        q = q.reshape(B, T, G, R, HEAD_DIM).transpose(0, 2, 3, 1, 4)
        kv = lambda a: a.reshape(B, T, G, HEAD_DIM)
        k_cmp = compress(kv(kc), cmp_pe_k[l], cmp_k_w1[l], cmp_k_w2[l])
        v_cmp = compress(kv(vc), cmp_pe_v[l], cmp_v_w1[l], cmp_v_w2[l])
        heads = lambda a: kv(a).transpose(0, 2, 1, 3)
        o_cmp, o_slc, o_win = nsa_attention(q, k_cmp, v_cmp, heads(kslc), heads(vslc),
                                            heads(kwin), heads(vwin), rel_table)
        g = jax.nn.sigmoid(gates.astype(jnp.float32)).astype(x.dtype).reshape(B, T, 3, NSA_HEADS, 1)
        nsa_out = (g[:, :, 0] * o_cmp + g[:, :, 1] * o_slc + g[:, :, 2] * o_win).reshape(B, T, NSA_WIDTH)
        pool_out = pool_mixer(z, pool_w[l], pool_scale[l])
        mix = jnp.concatenate([nsa_out, pool_out], axis=-1) @ w_out[l]
        h = layer_norm(ALPHA * h + mix, ln1_g[l], ln1_b[l])
        ffn = peer(h, peer_wq[l], peer_subkeys[l], peer_u[l], peer_v[l])
        h = layer_norm(ALPHA * h + ffn, ln2_g[l], ln2_b[l])
    return h
```

```python
import functools
import math

import numpy as np
import jax
import jax.numpy as jnp
from jax import lax
from jax.experimental import pallas as pl
from jax.experimental.pallas import tpu as pltpu

F32 = jnp.float32
BF16 = jnp.bfloat16
I32 = jnp.int32

D_MODEL = 1024
NSA_HEADS = 8
KV_HEADS = 2
GQA = NSA_HEADS // KV_HEADS
HEAD_DIM = 64
NSA_WIDTH = NSA_HEADS * HEAD_DIM
KV_WIDTH = KV_HEADS * HEAD_DIM
POOL_WIDTH = D_MODEL - NSA_WIDTH
POOL_WINDOWS = (2, 4, 8, 16)
POOL_GROUP_DIM = POOL_WIDTH // len(POOL_WINDOWS)
N_GATES = 3 * NSA_HEADS
GATE_ROWS = 32
CMP_BLOCK = 32
CMP_STRIDE = 16
CMP_HIDDEN = 256
SLC_BLOCK = 64
SLC_COUNT = 16
WINDOW = 512
Q_BLOCK = 128
NEG = -1e30
FORCE = 1e6
N_BUCKETS = 32
MAX_DISTANCE = 128
PEER_HEADS = 8
PEER_KEYS = 128
PEER_EXPERTS = PEER_KEYS * PEER_KEYS
PEER_QDIM = 256
PEER_TOPK = 16
DEPTH = 1
ALPHA = (2 * DEPTH) ** 0.25
LN_EPS = 1e-5
SQRT_HALF = 0.7071067811865476

LANES = 128
VMEM_LIMIT = 52 * 1024 * 1024

_FM_Q, _FM_V, _FM_G, _FM_Z, _FM_END = 0, 512, 768, 800, 1312
_TM_WIDTH = 640


def _gelu(x):
    return 0.5 * x * (1.0 + lax.erf(x * SQRT_HALF))


def _dot(a, b):
    return jnp.dot(a, b, preferred_element_type=F32)


def _dot_nt(a, b):
    return lax.dot_general(a, b, (((1,), (1,)), ((), ())), preferred_element_type=F32)


def _bucket_np(dist):
    dist = np.maximum(dist, 0)
    max_exact = N_BUCKETS // 2
    large = max_exact + np.floor(
        np.log(np.maximum(dist, 1) / max_exact) / math.log(MAX_DISTANCE / max_exact)
        * (N_BUCKETS - max_exact)).astype(np.int64)
    large = np.minimum(large, N_BUCKETS - 1)
    return np.where(dist < max_exact, dist, large)


def _bias_index_tables():
    m = np.arange(512)[:, None] - 256
    qi = np.arange(Q_BLOCK)[None, :]
    d_cmp = qi - CMP_STRIDE * m - (CMP_BLOCK - 1)
    idx_cmp = np.where(d_cmp >= 0, _bucket_np(d_cmp), -1).astype(np.int32)
    ki = np.arange(Q_BLOCK)[:, None]
    d_tile = np.stack([qi - ki, Q_BLOCK + qi - ki])
    idx_tile = np.where(d_tile >= 0, _bucket_np(d_tile), -1).astype(np.int32)
    return idx_cmp, idx_tile


def _overlap_t(n_cmp_rows, n_slc):
    n = np.arange(n_cmp_rows)[None, :]
    j = np.arange(n_slc)[:, None]
    start = n * CMP_STRIDE
    end = start + CMP_BLOCK - 1
    return ((start <= j * SLC_BLOCK + SLC_BLOCK - 1) & (end >= j * SLC_BLOCK)).astype(np.float32)


def _peer_candidates():
    return [(a, b) for a in range(PEER_TOPK) for b in range(PEER_TOPK)
            if (a + 1) * (b + 1) <= PEER_TOPK]


def _bias_kernel(tab_ref, idxc_ref, idxt_ref, et_ref, tb_ref):
    g = pl.program_id(0)
    idx_c = idxc_ref[...]
    idx_t = idxt_ref[...]
    for r in range(GQA):
        h = g * GQA + r
        far = tab_ref[N_BUCKETS - 1, h]
        acc_c = jnp.zeros(idx_c.shape, F32)
        acc_t = jnp.zeros(idx_t.shape, F32)
        for b in range(N_BUCKETS - 1):
            val = tab_ref[b, h] - far
            acc_c = jnp.where(idx_c == b, val, acc_c)
            acc_t = jnp.where(idx_t == b, val, acc_t)
        et_ref[r] = acc_c
        tb_ref[0, :, :, r * LANES:(r + 1) * LANES] = acc_t


def _bias_tables(rel_table):
    idx_c, idx_t = _bias_index_tables()
    return pl.pallas_call(
        _bias_kernel,
        grid=(KV_HEADS,),
        in_specs=[
            pl.BlockSpec(memory_space=pltpu.SMEM),
            pl.BlockSpec((512, Q_BLOCK), lambda g: (0, 0)),
            pl.BlockSpec((2, Q_BLOCK, Q_BLOCK), lambda g: (0, 0, 0)),
        ],
        out_specs=[
            pl.BlockSpec((GQA, 512, Q_BLOCK), lambda g: (g, 0, 0)),
            pl.BlockSpec((1, 2, Q_BLOCK, GQA * LANES), lambda g: (g, 0, 0, 0)),
        ],
        out_shape=[
            jax.ShapeDtypeStruct((NSA_HEADS, 512, Q_BLOCK), F32),
            jax.ShapeDtypeStruct((KV_HEADS, 2, Q_BLOCK, GQA * LANES), F32),
        ],
        name="bias_tables",
    )(rel_table, jnp.asarray(idx_c), jnp.asarray(idx_t))


def _inproj_kernel(x_ref, wfm_ref, wtm_ref, qt_ref, vt_ref, gt_ref, zt_ref, kvc_ref,
                   kslc_ref, kwin_ref, *, tn, seq):
    xb = x_ref[...].astype(BF16)
    fm = _dot_nt(wfm_ref[...], xb)
    tm = _dot(xb, wtm_ref[...])
    qt_ref[...] = (fm[_FM_Q:_FM_V] * (HEAD_DIM ** -0.5)).astype(BF16)
    for c in range(2 * KV_HEADS):
        for s in range(tn // LANES):
            vt_ref[c, s] = fm[_FM_V + c * HEAD_DIM:_FM_V + (c + 1) * HEAD_DIM,
                              s * LANES:(s + 1) * LANES].astype(BF16)
    gt_ref[...] = jax.nn.sigmoid(fm[_FM_G:_FM_Z])
    zt_ref[...] = fm[_FM_Z:_FM_END]
    kvc_ref[0] = tm[:, 0:KV_WIDTH]
    kvc_ref[1] = tm[:, KV_WIDTH:2 * KV_WIDTH]
    t0 = (pl.program_id(0) * tn) % seq
    row = lax.broadcasted_iota(I32, (tn, LANES), 0) + t0
    lane = lax.broadcasted_iota(I32, (tn, LANES), 1)
    onehot = jnp.where(lane - HEAD_DIM == (row >> 6), 1.0, 0.0)
    for g in range(KV_HEADS):
        k = tm[:, 256 + g * LANES:256 + (g + 1) * LANES]
        kslc_ref[g] = jnp.where(lane < HEAD_DIM, k, onehot).astype(BF16)
    kwin_ref[...] = tm[:, 512:640].astype(BF16)


def _inproj(x2, w_in, seq):
    n = x2.shape[0]
    tn = 512
    w = w_in
    o = NSA_WIDTH
    w_q = w[:, 0:o]
    w_kc, w_vc, w_ks, w_vs, w_kw, w_vw = [w[:, o + i * KV_WIDTH:o + (i + 1) * KV_WIDTH] for i in range(6)]
    o += 6 * KV_WIDTH
    w_g = w[:, o:o + N_GATES]
    w_z = w[:, o + N_GATES:]
    zpad = jnp.zeros((D_MODEL, HEAD_DIM), w.dtype)
    wfm = jnp.concatenate(
        [w_q, w_vs, w_vw, w_g, jnp.zeros((D_MODEL, GATE_ROWS - N_GATES), w.dtype), w_z], axis=1).T.astype(BF16)
    wtm = jnp.concatenate(
        [w_kc, w_vc, w_ks[:, :HEAD_DIM], zpad, w_ks[:, HEAD_DIM:], zpad, w_kw], axis=1).astype(BF16)
    nt = n // LANES
    return pl.pallas_call(
        functools.partial(_inproj_kernel, tn=tn, seq=seq),
        grid=(n // tn,),
        in_specs=[
            pl.BlockSpec((tn, D_MODEL), lambda i: (i, 0)),
            pl.BlockSpec((_FM_END, D_MODEL), lambda i: (0, 0)),
            pl.BlockSpec((D_MODEL, _TM_WIDTH), lambda i: (0, 0)),
        ],
        out_specs=[
            pl.BlockSpec((NSA_WIDTH, tn), lambda i: (0, i)),
            pl.BlockSpec((2 * KV_HEADS, tn // LANES, HEAD_DIM, LANES), lambda i: (0, i, 0, 0)),
            pl.BlockSpec((GATE_ROWS, tn), lambda i: (0, i)),
            pl.BlockSpec((POOL_WIDTH, tn), lambda i: (0, i)),
            pl.BlockSpec((2, tn, KV_WIDTH), lambda i: (0, i, 0)),
            pl.BlockSpec((KV_HEADS, tn, LANES), lambda i: (0, i, 0)),
            pl.BlockSpec((tn, LANES), lambda i: (i, 0)),
        ],
        out_shape=[
            jax.ShapeDtypeStruct((NSA_WIDTH, n), BF16),
            jax.ShapeDtypeStruct((2 * KV_HEADS, nt, HEAD_DIM, LANES), BF16),
            jax.ShapeDtypeStruct((GATE_ROWS, n), F32),
            jax.ShapeDtypeStruct((POOL_WIDTH, n), F32),
            jax.ShapeDtypeStruct((2, n, KV_WIDTH), F32),
            jax.ShapeDtypeStruct((KV_HEADS, n, LANES), BF16),
            jax.ShapeDtypeStruct((n, LANES), BF16),
        ],
        compiler_params=pltpu.CompilerParams(vmem_limit_bytes=VMEM_LIMIT),
        name="inproj",
    )(x2, wfm, wtm)


def _compress_kernel(kc_ref, vc_ref, w2k_ref, w2v_ref, pek_ref, pev_ref, w1k_ref, w1v_ref,
                     wok0_ref, wok1_ref, wovt_ref, kcmp_ref, vcmpt_ref, *, n_rows):
    half = CMP_BLOCK // 2
    acc_k = jnp.zeros((n_rows, 4 * CMP_HIDDEN), F32)
    acc_v = jnp.zeros((n_rows, 4 * CMP_HIDDEN), F32)
    for l in range(half):
        rows = pl.ds(l, n_rows, stride=CMP_STRIDE)
        acc_k = acc_k + _dot(kc_ref[rows, :].astype(BF16), w2k_ref[l])
        acc_v = acc_v + _dot(vc_ref[rows, :].astype(BF16), w2v_ref[l])
    ck = _dot(pek_ref[...], w1k_ref[...])[0:1]
    cv = _dot(pev_ref[...], w1v_ref[...])[0:1]

    def hidden(acc, c, g):
        a = acc[:, g * CMP_HIDDEN:(g + 1) * CMP_HIDDEN]
        b = acc[:, (2 + g) * CMP_HIDDEN:(3 + g) * CMP_HIDDEN]
        return _gelu(a + pltpu.roll(b, n_rows - 1, axis=0) + c).astype(BF16)

    kcmp_ref[...] = (_dot(hidden(acc_k, ck, 0), wok0_ref[...])
                     + _dot(hidden(acc_k, ck, 1), wok1_ref[...])).astype(BF16)
    for g in range(KV_HEADS):
        vcmpt_ref[g] = _dot_nt(wovt_ref[...], hidden(acc_v, cv, g)).astype(BF16)


def _compress(kvc, batch, seq, pe_k, pe_v, k_w1, k_w2, v_w1, v_w2):
    n_rows = seq // CMP_STRIDE
    half = CMP_BLOCK // 2

    def blockdiag(w1):
        w = w1.reshape(CMP_BLOCK, HEAD_DIM, CMP_HIDDEN)
        z = jnp.zeros((half, HEAD_DIM, CMP_HIDDEN), w1.dtype)
        top = jnp.concatenate([w[:half], z, w[half:], z], axis=2)
        bot = jnp.concatenate([z, w[:half], z, w[half:]], axis=2)
        return jnp.concatenate([top, bot], axis=1).astype(BF16)

    def pe_rows(pe):
        flat = pe.reshape(1, CMP_BLOCK * HEAD_DIM)
        return jnp.concatenate([flat, jnp.zeros((7, flat.shape[1]), pe.dtype)], axis=0).astype(BF16)

    zo = jnp.zeros((CMP_HIDDEN, HEAD_DIM), k_w2.dtype)
    wok0 = jnp.concatenate([k_w2, zo], axis=1).astype(BF16)
    wok1 = jnp.concatenate([zo, k_w2], axis=1).astype(BF16)
    full = lambda *s: pl.BlockSpec(s, lambda b: (0,) * len(s))
    return pl.pallas_call(
        functools.partial(_compress_kernel, n_rows=n_rows),
        grid=(batch,),
        in_specs=[
            pl.BlockSpec((None, seq, KV_WIDTH), lambda b: (0, b, 0)),
            pl.BlockSpec((None, seq, KV_WIDTH), lambda b: (1, b, 0)),
            full(half, KV_WIDTH, 4 * CMP_HIDDEN), full(half, KV_WIDTH, 4 * CMP_HIDDEN),
            full(8, CMP_BLOCK * HEAD_DIM), full(8, CMP_BLOCK * HEAD_DIM),
            full(CMP_BLOCK * HEAD_DIM, CMP_HIDDEN), full(CMP_BLOCK * HEAD_DIM, CMP_HIDDEN),
            full(CMP_HIDDEN, KV_WIDTH), full(CMP_HIDDEN, KV_WIDTH), full(HEAD_DIM, CMP_HIDDEN),
        ],
        out_specs=[
            pl.BlockSpec((None, n_rows, KV_WIDTH), lambda b: (b, 0, 0)),
            pl.BlockSpec((None, KV_HEADS, HEAD_DIM, n_rows), lambda b: (b, 0, 0, 0)),
        ],
        out_shape=[
            jax.ShapeDtypeStruct((batch, n_rows, KV_WIDTH), BF16),
            jax.ShapeDtypeStruct((batch, KV_HEADS, HEAD_DIM, n_rows), BF16),
        ],
        compiler_params=pltpu.CompilerParams(vmem_limit_bytes=VMEM_LIMIT),
        name="compress",
    )(kvc, kvc, blockdiag(k_w1), blockdiag(v_w1), pe_rows(pe_k), pe_rows(pe_v),
      k_w1.astype(BF16), v_w1.astype(BF16), wok0, wok1, v_w2.T.astype(BF16))


def _softmax_init(s, vt):
    m = jnp.max(s, axis=0, keepdims=True)
    p = jnp.exp(s - m)
    return m, jnp.sum(p, axis=0, keepdims=True), _dot(vt, p.astype(BF16))


def _softmax_step(s, vt, m, l, acc):
    m_new = jnp.maximum(m, jnp.max(s, axis=0, keepdims=True))
    alpha = jnp.exp(m - m_new)
    p = jnp.exp(s - m_new)
    return m_new, alpha * l + jnp.sum(p, axis=0, keepdims=True), alpha * acc + _dot(vt, p.astype(BF16))


def _nsa_kernel(qt_ref, kcmp_ref, vcmpt_ref, kslc_ref, kwin_ref, vslct_ref, vwint_ref, gt_ref,
                et_ref, tb_ref, ovt_ref, out_ref, *, n_cmp_rows, n_slc):
    g = pl.program_id(1)
    qb = pl.program_id(2)
    qw = GQA * Q_BLOCK
    q4 = qt_ref[...]
    qcat = jnp.concatenate([q4[r * HEAD_DIM:(r + 1) * HEAD_DIM] for r in range(GQA)], axis=1)
    zq = jnp.zeros_like(qcat)
    qg = jnp.concatenate([jnp.where(g == 0, qcat, zq), jnp.where(g == 1, qcat, zq)], axis=0)

    qi = lax.broadcasted_iota(I32, (1, qw), 1) & (Q_BLOCK - 1)
    t_row = qb * Q_BLOCK + qi

    s = _dot(kcmp_ref[...], qg)
    off = pl.multiple_of(256 - 8 * qb, 8)
    s = s + jnp.concatenate([et_ref[r, pl.ds(off, n_cmp_rows), :] for r in range(GQA)], axis=1)
    n_i = lax.broadcasted_iota(I32, (n_cmp_rows, qw), 0)
    valid = n_i * CMP_STRIDE + (CMP_BLOCK - 1) <= t_row
    s = jnp.where(valid, s, NEG)
    m = jnp.max(s, axis=0, keepdims=True)
    p = jnp.where(valid, jnp.exp(s - m), 0.0)
    l = jnp.sum(p, axis=0, keepdims=True)
    p = p * jnp.where(l > 0.0, 1.0 / l, 0.0)
    o_cmp = _dot(vcmpt_ref[...], p.astype(BF16))
    p_sum = p[:, 0:Q_BLOCK]
    for r in range(1, GQA):
        p_sum = p_sum + p[:, r * Q_BLOCK:(r + 1) * Q_BLOCK]
    imp = jnp.dot(ovt_ref[...], p_sum, preferred_element_type=F32,
                  precision=lax.Precision.HIGHEST)

    j_i = lax.broadcasted_iota(I32, (n_slc, Q_BLOCK), 0)
    t1 = qb * Q_BLOCK + lax.broadcasted_iota(I32, (n_slc, Q_BLOCK), 1)
    cur = t1 >> 6
    forced = (j_i == 0) | (j_i == cur) | (j_i == cur - 1)
    imp = jnp.where(forced, FORCE, imp)
    imp = jnp.where(j_i <= cur, imp, NEG)
    cnt = jnp.zeros((n_slc, Q_BLOCK), F32)
    for jp in range(n_slc):
        row = imp[jp:jp + 1, :]
        beats = (row > imp) | ((row == imp) & (j_i > jp))
        cnt = cnt + jnp.where(beats, 1.0, 0.0)
    selneg = jnp.where(cnt < float(SLC_COUNT), 0.0, NEG).astype(BF16)
    qsel = [qcat, jnp.concatenate([selneg] * GQA, axis=1)]
    if n_slc < LANES - HEAD_DIM:
        qsel.append(jnp.zeros((LANES - HEAD_DIM - n_slc, qw), BF16))
    qsel = jnp.concatenate(qsel, axis=0)

    ki = lax.broadcasted_iota(I32, (Q_BLOCK, qw), 0)
    causal = ki <= qi
    tb0 = tb_ref[0]
    tb1 = tb_ref[1]
    has_prev = qb >= 1
    kt_prev = jnp.maximum(qb - 1, 0)

    def k_tile(ref, kt):
        return ref[pl.ds(pl.multiple_of(kt * Q_BLOCK, Q_BLOCK), Q_BLOCK), :]

    s = jnp.where(causal, _dot(k_tile(kslc_ref, qb), qsel) + tb0, NEG)
    m, l, acc = _softmax_init(s, vslct_ref[qb])
    s = jnp.where(has_prev, _dot(k_tile(kslc_ref, kt_prev), qsel) + tb1, NEG)
    m, l, acc = _softmax_step(s, vslct_ref[kt_prev], m, l, acc)

    def far_tile(kt, carry):
        return _softmax_step(_dot(k_tile(kslc_ref, kt), qsel), vslct_ref[kt], *carry)

    m, l, acc = lax.fori_loop(0, kt_prev, far_tile, (m, l, acc))
    o_slc = acc * (1.0 / l)

    s = jnp.where(causal, _dot(k_tile(kwin_ref, qb), qg) + tb0, NEG)
    m, l, acc = _softmax_init(s, vwint_ref[qb])
    s = jnp.where(has_prev, _dot(k_tile(kwin_ref, kt_prev), qg) + tb1, NEG)
    m, l, acc = _softmax_step(s, vwint_ref[kt_prev], m, l, acc)
    n_full = WINDOW // Q_BLOCK
    for d in range(2, n_full + 1):
        kt = jnp.maximum(qb - d, 0)
        keep = qb >= d
        if d == n_full:
            keep = keep & (ki > qi)
        s = jnp.where(keep, _dot(k_tile(kwin_ref, kt), qg), NEG)
        m, l, acc = _softmax_step(s, vwint_ref[kt], m, l, acc)
    o_win = acc * (1.0 / l)

    outs = []
    for r in range(GQA):
        cols = slice(r * Q_BLOCK, (r + 1) * Q_BLOCK)
        gate = [gt_ref[pl.ds(c * NSA_HEADS + g * GQA + r, 1), :] for c in range(3)]
        outs.append(gate[0] * o_cmp[:, cols] + gate[1] * o_slc[:, cols] + gate[2] * o_win[:, cols])
    out_ref[...] = jnp.concatenate(outs, axis=0).astype(BF16)


def _nsa(qt, kcmp, vcmpt, kslc, kwin, vt, gt, et, tb, batch, seq):
    nq = seq // Q_BLOCK
    n_cmp_rows = seq // CMP_STRIDE
    n_slc = seq // SLC_BLOCK
    n = batch * seq
    ovt = jnp.asarray(_overlap_t(n_cmp_rows, n_slc))
    gw = GQA * HEAD_DIM
    return pl.pallas_call(
        functools.partial(_nsa_kernel, n_cmp_rows=n_cmp_rows, n_slc=n_slc),
        grid=(batch, KV_HEADS, nq),
        in_specs=[
            pl.BlockSpec((gw, Q_BLOCK), lambda b, g, q: (g, b * nq + q)),
            pl.BlockSpec((None, n_cmp_rows, KV_WIDTH), lambda b, g, q: (b, 0, 0)),
            pl.BlockSpec((None, None, HEAD_DIM, n_cmp_rows), lambda b, g, q: (b, g, 0, 0)),
            pl.BlockSpec((None, seq, LANES), lambda b, g, q: (g, b, 0)),
            pl.BlockSpec((seq, LANES), lambda b, g, q: (b, 0)),
            pl.BlockSpec((None, nq, HEAD_DIM, LANES), lambda b, g, q: (g, b, 0, 0)),
            pl.BlockSpec((None, nq, HEAD_DIM, LANES), lambda b, g, q: (KV_HEADS + g, b, 0, 0)),
            pl.BlockSpec((GATE_ROWS, Q_BLOCK), lambda b, g, q: (0, b * nq + q)),
            pl.BlockSpec((GQA, 512, Q_BLOCK), lambda b, g, q: (g, 0, 0)),
            pl.BlockSpec((None, 2, Q_BLOCK, GQA * LANES), lambda b, g, q: (g, 0, 0, 0)),
            pl.BlockSpec((n_slc, n_cmp_rows), lambda b, g, q: (0, 0)),
        ],
        out_specs=pl.BlockSpec((gw, Q_BLOCK), lambda b, g, q: (g, b * nq + q)),
        out_shape=jax.ShapeDtypeStruct((NSA_WIDTH, n), BF16),
        compiler_params=pltpu.CompilerParams(vmem_limit_bytes=VMEM_LIMIT),
        name="nsa",
    )(qt, kcmp, vcmpt, kslc, kwin, vt, vt, gt, et, tb, ovt)


def _layer_norm_fm(h, g_ref, b_ref, reps):
    mu = jnp.mean(h, axis=0, keepdims=True)
    hc = h - mu
    var = jnp.mean(hc * hc, axis=0, keepdims=True)
    gain = jnp.concatenate([g_ref[...]] * reps, axis=1)
    bias = jnp.concatenate([b_ref[...]] * reps, axis=1)
    return hc * lax.rsqrt(var + LN_EPS) * gain + bias


def _mix_kernel(nsat_ref, zt_ref, halo_ref, x_ref, won_ref, wop_ref, pwt_ref, ps_ref, g_ref, b_ref,
                h_ref, hb_ref, *, tn, seq):
    t0 = (pl.program_id(0) * tn) % seq
    reps = tn // LANES
    zc = jnp.concatenate([halo_ref[...], zt_ref[...]], axis=1)
    t_ext = t0 - LANES + lax.broadcasted_iota(I32, (1, LANES + tn), 1)
    zc = jnp.where(t_ext >= 0, zc, 0.0)
    pooled = []
    for gi, win in enumerate(POOL_WINDOWS):
        zg = zc[gi * POOL_GROUP_DIM:(gi + 1) * POOL_GROUP_DIM]
        s = zg
        sh = 1
        while sh < win:
            s = s + pltpu.roll(s, sh, axis=1)
            sh *= 2
        cnt = jnp.clip(t_ext + 1, 1, win).astype(F32)
        y = (s / cnt - zg)[:, LANES:]
        scale = jnp.concatenate([ps_ref[gi * POOL_GROUP_DIM:(gi + 1) * POOL_GROUP_DIM]] * reps, axis=1)
        pooled.append((_dot(pwt_ref[gi], y.astype(BF16)) * scale).astype(BF16))
    pool_t = jnp.concatenate(pooled, axis=0)
    mix = _dot(won_ref[...], nsat_ref[...]) + _dot(wop_ref[...], pool_t)
    h = ALPHA * x_ref[...].T + mix
    h = _layer_norm_fm(h, g_ref, b_ref, reps)
    h_ref[...] = h
    hb_ref[...] = h.astype(BF16)


def _lane_bcast(v):
    return jnp.broadcast_to(v.reshape(-1, 1), (v.shape[0], LANES)).astype(F32)


def _mix(nsat, zt, x2, w_out, pool_w, pool_scale, ln_g, ln_b, seq):
    n = x2.shape[0]
    tn = 256
    wot = w_out.T.astype(BF16)
    full = lambda *s: pl.BlockSpec(s, lambda i: (0,) * len(s))
    return pl.pallas_call(
        functools.partial(_mix_kernel, tn=tn, seq=seq),
        grid=(n // tn,),
        in_specs=[
            pl.BlockSpec((NSA_WIDTH, tn), lambda i: (0, i)),
            pl.BlockSpec((POOL_WIDTH, tn), lambda i: (0, i)),
            pl.BlockSpec((POOL_WIDTH, LANES), lambda i: (0, jnp.maximum(i * (tn // LANES) - 1, 0))),
            pl.BlockSpec((tn, D_MODEL), lambda i: (i, 0)),
            full(D_MODEL, NSA_WIDTH), full(D_MODEL, POOL_WIDTH),
            full(len(POOL_WINDOWS), POOL_GROUP_DIM, POOL_GROUP_DIM),
            full(POOL_WIDTH, LANES), full(D_MODEL, LANES), full(D_MODEL, LANES),
        ],
        out_specs=[pl.BlockSpec((D_MODEL, tn), lambda i: (0, i))] * 2,
        out_shape=[jax.ShapeDtypeStruct((D_MODEL, n), F32), jax.ShapeDtypeStruct((D_MODEL, n), BF16)],
        compiler_params=pltpu.CompilerParams(vmem_limit_bytes=VMEM_LIMIT),
        name="mix",
    )(nsat, zt, zt, x2, wot[:, :NSA_WIDTH], wot[:, NSA_WIDTH:],
      jnp.swapaxes(pool_w, 1, 2).astype(BF16), _lane_bcast(pool_scale), _lane_bcast(ln_g), _lane_bcast(ln_b))


def _top16_rows(s):
    rows = lax.broadcasted_iota(I32, s.shape, 0).astype(F32)
    cur = s
    rank = jnp.full(s.shape, float(PEER_TOPK), F32)
    vals = []
    for r in range(PEER_TOPK):
        mx = jnp.max(cur, axis=0, keepdims=True)
        first = jnp.min(jnp.where(cur == mx, rows, float(s.shape[0])), axis=0, keepdims=True)
        hit = rows == first
        vals.append(mx)
        rank = jnp.where(hit, float(r), rank)
        cur = jnp.where(hit, -jnp.inf, cur)
    return vals, rank


def _route_kernel(hb_ref, wqt_ref, sk1_ref, sk2_ref, n_ref, a_ref, r2_ref, b_ref, *, tn):
    qt = _dot(wqt_ref[...], hb_ref[...])
    cands = _peer_candidates()
    n_pad = -(-len(cands) // 8) * 8
    c_i = lax.broadcasted_iota(I32, (n_pad, tn), 0)
    half = PEER_QDIM // 2
    for h in range(PEER_HEADS):
        q1 = qt[h * PEER_QDIM:h * PEER_QDIM + half].astype(BF16)
        q2 = qt[h * PEER_QDIM + half:(h + 1) * PEER_QDIM].astype(BF16)
        s1 = _dot(sk1_ref[...], q1)
        s2 = _dot(sk2_ref[...], q2)
        v1, rank1 = _top16_rows(s1)
        v2, rank2 = _top16_rows(s2)
        sums = [v1[a] + v2[b] for a, b in cands]
        cs = jnp.full((n_pad, tn), -jnp.inf, F32)
        for c in range(len(cands)):
            cs = jnp.where(c_i == c, sums[c], cs)
        cnt = jnp.zeros((n_pad, tn), F32)
        for c in range(len(cands)):
            row = sums[c]
            beats = (row > cs) | ((row == cs) & (c_i > c))
            cnt = cnt + jnp.where(beats, 1.0, 0.0)
        sel = cnt < float(PEER_TOPK)
        z = jnp.sum(jnp.where(sel, jnp.exp(cs - sums[0]), 0.0), axis=0, keepdims=True)
        self_f = jnp.where(sel, 1.0, 0.0)
        nmap = jnp.zeros((PEER_KEYS, tn), F32)
        pos = 0
        for a in range(PEER_TOPK):
            nb = PEER_TOPK // (a + 1)
            in_a = (c_i >= pos) & (c_i < pos + nb)
            n_a = jnp.sum(jnp.where(in_a, self_f, 0.0), axis=0, keepdims=True)
            pos += nb
            nmap = jnp.where(rank1 == float(a), n_a, nmap)
        n_ref[h] = nmap
        a_ref[h] = jnp.where(rank1 < float(PEER_TOPK), jnp.exp(s1 - v1[0]), 0.0)
        r2_ref[h] = rank2.astype(BF16)
        b_ref[h] = (jnp.where(rank2 < float(PEER_TOPK), jnp.exp(s2 - v2[0]), 0.0) / z).astype(BF16)


def _route(hb, wq, subkeys):
    n = hb.shape[1]
    tn = 256
    full = lambda *s: pl.BlockSpec(s, lambda i: (0,) * len(s))
    tab = lambda: pl.BlockSpec((PEER_HEADS, PEER_KEYS, tn), lambda i: (0, 0, i))
    shape = (PEER_HEADS, PEER_KEYS, n)
    return pl.pallas_call(
        functools.partial(_route_kernel, tn=tn),
        grid=(n // tn,),
        in_specs=[
            pl.BlockSpec((D_MODEL, tn), lambda i: (0, i)),
            full(PEER_HEADS * PEER_QDIM, D_MODEL), full(PEER_KEYS, PEER_QDIM // 2), full(PEER_KEYS, PEER_QDIM // 2),
        ],
        out_specs=[tab(), tab(), tab(), tab()],
        out_shape=[
            jax.ShapeDtypeStruct(shape, F32),
            jax.ShapeDtypeStruct(shape, F32),
            jax.ShapeDtypeStruct(shape, BF16),
            jax.ShapeDtypeStruct(shape, BF16),
        ],
        compiler_params=pltpu.CompilerParams(vmem_limit_bytes=VMEM_LIMIT),
        name="peer_route",
    )(hb, wq.T.astype(BF16), subkeys[0].astype(BF16), subkeys[1].astype(BF16))


def _experts_kernel(hb_ref, u_ref, vt_ref, n_ref, a_ref, r2_ref, b_ref, h_ref, g_ref, bb_ref,
                    out_ref, acc_ref, *, tn, te):
    j = pl.program_id(1)

    @pl.when(j == 0)
    def _():
        acc_ref[...] = jnp.zeros_like(acc_ref)

    act = _gelu(_dot(u_ref[...], hb_ref[...]))
    chunks = te // PEER_KEYS
    zero = jnp.zeros((PEER_KEYS, tn), BF16)
    ps = []
    for c in range(chunks):
        i1 = j * chunks + c
        w = zero
        for h in range(PEER_HEADS):
            n_row = n_ref[h, pl.ds(i1, 1), :].astype(BF16)
            a_row = a_ref[h, pl.ds(i1, 1), :].astype(BF16)
            w = w + a_row * jnp.where(r2_ref[h] < n_row, b_ref[h], zero)
        ps.append(w * act[c * PEER_KEYS:(c + 1) * PEER_KEYS].astype(BF16))
    acc_ref[...] += _dot(vt_ref[...], jnp.concatenate(ps, axis=0))

    @pl.when(j == pl.num_programs(1) - 1)
    def _():
        h = ALPHA * h_ref[...] + acc_ref[...]
        out_ref[...] = _layer_norm_fm(h, g_ref, bb_ref, tn // LANES).T


def _experts(hb, hf, u_b, vt_b, tabs, ln_g, ln_b):
    n = hb.shape[1]
    tn, te = 512, 512
    tab = lambda: pl.BlockSpec((PEER_HEADS, PEER_KEYS, tn), lambda i, j: (0, 0, i))
    return pl.pallas_call(
        functools.partial(_experts_kernel, tn=tn, te=te),
        grid=(n // tn, PEER_EXPERTS // te),
        in_specs=[
            pl.BlockSpec((D_MODEL, tn), lambda i, j: (0, i)),
            pl.BlockSpec((te, D_MODEL), lambda i, j: (j, 0)),
            pl.BlockSpec((D_MODEL, te), lambda i, j: (0, j)),
            tab(), tab(), tab(), tab(),
            pl.BlockSpec((D_MODEL, tn), lambda i, j: (0, i)),
            pl.BlockSpec((D_MODEL, LANES), lambda i, j: (0, 0)),
            pl.BlockSpec((D_MODEL, LANES), lambda i, j: (0, 0)),
        ],
        out_specs=pl.BlockSpec((tn, D_MODEL), lambda i, j: (i, 0)),
        out_shape=jax.ShapeDtypeStruct((n, D_MODEL), F32),
        scratch_shapes=[pltpu.VMEM((D_MODEL, tn), F32)],
        compiler_params=pltpu.CompilerParams(
            dimension_semantics=("arbitrary", "arbitrary"), vmem_limit_bytes=VMEM_LIMIT),
        name="peer_experts",
    )(hb, u_b, vt_b, *tabs, hf, _lane_bcast(ln_g), _lane_bcast(ln_b))


def kernel(x, w_in, rel_table, cmp_pe_k, cmp_pe_v, cmp_k_w1, cmp_k_w2, cmp_v_w1, cmp_v_w2, pool_w, pool_scale,
           w_out, ln1_g, ln1_b, peer_wq, peer_subkeys, peer_u, peer_v, ln2_g, ln2_b):
    batch, seq, _ = x.shape
    assert w_in.shape[0] == DEPTH == 1 and seq % 512 == 0 and seq // CMP_STRIDE <= 256
    x2 = x.reshape(batch * seq, D_MODEL)
    et, tb = _bias_tables(rel_table)
    qt, vt, gt, zt, kvc, kslc, kwin = _inproj(x2, w_in[0], seq)
    kcmp, vcmpt = _compress(kvc, batch, seq, cmp_pe_k[0], cmp_pe_v[0], cmp_k_w1[0], cmp_k_w2[0],
                            cmp_v_w1[0], cmp_v_w2[0])
    nsat = _nsa(qt, kcmp, vcmpt, kslc, kwin, vt, gt, et, tb, batch, seq)
    hf, hb = _mix(nsat, zt, x2, w_out[0], pool_w[0], pool_scale[0], ln1_g[0], ln1_b[0], seq)
    tabs = _route(hb, peer_wq[0], peer_subkeys[0])
    y = _experts(hb, hf, peer_u[0].astype(BF16), peer_v[0].T.astype(BF16), tabs, ln2_g[0], ln2_b[0])
    return y.reshape(batch, seq, D_MODEL)
```

```python
import functools
import math

import numpy as np
import jax
import jax.numpy as jnp
from jax import lax
from jax.experimental import pallas as pl
from jax.experimental.pallas import tpu as pltpu

F32 = jnp.float32
BF16 = jnp.bfloat16
I32 = jnp.int32

D_MODEL = 1024
NSA_HEADS = 8
KV_HEADS = 2
GQA = NSA_HEADS // KV_HEADS
HEAD_DIM = 64
NSA_WIDTH = NSA_HEADS * HEAD_DIM
KV_WIDTH = KV_HEADS * HEAD_DIM
POOL_WIDTH = D_MODEL - NSA_WIDTH
POOL_WINDOWS = (2, 4, 8, 16)
POOL_GROUP_DIM = POOL_WIDTH // len(POOL_WINDOWS)
N_GATES = 3 * NSA_HEADS
GATE_ROWS = 32
CMP_BLOCK = 32
CMP_STRIDE = 16
CMP_HIDDEN = 256
SLC_BLOCK = 64
SLC_COUNT = 16
WINDOW = 512
Q_BLOCK = 128
NEG = -1e30
FORCE = 1e6
N_BUCKETS = 32
MAX_DISTANCE = 128
PEER_HEADS = 8
PEER_KEYS = 128
PEER_EXPERTS = PEER_KEYS * PEER_KEYS
PEER_QDIM = 256
PEER_TOPK = 16
DEPTH = 1
ALPHA = (2 * DEPTH) ** 0.25
LN_EPS = 1e-5
SQRT_HALF = 0.7071067811865476

LANES = 128
VMEM_LIMIT = 52 * 1024 * 1024

_FM_Q, _FM_V, _FM_G, _FM_Z, _FM_END = 0, 512, 768, 800, 1312
_TM_WIDTH = 640


def _gelu(x):
    return 0.5 * x * (1.0 + lax.erf(x * SQRT_HALF))


def _dot(a, b):
    return jnp.dot(a, b, preferred_element_type=F32)


def _dot_nt(a, b):
    return lax.dot_general(a, b, (((1,), (1,)), ((), ())), preferred_element_type=F32)


def _bucket_np(dist):
    dist = np.maximum(dist, 0)
    max_exact = N_BUCKETS // 2
    large = max_exact + np.floor(
        np.log(np.maximum(dist, 1) / max_exact) / math.log(MAX_DISTANCE / max_exact)
        * (N_BUCKETS - max_exact)).astype(np.int64)
    large = np.minimum(large, N_BUCKETS - 1)
    return np.where(dist < max_exact, dist, large)


ET_ROWS = 512
XS_ROWS = 896
XS_MASKED_ROW = 640
XW_ROWS = 1152
_CODE_MASKED = N_BUCKETS


def _bias_index_tables():
    qi = np.arange(Q_BLOCK)[None, :]

    def codes(dist, visible):
        return np.where(visible, _bucket_np(dist), _CODE_MASKED).astype(np.int32)

    d_cmp = qi - CMP_STRIDE * (np.arange(ET_ROWS)[:, None] - 256) - (CMP_BLOCK - 1)
    rho = np.arange(XS_ROWS)[:, None]
    d_sel = qi - (rho - 384)
    d_win = qi - (np.arange(XW_ROWS)[:, None] - 512)
    return np.concatenate([
        codes(d_cmp, d_cmp >= 0),
        codes(d_sel, (d_sel >= 0) & (rho < XS_MASKED_ROW)),
        codes(d_win, (d_win >= 0) & (d_win < WINDOW)),
    ], axis=0)


def _overlap_t(n_cmp_rows, n_slc):
    n = np.arange(n_cmp_rows)[None, :]
    j = np.arange(n_slc)[:, None]
    start = n * CMP_STRIDE
    end = start + CMP_BLOCK - 1
    return ((start <= j * SLC_BLOCK + SLC_BLOCK - 1) & (end >= j * SLC_BLOCK)).astype(np.float32)


def _peer_candidates():
    return [(a, b) for a in range(PEER_TOPK) for b in range(PEER_TOPK)
            if (a + 1) * (b + 1) <= PEER_TOPK]


def _bias_kernel(tab_ref, idx_ref, et_ref, xs_ref, xw_ref):
    g = pl.program_id(0)
    chunk = Q_BLOCK
    for r in range(GQA):
        h = g * GQA + r
        far = tab_ref[N_BUCKETS - 1, h]
        cols = slice(r * LANES, (r + 1) * LANES)
        for c in range((ET_ROWS + XS_ROWS + XW_ROWS) // chunk):
            idx = idx_ref[c * chunk:(c + 1) * chunk, :]
            acc = jnp.where(idx == _CODE_MASKED, NEG, 0.0)
            for b in range(N_BUCKETS - 1):
                acc = jnp.where(idx == b, tab_ref[b, h] - far, acc)
            row = c * chunk
            if row < ET_ROWS:
                et_ref[r, row:row + chunk, :] = acc
            elif row < ET_ROWS + XS_ROWS:
                xs_ref[0, row - ET_ROWS:row - ET_ROWS + chunk, cols] = acc
            else:
                row -= ET_ROWS + XS_ROWS
                xw_ref[0, row:row + chunk, cols] = acc


def _bias_tables(rel_table):
    idx = _bias_index_tables()
    qw = GQA * LANES
    return pl.pallas_call(
        _bias_kernel,
        grid=(KV_HEADS,),
        in_specs=[
            pl.BlockSpec(memory_space=pltpu.SMEM),
            pl.BlockSpec(idx.shape, lambda g: (0, 0)),
        ],
        out_specs=[
            pl.BlockSpec((GQA, ET_ROWS, Q_BLOCK), lambda g: (g, 0, 0)),
            pl.BlockSpec((1, XS_ROWS, qw), lambda g: (g, 0, 0)),
            pl.BlockSpec((1, XW_ROWS, qw), lambda g: (g, 0, 0)),
        ],
        out_shape=[
            jax.ShapeDtypeStruct((NSA_HEADS, ET_ROWS, Q_BLOCK), F32),
            jax.ShapeDtypeStruct((KV_HEADS, XS_ROWS, qw), F32),
            jax.ShapeDtypeStruct((KV_HEADS, XW_ROWS, qw), F32),
        ],
        name="bias_tables",
    )(rel_table, jnp.asarray(idx))


def _inproj_kernel(x_ref, wfm_ref, wtm_ref, qt_ref, vt_ref, gt_ref, zt_ref, kvc_ref,
                   kslc_ref, kwin_ref, *, tn, seq):
    xb = x_ref[...].astype(BF16)
    fm = _dot_nt(wfm_ref[...], xb)
    tm = _dot(xb, wtm_ref[...])
    qt_ref[...] = (fm[_FM_Q:_FM_V] * (HEAD_DIM ** -0.5)).astype(BF16)
    for c in range(2 * KV_HEADS):
        for s in range(tn // LANES):
            vt_ref[c, s] = fm[_FM_V + c * HEAD_DIM:_FM_V + (c + 1) * HEAD_DIM,
                              s * LANES:(s + 1) * LANES].astype(BF16)
    gt_ref[...] = jax.nn.sigmoid(fm[_FM_G:_FM_Z])
    zt_ref[...] = fm[_FM_Z:_FM_END]
    kvc_ref[0] = tm[:, 0:KV_WIDTH]
    kvc_ref[1] = tm[:, KV_WIDTH:2 * KV_WIDTH]
    t0 = (pl.program_id(0) * tn) % seq
    row = lax.broadcasted_iota(I32, (tn, LANES), 0) + t0
    lane = lax.broadcasted_iota(I32, (tn, LANES), 1)
    onehot = jnp.where(lane - HEAD_DIM == (row >> 6), 1.0, 0.0)
    for g in range(KV_HEADS):
        k = tm[:, 256 + g * LANES:256 + (g + 1) * LANES]
        kslc_ref[g] = jnp.where(lane < HEAD_DIM, k, onehot).astype(BF16)
    kwin_ref[...] = tm[:, 512:640].astype(BF16)


def _inproj(x2, w_in, seq):
    n = x2.shape[0]
    tn = 512
    w = w_in
    o = NSA_WIDTH
    w_q = w[:, 0:o]
    w_kc, w_vc, w_ks, w_vs, w_kw, w_vw = [w[:, o + i * KV_WIDTH:o + (i + 1) * KV_WIDTH] for i in range(6)]
    o += 6 * KV_WIDTH
    w_g = w[:, o:o + N_GATES]
    w_z = w[:, o + N_GATES:]
    zpad = jnp.zeros((D_MODEL, HEAD_DIM), w.dtype)
    wfm = jnp.concatenate(
        [w_q, w_vs, w_vw, w_g, jnp.zeros((D_MODEL, GATE_ROWS - N_GATES), w.dtype), w_z], axis=1).T.astype(BF16)
    wtm = jnp.concatenate(
        [w_kc, w_vc, w_ks[:, :HEAD_DIM], zpad, w_ks[:, HEAD_DIM:], zpad, w_kw], axis=1).astype(BF16)
    nt = n // LANES
    return pl.pallas_call(
        functools.partial(_inproj_kernel, tn=tn, seq=seq),
        grid=(n // tn,),
        in_specs=[
            pl.BlockSpec((tn, D_MODEL), lambda i: (i, 0)),
            pl.BlockSpec((_FM_END, D_MODEL), lambda i: (0, 0)),
            pl.BlockSpec((D_MODEL, _TM_WIDTH), lambda i: (0, 0)),
        ],
        out_specs=[
            pl.BlockSpec((NSA_WIDTH, tn), lambda i: (0, i)),
            pl.BlockSpec((2 * KV_HEADS, tn // LANES, HEAD_DIM, LANES), lambda i: (0, i, 0, 0)),
            pl.BlockSpec((GATE_ROWS, tn), lambda i: (0, i)),
            pl.BlockSpec((POOL_WIDTH, tn), lambda i: (0, i)),
            pl.BlockSpec((2, tn, KV_WIDTH), lambda i: (0, i, 0)),
            pl.BlockSpec((KV_HEADS, tn, LANES), lambda i: (0, i, 0)),
            pl.BlockSpec((tn, LANES), lambda i: (i, 0)),
        ],
        out_shape=[
            jax.ShapeDtypeStruct((NSA_WIDTH, n), BF16),
            jax.ShapeDtypeStruct((2 * KV_HEADS, nt, HEAD_DIM, LANES), BF16),
            jax.ShapeDtypeStruct((GATE_ROWS, n), F32),
            jax.ShapeDtypeStruct((POOL_WIDTH, n), F32),
            jax.ShapeDtypeStruct((2, n, KV_WIDTH), F32),
            jax.ShapeDtypeStruct((KV_HEADS, n, LANES), BF16),
            jax.ShapeDtypeStruct((n, LANES), BF16),
        ],
        compiler_params=pltpu.CompilerParams(vmem_limit_bytes=VMEM_LIMIT),
        name="inproj",
    )(x2, wfm, wtm)


def _compress_kernel(kc_ref, vc_ref, w2k_ref, w2v_ref, pek_ref, pev_ref, w1k_ref, w1v_ref,
                     wok0_ref, wok1_ref, wovt_ref, kcmp_ref, vcmpt_ref, *, n_rows):
    half = CMP_BLOCK // 2
    acc_k = jnp.zeros((n_rows, 4 * CMP_HIDDEN), F32)
    acc_v = jnp.zeros((n_rows, 4 * CMP_HIDDEN), F32)
    for l in range(half):
        rows = pl.ds(l, n_rows, stride=CMP_STRIDE)
        acc_k = acc_k + _dot(kc_ref[rows, :].astype(BF16), w2k_ref[l])
        acc_v = acc_v + _dot(vc_ref[rows, :].astype(BF16), w2v_ref[l])
    ck = _dot(pek_ref[...], w1k_ref[...])[0:1]
    cv = _dot(pev_ref[...], w1v_ref[...])[0:1]

    def hidden(acc, c, g):
        a = acc[:, g * CMP_HIDDEN:(g + 1) * CMP_HIDDEN]
        b = acc[:, (2 + g) * CMP_HIDDEN:(3 + g) * CMP_HIDDEN]
        return _gelu(a + pltpu.roll(b, n_rows - 1, axis=0) + c).astype(BF16)

    kcmp_ref[...] = (_dot(hidden(acc_k, ck, 0), wok0_ref[...])
                     + _dot(hidden(acc_k, ck, 1), wok1_ref[...])).astype(BF16)
    for g in range(KV_HEADS):
        vcmpt_ref[g] = _dot_nt(wovt_ref[...], hidden(acc_v, cv, g)).astype(BF16)


def _compress(kvc, batch, seq, pe_k, pe_v, k_w1, k_w2, v_w1, v_w2):
    n_rows = seq // CMP_STRIDE
    half = CMP_BLOCK // 2

    def blockdiag(w1):
        w = w1.reshape(CMP_BLOCK, HEAD_DIM, CMP_HIDDEN)
        z = jnp.zeros((half, HEAD_DIM, CMP_HIDDEN), w1.dtype)
        top = jnp.concatenate([w[:half], z, w[half:], z], axis=2)
        bot = jnp.concatenate([z, w[:half], z, w[half:]], axis=2)
        return jnp.concatenate([top, bot], axis=1).astype(BF16)

    def pe_rows(pe):
        flat = pe.reshape(1, CMP_BLOCK * HEAD_DIM)
        return jnp.concatenate([flat, jnp.zeros((7, flat.shape[1]), pe.dtype)], axis=0).astype(BF16)

    zo = jnp.zeros((CMP_HIDDEN, HEAD_DIM), k_w2.dtype)
    wok0 = jnp.concatenate([k_w2, zo], axis=1).astype(BF16)
    wok1 = jnp.concatenate([zo, k_w2], axis=1).astype(BF16)
    full = lambda *s: pl.BlockSpec(s, lambda b: (0,) * len(s))
    return pl.pallas_call(
        functools.partial(_compress_kernel, n_rows=n_rows),
        grid=(batch,),
        in_specs=[
            pl.BlockSpec((None, seq, KV_WIDTH), lambda b: (0, b, 0)),
            pl.BlockSpec((None, seq, KV_WIDTH), lambda b: (1, b, 0)),
            full(half, KV_WIDTH, 4 * CMP_HIDDEN), full(half, KV_WIDTH, 4 * CMP_HIDDEN),
            full(8, CMP_BLOCK * HEAD_DIM), full(8, CMP_BLOCK * HEAD_DIM),
            full(CMP_BLOCK * HEAD_DIM, CMP_HIDDEN), full(CMP_BLOCK * HEAD_DIM, CMP_HIDDEN),
            full(CMP_HIDDEN, KV_WIDTH), full(CMP_HIDDEN, KV_WIDTH), full(HEAD_DIM, CMP_HIDDEN),
        ],
        out_specs=[
            pl.BlockSpec((None, n_rows, KV_WIDTH), lambda b: (b, 0, 0)),
            pl.BlockSpec((None, KV_HEADS, HEAD_DIM, n_rows), lambda b: (b, 0, 0, 0)),
        ],
        out_shape=[
            jax.ShapeDtypeStruct((batch, n_rows, KV_WIDTH), BF16),
            jax.ShapeDtypeStruct((batch, KV_HEADS, HEAD_DIM, n_rows), BF16),
        ],
        compiler_params=pltpu.CompilerParams(vmem_limit_bytes=VMEM_LIMIT),
        name="compress",
    )(kvc, kvc, blockdiag(k_w1), blockdiag(v_w1), pe_rows(pe_k), pe_rows(pe_v),
      k_w1.astype(BF16), v_w1.astype(BF16), wok0, wok1, v_w2.T.astype(BF16))


SUPER_TILE = 2 * Q_BLOCK
WIN_KEYS = WINDOW + Q_BLOCK


def _softmax_init(s, vt):
    m = jnp.max(s, axis=0, keepdims=True)
    p = jnp.exp(s - m)
    return m, jnp.sum(p, axis=0, keepdims=True), _dot(vt, p.astype(BF16))


def _softmax_step(s, vt, state):
    m, l, acc = state
    m_new = jnp.maximum(m, jnp.max(s, axis=0, keepdims=True))
    alpha = jnp.exp(m - m_new)
    p = jnp.exp(s - m_new)
    return m_new, alpha * l + jnp.sum(p, axis=0, keepdims=True), alpha * acc + _dot(vt, p.astype(BF16))


def _softmax_merge(a, b):
    m = jnp.maximum(a[0], b[0])
    wa = jnp.exp(a[0] - m)
    wb = jnp.exp(b[0] - m)
    return (wa * a[2] + wb * b[2]) * (1.0 / (wa * a[1] + wb * b[1]))


def _nsa_kernel(qt_ref, kcmp_ref, vcmpt_ref, kslc_ref, kwin_ref, vslct_ref, vwint_ref, gt_ref,
                et_ref, xs_ref, xw_ref, ovt_ref, out_ref, *, n_cmp_rows, n_slc):
    g = pl.program_id(1)
    qb = pl.program_id(2)
    qw = GQA * Q_BLOCK
    q4 = qt_ref[...]
    qcat = jnp.concatenate([q4[r * HEAD_DIM:(r + 1) * HEAD_DIM] for r in range(GQA)], axis=1)
    zq = jnp.zeros_like(qcat)
    qg = jnp.concatenate([jnp.where(g == 0, qcat, zq), jnp.where(g == 1, qcat, zq)], axis=0)

    s = _dot(kcmp_ref[...], qg)
    k0 = jnp.maximum(qb * Q_BLOCK - WINDOW, 0)
    row_w = pl.multiple_of(WINDOW - qb * Q_BLOCK + k0, Q_BLOCK)
    s_win = _dot(kwin_ref[pl.ds(pl.multiple_of(k0, Q_BLOCK), WIN_KEYS), :], qg)

    off = pl.multiple_of(256 - 8 * qb, 8)
    s = s + jnp.concatenate([et_ref[r, pl.ds(off, n_cmp_rows), :] for r in range(GQA)], axis=1)
    m = jnp.max(s, axis=0, keepdims=True)
    p = jnp.exp(s - m)
    l = jnp.sum(p, axis=0, keepdims=True)
    p = p * jnp.where(m > 0.5 * NEG, 1.0 / l, 0.0)
    o_cmp = _dot(vcmpt_ref[...], p.astype(BF16))
    p_sum = p[:, 0:Q_BLOCK]
    for r in range(1, GQA):
        p_sum = p_sum + p[:, r * Q_BLOCK:(r + 1) * Q_BLOCK]
    imp = jnp.dot(ovt_ref[...], p_sum, preferred_element_type=F32,
                  precision=lax.Precision.HIGHEST)

    kt0 = k0 >> 7
    vt = jnp.concatenate([vwint_ref[kt0 + i] for i in range(WIN_KEYS // Q_BLOCK)], axis=1)
    m, l, acc = _softmax_init(s_win + xw_ref[pl.ds(row_w, WIN_KEYS), :], vt)
    o_win = acc * (1.0 / l)

    j_i = lax.broadcasted_iota(I32, (n_slc, Q_BLOCK), 0)
    t1 = qb * Q_BLOCK + lax.broadcasted_iota(I32, (n_slc, Q_BLOCK), 1)
    cur = t1 >> 6
    forced = (j_i == 0) | (j_i == cur) | (j_i == cur - 1)
    imp = jnp.where(forced, FORCE, imp)
    imp = jnp.where(j_i <= cur, imp, NEG)
    cnt = jnp.zeros((n_slc, Q_BLOCK), F32)
    for jp in range(n_slc):
        row = imp[jp:jp + 1, :]
        beats = (row > imp) | ((row == imp) & (j_i > jp))
        cnt = cnt + jnp.where(beats, 1.0, 0.0)
    selneg = jnp.where(cnt < float(SLC_COUNT), 0.0, NEG).astype(BF16)
    qsel = [qcat, jnp.concatenate([selneg] * GQA, axis=1)]
    if n_slc < LANES - HEAD_DIM:
        qsel.append(jnp.zeros((LANES - HEAD_DIM - n_slc, qw), BF16))
    qsel = jnp.concatenate(qsel, axis=0)

    def sel_scores(st):
        k = kslc_ref[pl.ds(pl.multiple_of(st * SUPER_TILE, SUPER_TILE), SUPER_TILE), :]
        vt = jnp.concatenate([vslct_ref[2 * st], vslct_ref[2 * st + 1]], axis=1)
        return _dot(k, qsel), vt

    odd = qb & 1
    st_diag = qb >> 1
    st_prev = jnp.maximum(st_diag - 1, 0)
    n_far = st_prev
    row_a = pl.multiple_of(384 - Q_BLOCK * odd, Q_BLOCK)
    row_b = pl.multiple_of(jnp.where(st_diag >= 1, Q_BLOCK - Q_BLOCK * odd, XS_MASKED_ROW), Q_BLOCK)
    s_a, vt_a = sel_scores(st_diag)
    s_b, vt_b = sel_scores(st_prev)
    chain_a = _softmax_init(s_a + xs_ref[pl.ds(row_a, SUPER_TILE), :], vt_a)
    chain_b = _softmax_init(s_b + xs_ref[pl.ds(row_b, SUPER_TILE), :], vt_b)

    def far_pair(i, chains):
        a, b = chains
        s_a, vt_a = sel_scores(2 * i)
        s_b, vt_b = sel_scores(2 * i + 1)
        return _softmax_step(s_a, vt_a, a), _softmax_step(s_b, vt_b, b)

    chain_a, chain_b = lax.fori_loop(0, n_far >> 1, far_pair, (chain_a, chain_b))
    chain_a = lax.cond((n_far & 1) == 1,
                       lambda a: _softmax_step(*sel_scores(n_far - 1), a), lambda a: a, chain_a)
    o_slc = _softmax_merge(chain_a, chain_b)

    outs = []
    for r in range(GQA):
        cols = slice(r * Q_BLOCK, (r + 1) * Q_BLOCK)
        gate = [gt_ref[pl.ds(c * NSA_HEADS + g * GQA + r, 1), :] for c in range(3)]
        outs.append(gate[0] * o_cmp[:, cols] + gate[1] * o_slc[:, cols] + gate[2] * o_win[:, cols])
    out_ref[...] = jnp.concatenate(outs, axis=0).astype(BF16)


def _nsa(qt, kcmp, vcmpt, kslc, kwin, vt, gt, et, xs, xw, batch, seq):
    assert seq % SUPER_TILE == 0 and seq >= WIN_KEYS and seq // CMP_STRIDE <= 256 and seq // SLC_BLOCK <= HEAD_DIM
    nq = seq // Q_BLOCK
    n_cmp_rows = seq // CMP_STRIDE
    n_slc = seq // SLC_BLOCK
    n = batch * seq
    ovt = jnp.asarray(_overlap_t(n_cmp_rows, n_slc))
    gw = GQA * HEAD_DIM
    return pl.pallas_call(
        functools.partial(_nsa_kernel, n_cmp_rows=n_cmp_rows, n_slc=n_slc),
        grid=(batch, KV_HEADS, nq),
        in_specs=[
            pl.BlockSpec((gw, Q_BLOCK), lambda b, g, q: (g, b * nq + q)),
            pl.BlockSpec((None, n_cmp_rows, KV_WIDTH), lambda b, g, q: (b, 0, 0)),
            pl.BlockSpec((None, None, HEAD_DIM, n_cmp_rows), lambda b, g, q: (b, g, 0, 0)),
            pl.BlockSpec((None, seq, LANES), lambda b, g, q: (g, b, 0)),
            pl.BlockSpec((seq, LANES), lambda b, g, q: (b, 0)),
            pl.BlockSpec((None, nq, HEAD_DIM, LANES), lambda b, g, q: (g, b, 0, 0)),
            pl.BlockSpec((None, nq, HEAD_DIM, LANES), lambda b, g, q: (KV_HEADS + g, b, 0, 0)),
            pl.BlockSpec((GATE_ROWS, Q_BLOCK), lambda b, g, q: (0, b * nq + q)),
            pl.BlockSpec((GQA, ET_ROWS, Q_BLOCK), lambda b, g, q: (g, 0, 0)),
            pl.BlockSpec((None, XS_ROWS, GQA * LANES), lambda b, g, q: (g, 0, 0)),
            pl.BlockSpec((None, XW_ROWS, GQA * LANES), lambda b, g, q: (g, 0, 0)),
            pl.BlockSpec((n_slc, n_cmp_rows), lambda b, g, q: (0, 0)),
        ],
        out_specs=pl.BlockSpec((gw, Q_BLOCK), lambda b, g, q: (g, b * nq + q)),
        out_shape=jax.ShapeDtypeStruct((NSA_WIDTH, n), BF16),
        compiler_params=pltpu.CompilerParams(vmem_limit_bytes=VMEM_LIMIT),
        name="nsa",
    )(qt, kcmp, vcmpt, kslc, kwin, vt, vt, gt, et, xs, xw, ovt)


def _layer_norm_fm(h, g_ref, b_ref, reps):
    mu = jnp.mean(h, axis=0, keepdims=True)
    hc = h - mu
    var = jnp.mean(hc * hc, axis=0, keepdims=True)
    gain = jnp.concatenate([g_ref[...]] * reps, axis=1)
    bias = jnp.concatenate([b_ref[...]] * reps, axis=1)
    return hc * lax.rsqrt(var + LN_EPS) * gain + bias


def _mix_kernel(nsat_ref, zt_ref, halo_ref, x_ref, won_ref, wop_ref, pwt_ref, ps_ref, g_ref, b_ref,
                h_ref, hb_ref, *, tn, seq):
    t0 = (pl.program_id(0) * tn) % seq
    reps = tn // LANES
    zc = jnp.concatenate([halo_ref[...], zt_ref[...]], axis=1)
    t_ext = t0 - LANES + lax.broadcasted_iota(I32, (1, LANES + tn), 1)
    zc = jnp.where(t_ext >= 0, zc, 0.0)
    pooled = []
    for gi, win in enumerate(POOL_WINDOWS):
        zg = zc[gi * POOL_GROUP_DIM:(gi + 1) * POOL_GROUP_DIM]
        s = zg
        sh = 1
        while sh < win:
            s = s + pltpu.roll(s, sh, axis=1)
            sh *= 2
        cnt = jnp.clip(t_ext + 1, 1, win).astype(F32)
        y = (s / cnt - zg)[:, LANES:]
        scale = jnp.concatenate([ps_ref[gi * POOL_GROUP_DIM:(gi + 1) * POOL_GROUP_DIM]] * reps, axis=1)
        pooled.append((_dot(pwt_ref[gi], y.astype(BF16)) * scale).astype(BF16))
    pool_t = jnp.concatenate(pooled, axis=0)
    mix = _dot(won_ref[...], nsat_ref[...]) + _dot(wop_ref[...], pool_t)
    h = ALPHA * x_ref[...].T + mix
    h = _layer_norm_fm(h, g_ref, b_ref, reps)
    h_ref[...] = h
    hb_ref[...] = h.astype(BF16)


def _lane_bcast(v):
    return jnp.broadcast_to(v.reshape(-1, 1), (v.shape[0], LANES)).astype(F32)


def _mix(nsat, zt, x2, w_out, pool_w, pool_scale, ln_g, ln_b, seq):
    n = x2.shape[0]
    tn = 256
    wot = w_out.T.astype(BF16)
    full = lambda *s: pl.BlockSpec(s, lambda i: (0,) * len(s))
    return pl.pallas_call(
        functools.partial(_mix_kernel, tn=tn, seq=seq),
        grid=(n // tn,),
        in_specs=[
            pl.BlockSpec((NSA_WIDTH, tn), lambda i: (0, i)),
            pl.BlockSpec((POOL_WIDTH, tn), lambda i: (0, i)),
            pl.BlockSpec((POOL_WIDTH, LANES), lambda i: (0, jnp.maximum(i * (tn // LANES) - 1, 0))),
            pl.BlockSpec((tn, D_MODEL), lambda i: (i, 0)),
            full(D_MODEL, NSA_WIDTH), full(D_MODEL, POOL_WIDTH),
            full(len(POOL_WINDOWS), POOL_GROUP_DIM, POOL_GROUP_DIM),
            full(POOL_WIDTH, LANES), full(D_MODEL, LANES), full(D_MODEL, LANES),
        ],
        out_specs=[pl.BlockSpec((D_MODEL, tn), lambda i: (0, i))] * 2,
        out_shape=[jax.ShapeDtypeStruct((D_MODEL, n), F32), jax.ShapeDtypeStruct((D_MODEL, n), BF16)],
        compiler_params=pltpu.CompilerParams(vmem_limit_bytes=VMEM_LIMIT),
        name="mix",
    )(nsat, zt, zt, x2, wot[:, :NSA_WIDTH], wot[:, NSA_WIDTH:],
      jnp.swapaxes(pool_w, 1, 2).astype(BF16), _lane_bcast(pool_scale), _lane_bcast(ln_g), _lane_bcast(ln_b))


def _top16_rows(s):
    rows = lax.broadcasted_iota(I32, s.shape, 0).astype(F32)
    cur = s
    rank = jnp.full(s.shape, float(PEER_TOPK), F32)
    vals = []
    for r in range(PEER_TOPK):
        mx = jnp.max(cur, axis=0, keepdims=True)
        first = jnp.min(jnp.where(cur == mx, rows, float(s.shape[0])), axis=0, keepdims=True)
        hit = rows == first
        vals.append(mx)
        rank = jnp.where(hit, float(r), rank)
        cur = jnp.where(hit, -jnp.inf, cur)
    return vals, rank


def _route_kernel(hb_ref, wqt_ref, sk1_ref, sk2_ref, n_ref, a_ref, r2_ref, b_ref, *, tn):
    qt = _dot(wqt_ref[...], hb_ref[...])
    cands = _peer_candidates()
    n_pad = -(-len(cands) // 8) * 8
    c_i = lax.broadcasted_iota(I32, (n_pad, tn), 0)
    half = PEER_QDIM // 2
    for h in range(PEER_HEADS):
        q1 = qt[h * PEER_QDIM:h * PEER_QDIM + half].astype(BF16)
        q2 = qt[h * PEER_QDIM + half:(h + 1) * PEER_QDIM].astype(BF16)
        s1 = _dot(sk1_ref[...], q1)
        s2 = _dot(sk2_ref[...], q2)
        v1, rank1 = _top16_rows(s1)
        v2, rank2 = _top16_rows(s2)
        sums = [v1[a] + v2[b] for a, b in cands]
        cs = jnp.full((n_pad, tn), -jnp.inf, F32)
        for c in range(len(cands)):
            cs = jnp.where(c_i == c, sums[c], cs)
        cnt = jnp.zeros((n_pad, tn), F32)
        for c in range(len(cands)):
            row = sums[c]
            beats = (row > cs) | ((row == cs) & (c_i > c))
            cnt = cnt + jnp.where(beats, 1.0, 0.0)
        sel = cnt < float(PEER_TOPK)
        z = jnp.sum(jnp.where(sel, jnp.exp(cs - sums[0]), 0.0), axis=0, keepdims=True)
        self_f = jnp.where(sel, 1.0, 0.0)
        nmap = jnp.zeros((PEER_KEYS, tn), F32)
        pos = 0
        for a in range(PEER_TOPK):
            nb = PEER_TOPK // (a + 1)
            in_a = (c_i >= pos) & (c_i < pos + nb)
            n_a = jnp.sum(jnp.where(in_a, self_f, 0.0), axis=0, keepdims=True)
            pos += nb
            nmap = jnp.where(rank1 == float(a), n_a, nmap)
        n_ref[h] = nmap
        a_ref[h] = jnp.where(rank1 < float(PEER_TOPK), jnp.exp(s1 - v1[0]), 0.0)
        r2_ref[h] = rank2.astype(BF16)
        b_ref[h] = (jnp.where(rank2 < float(PEER_TOPK), jnp.exp(s2 - v2[0]), 0.0) / z).astype(BF16)


def _route(hb, wq, subkeys):
    n = hb.shape[1]
    tn = 256
    full = lambda *s: pl.BlockSpec(s, lambda i: (0,) * len(s))
    tab = lambda: pl.BlockSpec((PEER_HEADS, PEER_KEYS, tn), lambda i: (0, 0, i))
    shape = (PEER_HEADS, PEER_KEYS, n)
    return pl.pallas_call(
        functools.partial(_route_kernel, tn=tn),
        grid=(n // tn,),
        in_specs=[
            pl.BlockSpec((D_MODEL, tn), lambda i: (0, i)),
            full(PEER_HEADS * PEER_QDIM, D_MODEL), full(PEER_KEYS, PEER_QDIM // 2), full(PEER_KEYS, PEER_QDIM // 2),
        ],
        out_specs=[tab(), tab(), tab(), tab()],
        out_shape=[
            jax.ShapeDtypeStruct(shape, F32),
            jax.ShapeDtypeStruct(shape, F32),
            jax.ShapeDtypeStruct(shape, BF16),
            jax.ShapeDtypeStruct(shape, BF16),
        ],
        compiler_params=pltpu.CompilerParams(vmem_limit_bytes=VMEM_LIMIT),
        name="peer_route",
    )(hb, wq.T.astype(BF16), subkeys[0].astype(BF16), subkeys[1].astype(BF16))


def _experts_kernel(hb_ref, u_ref, vt_ref, n_ref, a_ref, r2_ref, b_ref, h_ref, g_ref, bb_ref,
                    out_ref, acc_ref, *, tn, te):
    j = pl.program_id(1)

    @pl.when(j == 0)
    def _():
        acc_ref[...] = jnp.zeros_like(acc_ref)

    act = _gelu(_dot(u_ref[...], hb_ref[...]))
    chunks = te // PEER_KEYS
    zero = jnp.zeros((PEER_KEYS, tn), BF16)
    ps = []
    for c in range(chunks):
        i1 = j * chunks + c
        w = zero
        for h in range(PEER_HEADS):
            n_row = n_ref[h, pl.ds(i1, 1), :].astype(BF16)
            a_row = a_ref[h, pl.ds(i1, 1), :].astype(BF16)
            w = w + a_row * jnp.where(r2_ref[h] < n_row, b_ref[h], zero)
        ps.append(w * act[c * PEER_KEYS:(c + 1) * PEER_KEYS].astype(BF16))
    acc_ref[...] += _dot(vt_ref[...], jnp.concatenate(ps, axis=0))

    @pl.when(j == pl.num_programs(1) - 1)
    def _():
        h = ALPHA * h_ref[...] + acc_ref[...]
        out_ref[...] = _layer_norm_fm(h, g_ref, bb_ref, tn // LANES).T


def _experts(hb, hf, u_b, vt_b, tabs, ln_g, ln_b):
    n = hb.shape[1]
    tn, te = 512, 512
    tab = lambda: pl.BlockSpec((PEER_HEADS, PEER_KEYS, tn), lambda i, j: (0, 0, i))
    return pl.pallas_call(
        functools.partial(_experts_kernel, tn=tn, te=te),
        grid=(n // tn, PEER_EXPERTS // te),
        in_specs=[
            pl.BlockSpec((D_MODEL, tn), lambda i, j: (0, i)),
            pl.BlockSpec((te, D_MODEL), lambda i, j: (j, 0)),
            pl.BlockSpec((D_MODEL, te), lambda i, j: (0, j)),
            tab(), tab(), tab(), tab(),
            pl.BlockSpec((D_MODEL, tn), lambda i, j: (0, i)),
            pl.BlockSpec((D_MODEL, LANES), lambda i, j: (0, 0)),
            pl.BlockSpec((D_MODEL, LANES), lambda i, j: (0, 0)),
        ],
        out_specs=pl.BlockSpec((tn, D_MODEL), lambda i, j: (i, 0)),
        out_shape=jax.ShapeDtypeStruct((n, D_MODEL), F32),
        scratch_shapes=[pltpu.VMEM((D_MODEL, tn), F32)],
        compiler_params=pltpu.CompilerParams(
            dimension_semantics=("arbitrary", "arbitrary"), vmem_limit_bytes=VMEM_LIMIT),
        name="peer_experts",
    )(hb, u_b, vt_b, *tabs, hf, _lane_bcast(ln_g), _lane_bcast(ln_b))


def kernel(x, w_in, rel_table, cmp_pe_k, cmp_pe_v, cmp_k_w1, cmp_k_w2, cmp_v_w1, cmp_v_w2, pool_w, pool_scale,
           w_out, ln1_g, ln1_b, peer_wq, peer_subkeys, peer_u, peer_v, ln2_g, ln2_b):
    batch, seq, _ = x.shape
    assert w_in.shape[0] == DEPTH == 1 and seq % 512 == 0 and seq // CMP_STRIDE <= 256
    x2 = x.reshape(batch * seq, D_MODEL)
    et, xs, xw = _bias_tables(rel_table)
    qt, vt, gt, zt, kvc, kslc, kwin = _inproj(x2, w_in[0], seq)
    kcmp, vcmpt = _compress(kvc, batch, seq, cmp_pe_k[0], cmp_pe_v[0], cmp_k_w1[0], cmp_k_w2[0],
                            cmp_v_w1[0], cmp_v_w2[0])
    nsat = _nsa(qt, kcmp, vcmpt, kslc, kwin, vt, gt, et, xs, xw, batch, seq)
    hf, hb = _mix(nsat, zt, x2, w_out[0], pool_w[0], pool_scale[0], ln1_g[0], ln1_b[0], seq)
    tabs = _route(hb, peer_wq[0], peer_subkeys[0])
    y = _experts(hb, hf, peer_u[0].astype(BF16), peer_v[0].T.astype(BF16), tabs, ln2_g[0], ln2_b[0])
    return y.reshape(batch, seq, D_MODEL)
```

```python
import functools
import math

import numpy as np
import jax
import jax.numpy as jnp
from jax import lax
from jax.experimental import pallas as pl
from jax.experimental.pallas import tpu as pltpu

F32 = jnp.float32
BF16 = jnp.bfloat16
I32 = jnp.int32

D_MODEL = 1024
NSA_HEADS = 8
KV_HEADS = 2
GQA = NSA_HEADS // KV_HEADS
HEAD_DIM = 64
NSA_WIDTH = NSA_HEADS * HEAD_DIM
KV_WIDTH = KV_HEADS * HEAD_DIM
POOL_WIDTH = D_MODEL - NSA_WIDTH
POOL_WINDOWS = (2, 4, 8, 16)
POOL_GROUP_DIM = POOL_WIDTH // len(POOL_WINDOWS)
N_GATES = 3 * NSA_HEADS
GATE_ROWS = 32
CMP_BLOCK = 32
CMP_STRIDE = 16
CMP_HIDDEN = 256
SLC_BLOCK = 64
SLC_COUNT = 16
WINDOW = 512
Q_BLOCK = 128
NEG = -1e30
FORCE = 1e6
N_BUCKETS = 32
MAX_DISTANCE = 128
PEER_HEADS = 8
PEER_KEYS = 128
PEER_EXPERTS = PEER_KEYS * PEER_KEYS
PEER_QDIM = 256
PEER_TOPK = 16
DEPTH = 1
ALPHA = (2 * DEPTH) ** 0.25
LN_EPS = 1e-5
SQRT_HALF = 0.7071067811865476

LANES = 128
SUBLANES = 8
MXU_SLICE = 128
VMEM_LIMIT = 52 * 1024 * 1024

_FM_Q, _FM_V, _FM_G, _FM_Z, _FM_END = 0, 512, 768, 800, 1312
_TM_WIDTH = 640


def _gelu(x):
    return 0.5 * x * (1.0 + lax.erf(x * SQRT_HALF))


def _dot(a, b):
    return jnp.dot(a, b, preferred_element_type=F32)


def _dot_nt(a, b):
    return lax.dot_general(a, b, (((1,), (1,)), ((), ())), preferred_element_type=F32)


def _bucket_np(dist):
    dist = np.maximum(dist, 0)
    max_exact = N_BUCKETS // 2
    large = max_exact + np.floor(
        np.log(np.maximum(dist, 1) / max_exact) / math.log(MAX_DISTANCE / max_exact)
        * (N_BUCKETS - max_exact)).astype(np.int64)
    large = np.minimum(large, N_BUCKETS - 1)
    return np.where(dist < max_exact, dist, large)


ET_ROWS = 512
XS_ROWS = 896
XS_MASKED_ROW = 640
XW_ROWS = 1152
_CODE_MASKED = N_BUCKETS


def _bias_index_tables():
    qi = np.arange(Q_BLOCK)[None, :]

    def codes(dist, visible):
        return np.where(visible, _bucket_np(dist), _CODE_MASKED).astype(np.int32)

    d_cmp = qi - CMP_STRIDE * (np.arange(ET_ROWS)[:, None] - 256) - (CMP_BLOCK - 1)
    rho = np.arange(XS_ROWS)[:, None]
    d_sel = qi - (rho - 384)
    d_win = qi - (np.arange(XW_ROWS)[:, None] - 512)
    return np.concatenate([
        codes(d_cmp, d_cmp >= 0),
        codes(d_sel, (d_sel >= 0) & (rho < XS_MASKED_ROW)),
        codes(d_win, (d_win >= 0) & (d_win < WINDOW)),
    ], axis=0)


def _overlap_t(n_cmp_rows, n_slc):
    n = np.arange(n_cmp_rows)[None, :]
    j = np.arange(n_slc)[:, None]
    start = n * CMP_STRIDE
    end = start + CMP_BLOCK - 1
    return ((start <= j * SLC_BLOCK + SLC_BLOCK - 1) & (end >= j * SLC_BLOCK)).astype(np.float32)


def _bias_kernel(tab_ref, idx_ref, et_ref, xs_ref, xw_ref):
    g = pl.program_id(0)
    chunk = Q_BLOCK
    for r in range(GQA):
        h = g * GQA + r
        far = tab_ref[N_BUCKETS - 1, h]
        cols = slice(r * LANES, (r + 1) * LANES)
        for c in range((ET_ROWS + XS_ROWS + XW_ROWS) // chunk):
            idx = idx_ref[c * chunk:(c + 1) * chunk, :]
            acc = jnp.where(idx == _CODE_MASKED, NEG, 0.0)
            for b in range(N_BUCKETS - 1):
                acc = jnp.where(idx == b, tab_ref[b, h] - far, acc)
            row = c * chunk
            if row < ET_ROWS:
                et_ref[r, row:row + chunk, :] = acc
            elif row < ET_ROWS + XS_ROWS:
                xs_ref[0, row - ET_ROWS:row - ET_ROWS + chunk, cols] = acc
            else:
                row -= ET_ROWS + XS_ROWS
                xw_ref[0, row:row + chunk, cols] = acc


def _bias_tables(rel_table):
    idx = _bias_index_tables()
    qw = GQA * LANES
    return pl.pallas_call(
        _bias_kernel,
        grid=(KV_HEADS,),
        in_specs=[
            pl.BlockSpec(memory_space=pltpu.SMEM),
            pl.BlockSpec(idx.shape, lambda g: (0, 0)),
        ],
        out_specs=[
            pl.BlockSpec((GQA, ET_ROWS, Q_BLOCK), lambda g: (g, 0, 0)),
            pl.BlockSpec((1, XS_ROWS, qw), lambda g: (g, 0, 0)),
            pl.BlockSpec((1, XW_ROWS, qw), lambda g: (g, 0, 0)),
        ],
        out_shape=[
            jax.ShapeDtypeStruct((NSA_HEADS, ET_ROWS, Q_BLOCK), F32),
            jax.ShapeDtypeStruct((KV_HEADS, XS_ROWS, qw), F32),
            jax.ShapeDtypeStruct((KV_HEADS, XW_ROWS, qw), F32),
        ],
        name="bias_tables",
    )(rel_table, jnp.asarray(idx))


def _inproj_kernel(x_ref, wfm_ref, wtm_ref, qt_ref, vt_ref, gt_ref, zt_ref, kvc_ref,
                   kslc_ref, kwin_ref, *, tn, seq):
    xb = x_ref[...].astype(BF16)
    fm = _dot_nt(wfm_ref[...], xb)
    tm = _dot(xb, wtm_ref[...])
    qt_ref[...] = (fm[_FM_Q:_FM_V] * (HEAD_DIM ** -0.5)).astype(BF16)
    for c in range(2 * KV_HEADS):
        for s in range(tn // LANES):
            vt_ref[c, s] = fm[_FM_V + c * HEAD_DIM:_FM_V + (c + 1) * HEAD_DIM,
                              s * LANES:(s + 1) * LANES].astype(BF16)
    gt_ref[...] = jax.nn.sigmoid(fm[_FM_G:_FM_Z])
    zt_ref[...] = fm[_FM_Z:_FM_END]
    kvc_ref[0] = tm[:, 0:KV_WIDTH]
    kvc_ref[1] = tm[:, KV_WIDTH:2 * KV_WIDTH]
    t0 = (pl.program_id(0) * tn) % seq
    row = lax.broadcasted_iota(I32, (tn, LANES), 0) + t0
    lane = lax.broadcasted_iota(I32, (tn, LANES), 1)
    onehot = jnp.where(lane - HEAD_DIM == (row >> 6), 1.0, 0.0)
    for g in range(KV_HEADS):
        k = tm[:, 256 + g * LANES:256 + (g + 1) * LANES]
        kslc_ref[g] = jnp.where(lane < HEAD_DIM, k, onehot).astype(BF16)
    kwin_ref[...] = tm[:, 512:640].astype(BF16)


def _inproj(x2, w_in, seq):
    n = x2.shape[0]
    tn = 512
    w = w_in
    o = NSA_WIDTH
    w_q = w[:, 0:o]
    w_kc, w_vc, w_ks, w_vs, w_kw, w_vw = [w[:, o + i * KV_WIDTH:o + (i + 1) * KV_WIDTH] for i in range(6)]
    o += 6 * KV_WIDTH
    w_g = w[:, o:o + N_GATES]
    w_z = w[:, o + N_GATES:]
    zpad = jnp.zeros((D_MODEL, HEAD_DIM), w.dtype)
    wfm = jnp.concatenate(
        [w_q, w_vs, w_vw, w_g, jnp.zeros((D_MODEL, GATE_ROWS - N_GATES), w.dtype), w_z], axis=1).T.astype(BF16)
    wtm = jnp.concatenate(
        [w_kc, w_vc, w_ks[:, :HEAD_DIM], zpad, w_ks[:, HEAD_DIM:], zpad, w_kw], axis=1).astype(BF16)
    nt = n // LANES
    return pl.pallas_call(
        functools.partial(_inproj_kernel, tn=tn, seq=seq),
        grid=(n // tn,),
        in_specs=[
            pl.BlockSpec((tn, D_MODEL), lambda i: (i, 0)),
            pl.BlockSpec((_FM_END, D_MODEL), lambda i: (0, 0)),
            pl.BlockSpec((D_MODEL, _TM_WIDTH), lambda i: (0, 0)),
        ],
        out_specs=[
            pl.BlockSpec((NSA_WIDTH, tn), lambda i: (0, i)),
            pl.BlockSpec((2 * KV_HEADS, tn // LANES, HEAD_DIM, LANES), lambda i: (0, i, 0, 0)),
            pl.BlockSpec((GATE_ROWS, tn), lambda i: (0, i)),
            pl.BlockSpec((POOL_WIDTH, tn), lambda i: (0, i)),
            pl.BlockSpec((2, tn, KV_WIDTH), lambda i: (0, i, 0)),
            pl.BlockSpec((KV_HEADS, tn, LANES), lambda i: (0, i, 0)),
            pl.BlockSpec((tn, LANES), lambda i: (i, 0)),
        ],
        out_shape=[
            jax.ShapeDtypeStruct((NSA_WIDTH, n), BF16),
            jax.ShapeDtypeStruct((2 * KV_HEADS, nt, HEAD_DIM, LANES), BF16),
            jax.ShapeDtypeStruct((GATE_ROWS, n), F32),
            jax.ShapeDtypeStruct((POOL_WIDTH, n), F32),
            jax.ShapeDtypeStruct((2, n, KV_WIDTH), F32),
            jax.ShapeDtypeStruct((KV_HEADS, n, LANES), BF16),
            jax.ShapeDtypeStruct((n, LANES), BF16),
        ],
        compiler_params=pltpu.CompilerParams(vmem_limit_bytes=VMEM_LIMIT),
        name="inproj",
    )(x2, wfm, wtm)


def _compress_kernel(kc_ref, vc_ref, w2k_ref, w2v_ref, pek_ref, pev_ref, w1k_ref, w1v_ref,
                     wok0_ref, wok1_ref, wovt_ref, kcmp_ref, vcmpt_ref, *, n_rows):
    half = CMP_BLOCK // 2
    acc_k = jnp.zeros((n_rows, 4 * CMP_HIDDEN), F32)
    acc_v = jnp.zeros((n_rows, 4 * CMP_HIDDEN), F32)
    for l in range(half):
        rows = pl.ds(l, n_rows, stride=CMP_STRIDE)
        acc_k = acc_k + _dot(kc_ref[rows, :].astype(BF16), w2k_ref[l])
        acc_v = acc_v + _dot(vc_ref[rows, :].astype(BF16), w2v_ref[l])
    ck = _dot(pek_ref[...], w1k_ref[...])[0:1]
    cv = _dot(pev_ref[...], w1v_ref[...])[0:1]

    def hidden(acc, c, g):
        a = acc[:, g * CMP_HIDDEN:(g + 1) * CMP_HIDDEN]
        b = acc[:, (2 + g) * CMP_HIDDEN:(3 + g) * CMP_HIDDEN]
        return _gelu(a + pltpu.roll(b, n_rows - 1, axis=0) + c).astype(BF16)

    kcmp_ref[...] = (_dot(hidden(acc_k, ck, 0), wok0_ref[...])
                     + _dot(hidden(acc_k, ck, 1), wok1_ref[...])).astype(BF16)
    for g in range(KV_HEADS):
        vcmpt_ref[g] = _dot_nt(wovt_ref[...], hidden(acc_v, cv, g)).astype(BF16)


def _compress(kvc, batch, seq, pe_k, pe_v, k_w1, k_w2, v_w1, v_w2):
    n_rows = seq // CMP_STRIDE
    half = CMP_BLOCK // 2

    def blockdiag(w1):
        w = w1.reshape(CMP_BLOCK, HEAD_DIM, CMP_HIDDEN)
        z = jnp.zeros((half, HEAD_DIM, CMP_HIDDEN), w1.dtype)
        top = jnp.concatenate([w[:half], z, w[half:], z], axis=2)
        bot = jnp.concatenate([z, w[:half], z, w[half:]], axis=2)
        return jnp.concatenate([top, bot], axis=1).astype(BF16)

    def pe_rows(pe):
        flat = pe.reshape(1, CMP_BLOCK * HEAD_DIM)
        return jnp.concatenate([flat, jnp.zeros((7, flat.shape[1]), pe.dtype)], axis=0).astype(BF16)

    zo = jnp.zeros((CMP_HIDDEN, HEAD_DIM), k_w2.dtype)
    wok0 = jnp.concatenate([k_w2, zo], axis=1).astype(BF16)
    wok1 = jnp.concatenate([zo, k_w2], axis=1).astype(BF16)
    full = lambda *s: pl.BlockSpec(s, lambda b: (0,) * len(s))
    return pl.pallas_call(
        functools.partial(_compress_kernel, n_rows=n_rows),
        grid=(batch,),
        in_specs=[
            pl.BlockSpec((None, seq, KV_WIDTH), lambda b: (0, b, 0)),
            pl.BlockSpec((None, seq, KV_WIDTH), lambda b: (1, b, 0)),
            full(half, KV_WIDTH, 4 * CMP_HIDDEN), full(half, KV_WIDTH, 4 * CMP_HIDDEN),
            full(8, CMP_BLOCK * HEAD_DIM), full(8, CMP_BLOCK * HEAD_DIM),
            full(CMP_BLOCK * HEAD_DIM, CMP_HIDDEN), full(CMP_BLOCK * HEAD_DIM, CMP_HIDDEN),
            full(CMP_HIDDEN, KV_WIDTH), full(CMP_HIDDEN, KV_WIDTH), full(HEAD_DIM, CMP_HIDDEN),
        ],
        out_specs=[
            pl.BlockSpec((None, n_rows, KV_WIDTH), lambda b: (b, 0, 0)),
            pl.BlockSpec((None, KV_HEADS, HEAD_DIM, n_rows), lambda b: (b, 0, 0, 0)),
        ],
        out_shape=[
            jax.ShapeDtypeStruct((batch, n_rows, KV_WIDTH), BF16),
            jax.ShapeDtypeStruct((batch, KV_HEADS, HEAD_DIM, n_rows), BF16),
        ],
        compiler_params=pltpu.CompilerParams(vmem_limit_bytes=VMEM_LIMIT),
        name="compress",
    )(kvc, kvc, blockdiag(k_w1), blockdiag(v_w1), pe_rows(pe_k), pe_rows(pe_v),
      k_w1.astype(BF16), v_w1.astype(BF16), wok0, wok1, v_w2.T.astype(BF16))


SUPER_TILE = 2 * Q_BLOCK
WIN_KEYS = WINDOW + Q_BLOCK


def _softmax_init(s, vt):
    m = jnp.max(s, axis=0, keepdims=True)
    p = jnp.exp(s - m)
    return m, jnp.sum(p, axis=0, keepdims=True), _dot(vt, p.astype(BF16))


def _softmax_step(s, vt, state):
    m, l, acc = state
    m_new = jnp.maximum(m, jnp.max(s, axis=0, keepdims=True))
    alpha = jnp.exp(m - m_new)
    p = jnp.exp(s - m_new)
    return m_new, alpha * l + jnp.sum(p, axis=0, keepdims=True), alpha * acc + _dot(vt, p.astype(BF16))


def _softmax_merge(a, b):
    m = jnp.maximum(a[0], b[0])
    wa = jnp.exp(a[0] - m)
    wb = jnp.exp(b[0] - m)
    return (wa * a[2] + wb * b[2]) * (1.0 / (wa * a[1] + wb * b[1]))


def _nsa_kernel(qt_ref, kcmp_ref, vcmpt_ref, kslc_ref, kwin_ref, vslct_ref, vwint_ref, gt_ref,
                et_ref, xs_ref, xw_ref, ovt_ref, out_ref, *, n_cmp_rows, n_slc):
    g = pl.program_id(1)
    qb = pl.program_id(2)
    qw = GQA * Q_BLOCK
    q4 = qt_ref[...]
    qcat = jnp.concatenate([q4[r * HEAD_DIM:(r + 1) * HEAD_DIM] for r in range(GQA)], axis=1)
    zq = jnp.zeros_like(qcat)
    qg = jnp.concatenate([jnp.where(g == 0, qcat, zq), jnp.where(g == 1, qcat, zq)], axis=0)

    s = _dot(kcmp_ref[...], qg)
    k0 = jnp.maximum(qb * Q_BLOCK - WINDOW, 0)
    row_w = pl.multiple_of(WINDOW - qb * Q_BLOCK + k0, Q_BLOCK)
    s_win = _dot(kwin_ref[pl.ds(pl.multiple_of(k0, Q_BLOCK), WIN_KEYS), :], qg)

    off = pl.multiple_of(256 - 8 * qb, 8)
    s = s + jnp.concatenate([et_ref[r, pl.ds(off, n_cmp_rows), :] for r in range(GQA)], axis=1)
    m = jnp.max(s, axis=0, keepdims=True)
    p = jnp.exp(s - m)
    l = jnp.sum(p, axis=0, keepdims=True)
    p = p * jnp.where(m > 0.5 * NEG, 1.0 / l, 0.0)
    o_cmp = _dot(vcmpt_ref[...], p.astype(BF16))
    p_sum = p[:, 0:Q_BLOCK]
    for r in range(1, GQA):
        p_sum = p_sum + p[:, r * Q_BLOCK:(r + 1) * Q_BLOCK]
    imp = jnp.dot(ovt_ref[...], p_sum, preferred_element_type=F32,
                  precision=lax.Precision.HIGHEST)

    kt0 = k0 >> 7
    vt = jnp.concatenate([vwint_ref[kt0 + i] for i in range(WIN_KEYS // Q_BLOCK)], axis=1)
    m, l, acc = _softmax_init(s_win + xw_ref[pl.ds(row_w, WIN_KEYS), :], vt)
    o_win = acc * (1.0 / l)

    j_i = lax.broadcasted_iota(I32, (n_slc, Q_BLOCK), 0)
    t1 = qb * Q_BLOCK + lax.broadcasted_iota(I32, (n_slc, Q_BLOCK), 1)
    cur = t1 >> 6
    forced = (j_i == 0) | (j_i == cur) | (j_i == cur - 1)
    imp = jnp.where(forced, FORCE, imp)
    imp = jnp.where(j_i <= cur, imp, NEG)
    cnt = jnp.zeros((n_slc, Q_BLOCK), F32)
    for jp in range(n_slc):
        row = imp[jp:jp + 1, :]
        beats = (row > imp) | ((row == imp) & (j_i > jp))
        cnt = cnt + jnp.where(beats, 1.0, 0.0)
    selneg = jnp.where(cnt < float(SLC_COUNT), 0.0, NEG).astype(BF16)
    qsel = [qcat, jnp.concatenate([selneg] * GQA, axis=1)]
    if n_slc < LANES - HEAD_DIM:
        qsel.append(jnp.zeros((LANES - HEAD_DIM - n_slc, qw), BF16))
    qsel = jnp.concatenate(qsel, axis=0)

    def sel_scores(st):
        k = kslc_ref[pl.ds(pl.multiple_of(st * SUPER_TILE, SUPER_TILE), SUPER_TILE), :]
        vt = jnp.concatenate([vslct_ref[2 * st], vslct_ref[2 * st + 1]], axis=1)
        return _dot(k, qsel), vt

    odd = qb & 1
    st_diag = qb >> 1
    st_prev = jnp.maximum(st_diag - 1, 0)
    n_far = st_prev
    row_a = pl.multiple_of(384 - Q_BLOCK * odd, Q_BLOCK)
    row_b = pl.multiple_of(jnp.where(st_diag >= 1, Q_BLOCK - Q_BLOCK * odd, XS_MASKED_ROW), Q_BLOCK)
    s_a, vt_a = sel_scores(st_diag)
    s_b, vt_b = sel_scores(st_prev)
    chain_a = _softmax_init(s_a + xs_ref[pl.ds(row_a, SUPER_TILE), :], vt_a)
    chain_b = _softmax_init(s_b + xs_ref[pl.ds(row_b, SUPER_TILE), :], vt_b)

    def far_pair(i, chains):
        a, b = chains
        s_a, vt_a = sel_scores(2 * i)
        s_b, vt_b = sel_scores(2 * i + 1)
        return _softmax_step(s_a, vt_a, a), _softmax_step(s_b, vt_b, b)

    chain_a, chain_b = lax.fori_loop(0, n_far >> 1, far_pair, (chain_a, chain_b))
    chain_a = lax.cond((n_far & 1) == 1,
                       lambda a: _softmax_step(*sel_scores(n_far - 1), a), lambda a: a, chain_a)
    o_slc = _softmax_merge(chain_a, chain_b)

    outs = []
    for r in range(GQA):
        cols = slice(r * Q_BLOCK, (r + 1) * Q_BLOCK)
        gate = [gt_ref[pl.ds(c * NSA_HEADS + g * GQA + r, 1), :] for c in range(3)]
        outs.append(gate[0] * o_cmp[:, cols] + gate[1] * o_slc[:, cols] + gate[2] * o_win[:, cols])
    out_ref[...] = jnp.concatenate(outs, axis=0).astype(BF16)


def _nsa(qt, kcmp, vcmpt, kslc, kwin, vt, gt, et, xs, xw, batch, seq):
    assert seq % SUPER_TILE == 0 and seq >= WIN_KEYS and seq // CMP_STRIDE <= 256 and seq // SLC_BLOCK <= HEAD_DIM
    nq = seq // Q_BLOCK
    n_cmp_rows = seq // CMP_STRIDE
    n_slc = seq // SLC_BLOCK
    n = batch * seq
    ovt = jnp.asarray(_overlap_t(n_cmp_rows, n_slc))
    gw = GQA * HEAD_DIM
    return pl.pallas_call(
        functools.partial(_nsa_kernel, n_cmp_rows=n_cmp_rows, n_slc=n_slc),
        grid=(batch, KV_HEADS, nq),
        in_specs=[
            pl.BlockSpec((gw, Q_BLOCK), lambda b, g, q: (g, b * nq + q)),
            pl.BlockSpec((None, n_cmp_rows, KV_WIDTH), lambda b, g, q: (b, 0, 0)),
            pl.BlockSpec((None, None, HEAD_DIM, n_cmp_rows), lambda b, g, q: (b, g, 0, 0)),
            pl.BlockSpec((None, seq, LANES), lambda b, g, q: (g, b, 0)),
            pl.BlockSpec((seq, LANES), lambda b, g, q: (b, 0)),
            pl.BlockSpec((None, nq, HEAD_DIM, LANES), lambda b, g, q: (g, b, 0, 0)),
            pl.BlockSpec((None, nq, HEAD_DIM, LANES), lambda b, g, q: (KV_HEADS + g, b, 0, 0)),
            pl.BlockSpec((GATE_ROWS, Q_BLOCK), lambda b, g, q: (0, b * nq + q)),
            pl.BlockSpec((GQA, ET_ROWS, Q_BLOCK), lambda b, g, q: (g, 0, 0)),
            pl.BlockSpec((None, XS_ROWS, GQA * LANES), lambda b, g, q: (g, 0, 0)),
            pl.BlockSpec((None, XW_ROWS, GQA * LANES), lambda b, g, q: (g, 0, 0)),
            pl.BlockSpec((n_slc, n_cmp_rows), lambda b, g, q: (0, 0)),
        ],
        out_specs=pl.BlockSpec((gw, Q_BLOCK), lambda b, g, q: (g, b * nq + q)),
        out_shape=jax.ShapeDtypeStruct((NSA_WIDTH, n), BF16),
        compiler_params=pltpu.CompilerParams(vmem_limit_bytes=VMEM_LIMIT),
        name="nsa",
    )(qt, kcmp, vcmpt, kslc, kwin, vt, vt, gt, et, xs, xw, ovt)


def _layer_norm_fm(h, g_ref, b_ref, reps):
    mu = jnp.mean(h, axis=0, keepdims=True)
    hc = h - mu
    var = jnp.mean(hc * hc, axis=0, keepdims=True)
    gain = jnp.concatenate([g_ref[...]] * reps, axis=1)
    bias = jnp.concatenate([b_ref[...]] * reps, axis=1)
    return hc * lax.rsqrt(var + LN_EPS) * gain + bias


def _mix_kernel(nsat_ref, zt_ref, halo_ref, x_ref, won_ref, wop_ref, pwt_ref, ps_ref, g_ref, b_ref,
                h_ref, hb_ref, *, tn, seq):
    t0 = (pl.program_id(0) * tn) % seq
    reps = tn // LANES
    zc = jnp.concatenate([halo_ref[...], zt_ref[...]], axis=1)
    t_ext = t0 - LANES + lax.broadcasted_iota(I32, (1, LANES + tn), 1)
    zc = jnp.where(t_ext >= 0, zc, 0.0)
    pooled = []
    for gi, win in enumerate(POOL_WINDOWS):
        zg = zc[gi * POOL_GROUP_DIM:(gi + 1) * POOL_GROUP_DIM]
        s = zg
        sh = 1
        while sh < win:
            s = s + pltpu.roll(s, sh, axis=1)
            sh *= 2
        cnt = jnp.clip(t_ext + 1, 1, win).astype(F32)
        y = (s / cnt - zg)[:, LANES:]
        scale = jnp.concatenate([ps_ref[gi * POOL_GROUP_DIM:(gi + 1) * POOL_GROUP_DIM]] * reps, axis=1)
        pooled.append((_dot(pwt_ref[gi], y.astype(BF16)) * scale).astype(BF16))
    pool_t = jnp.concatenate(pooled, axis=0)
    mix = _dot(won_ref[...], nsat_ref[...]) + _dot(wop_ref[...], pool_t)
    h = ALPHA * x_ref[...].T + mix
    h = _layer_norm_fm(h, g_ref, b_ref, reps)
    h_ref[...] = h
    hb_ref[...] = h.astype(BF16)


def _lane_bcast(v):
    return jnp.broadcast_to(v.reshape(-1, 1), (v.shape[0], LANES)).astype(F32)


def _mix(nsat, zt, x2, w_out, pool_w, pool_scale, ln_g, ln_b, seq):
    n = x2.shape[0]
    tn = 256
    wot = w_out.T.astype(BF16)
    full = lambda *s: pl.BlockSpec(s, lambda i: (0,) * len(s))
    return pl.pallas_call(
        functools.partial(_mix_kernel, tn=tn, seq=seq),
        grid=(n // tn,),
        in_specs=[
            pl.BlockSpec((NSA_WIDTH, tn), lambda i: (0, i)),
            pl.BlockSpec((POOL_WIDTH, tn), lambda i: (0, i)),
            pl.BlockSpec((POOL_WIDTH, LANES), lambda i: (0, jnp.maximum(i * (tn // LANES) - 1, 0))),
            pl.BlockSpec((tn, D_MODEL), lambda i: (i, 0)),
            full(D_MODEL, NSA_WIDTH), full(D_MODEL, POOL_WIDTH),
            full(len(POOL_WINDOWS), POOL_GROUP_DIM, POOL_GROUP_DIM),
            full(POOL_WIDTH, LANES), full(D_MODEL, LANES), full(D_MODEL, LANES),
        ],
        out_specs=[pl.BlockSpec((D_MODEL, tn), lambda i: (0, i))] * 2,
        out_shape=[jax.ShapeDtypeStruct((D_MODEL, n), F32), jax.ShapeDtypeStruct((D_MODEL, n), BF16)],
        compiler_params=pltpu.CompilerParams(vmem_limit_bytes=VMEM_LIMIT),
        name="mix",
    )(nsat, zt, zt, x2, wot[:, :NSA_WIDTH], wot[:, NSA_WIDTH:],
      jnp.swapaxes(pool_w, 1, 2).astype(BF16), _lane_bcast(pool_scale), _lane_bcast(ln_g), _lane_bcast(ln_b))


ROUTE_HEAD_GROUP = 4
_REMOVED = -(2.0 ** 100)
_PAD_SCORE = -3e38


def _top16_by_index(s):
    rows = lax.broadcasted_iota(I32, s.shape, 0).astype(F32)
    row16 = lax.broadcasted_iota(I32, (PEER_TOPK, s.shape[1]), 0)
    cur = s
    rank = jnp.full(s.shape, float(PEER_TOPK), F32)
    vals = jnp.zeros((PEER_TOPK, s.shape[1]), F32)
    for r in range(PEER_TOPK):
        mx = jnp.max(cur, axis=0, keepdims=True)
        first = jnp.min(jnp.where(cur == mx, rows, float(s.shape[0])), axis=0, keepdims=True)
        hit = rows == first
        vals = jnp.where(row16 == r, mx, vals)
        rank = jnp.where(hit, float(r), rank)
        cur = jnp.where(hit, -jnp.inf, cur)
    return vals, rank, jnp.full((1, s.shape[1]), float(PEER_TOPK), F32)


def _top16_by_value(scores):
    width = scores[0].shape[1]
    row16 = lax.broadcasted_iota(I32, (PEER_TOPK, width), 0)
    cur = list(scores)
    vals = [jnp.zeros((PEER_TOPK, width), F32) for _ in scores]
    for r in range(PEER_TOPK):
        mx = [jnp.max(c, axis=0, keepdims=True) for c in cur]
        vals = [jnp.where(row16 == r, m, v) for m, v in zip(mx, vals)]
        cur = [jnp.where(c == m, _REMOVED * (r + 1), c) for c, m in zip(cur, mx)]
    out = []
    for c, v in zip(cur, vals):
        member = c <= _REMOVED
        rank = jnp.where(member, c * (1.0 / _REMOVED) - 1.0, float(PEER_TOPK))
        out.append((v, rank, jnp.sum(jnp.where(member, 1.0, 0.0), axis=0, keepdims=True)))
    return out


def _candidate_table(v1, v2):
    tn = v1.shape[1]
    row8 = lax.broadcasted_iota(I32, (SUBLANES, tn), 0)
    groups = [v1[0:1, :] + v2]
    real = [PEER_TOPK]
    for a in range(1, SUBLANES):
        nb = PEER_TOPK // (a + 1)
        groups.append(jnp.where(row8 < nb, v1[a:a + 1, :] + v2[0:SUBLANES, :], _PAD_SCORE))
        real.append(nb)
    groups.append(v1[SUBLANES:, :] + v2[0:1, :])
    real.append(PEER_TOPK - SUBLANES)
    return jnp.concatenate(groups, axis=0), real


def _peer_scores(h, hb_ref, wqt_ref, sk1_ref, sk2_ref):
    half = PEER_QDIM // 2
    q = _dot(wqt_ref[h * PEER_QDIM:(h + 1) * PEER_QDIM, :], hb_ref[...])
    return _dot(sk1_ref[...], q[:half].astype(BF16)), _dot(sk2_ref[...], q[half:].astype(BF16))


def _route_heads(heads, scores, outs, *, tn, by_value):
    n_ref, a_ref, r2_ref, b_ref = outs
    want = float(PEER_TOPK)
    miscount = jnp.zeros((1, tn), F32)
    per_head = []
    for h, (s1, s2) in zip(heads, scores):
        if by_value:
            (v1, rank1, count1), (v2, rank2, count2) = _top16_by_value([s1, s2])
        else:
            (v1, rank1, count1), (v2, rank2, count2) = _top16_by_index(s1), _top16_by_index(s2)
        miscount = jnp.maximum(miscount, jnp.maximum(jnp.abs(count1 - want), jnp.abs(count2 - want)))
        cs, real = _candidate_table(v1, v2)
        per_head.append((h, s1, s2, v1, v2, rank1, rank2, cs))

    if by_value:
        cur = [ph[-1] for ph in per_head]
        for _ in range(PEER_TOPK):
            cur = [jnp.where(c == jnp.max(c, axis=0, keepdims=True), -jnp.inf, c) for c in cur]
        sels = [c == -jnp.inf for c in cur]
    else:
        sels = []
        for ph in per_head:
            cs = ph[-1]
            c_i = lax.broadcasted_iota(I32, cs.shape, 0)
            cnt = jnp.zeros(cs.shape, F32)
            base = 0
            for g, n_real in enumerate(real):
                for c in range(base, base + n_real):
                    row = cs[c:c + 1, :]
                    beats = (row > cs) | ((row == cs) & (c_i > c))
                    cnt = cnt + jnp.where(beats, 1.0, 0.0)
                base += PEER_TOPK if g == 0 else SUBLANES
            sels.append(cnt < want)

    for (h, s1, s2, v1, v2, rank1, rank2, cs), sel in zip(per_head, sels):
        self_f = jnp.where(sel, 1.0, 0.0)
        miscount = jnp.maximum(miscount, jnp.abs(jnp.sum(self_f, axis=0, keepdims=True) - want))
        top = v1[0:1, :] + v2[0:1, :]
        z = jnp.sum(jnp.where(sel, jnp.exp(cs - top), 0.0), axis=0, keepdims=True)
        nmap = jnp.zeros((PEER_KEYS, tn), F32)
        base = 0
        for a in range(PEER_TOPK):
            if a < SUBLANES:
                size = PEER_TOPK if a == 0 else SUBLANES
                n_a = jnp.sum(self_f[base:base + size, :], axis=0, keepdims=True)
                base += size
            else:
                n_a = self_f[base + a - SUBLANES:base + a - SUBLANES + 1, :]
            nmap = jnp.where(rank1 == float(a), n_a, nmap)
        n_ref[h] = nmap
        a_ref[h] = jnp.where(rank1 < want, jnp.exp(s1 - v1[0:1, :]), 0.0)
        r2_ref[h] = rank2.astype(BF16)
        b_ref[h] = (jnp.where(rank2 < want, jnp.exp(s2 - v2[0:1, :]), 0.0) / z).astype(BF16)
    return miscount


def _route_kernel(hb_ref, wqt_ref, sk1_ref, sk2_ref, n_ref, a_ref, r2_ref, b_ref, *, tn):
    scores = functools.partial(_peer_scores, hb_ref=hb_ref, wqt_ref=wqt_ref, sk1_ref=sk1_ref, sk2_ref=sk2_ref)
    outs = (n_ref, a_ref, r2_ref, b_ref)
    groups = [range(h0, h0 + ROUTE_HEAD_GROUP) for h0 in range(0, PEER_HEADS, ROUTE_HEAD_GROUP)]
    miscount = jnp.zeros((1, tn), F32)
    ready = [scores(h) for h in groups[0]]
    for gi, heads in enumerate(groups):
        ahead = [scores(h) for h in groups[gi + 1]] if gi + 1 < len(groups) else None
        miscount = jnp.maximum(miscount, _route_heads(heads, ready, outs, tn=tn, by_value=True))
        ready = ahead

    @pl.when(jnp.max(miscount) > 0.0)
    def _():
        for h in range(PEER_HEADS):
            _route_heads([h], [scores(h)], outs, tn=tn, by_value=False)


def _route(hb, wq, subkeys):
    n = hb.shape[1]
    tn = 256
    full = lambda *s: pl.BlockSpec(s, lambda i: (0,) * len(s))
    tab = lambda: pl.BlockSpec((PEER_HEADS, PEER_KEYS, tn), lambda i: (0, 0, i))
    shape = (PEER_HEADS, PEER_KEYS, n)
    return pl.pallas_call(
        functools.partial(_route_kernel, tn=tn),
        grid=(n // tn,),
        in_specs=[
            pl.BlockSpec((D_MODEL, tn), lambda i: (0, i)),
            full(PEER_HEADS * PEER_QDIM, D_MODEL), full(PEER_KEYS, PEER_QDIM // 2), full(PEER_KEYS, PEER_QDIM // 2),
        ],
        out_specs=[tab(), tab(), tab(), tab()],
        out_shape=[
            jax.ShapeDtypeStruct(shape, F32),
            jax.ShapeDtypeStruct(shape, F32),
            jax.ShapeDtypeStruct(shape, BF16),
            jax.ShapeDtypeStruct(shape, BF16),
        ],
        compiler_params=pltpu.CompilerParams(vmem_limit_bytes=VMEM_LIMIT),
        name="peer_route",
    )(hb, wq.T.astype(BF16), subkeys[0].astype(BF16), subkeys[1].astype(BF16))


def _experts_kernel(hb_ref, u_ref, vt_ref, n_ref, a_ref, r2_ref, b_ref, h_ref, g_ref, bb_ref,
                    out_ref, acc_ref, act0_ref, act1_ref, p0_ref, p1_ref, rows_ref, *, tn, te, n_tiles):
    s = pl.program_id(1)

    @pl.when(s == 0)
    def _():
        acc_ref[...] = jnp.zeros_like(acc_ref)
        act1_ref[...] = jnp.zeros_like(act1_ref)
        p0_ref[...] = jnp.zeros_like(p0_ref)
        p1_ref[...] = jnp.zeros_like(p1_ref)

    keys_per_tile = te // PEER_KEYS
    zero = jnp.zeros((PEER_KEYS, LANES), BF16)
    tile_mid = jnp.clip(s - 1, 0, n_tiles - 1)

    def bcast_row(kk, r, cols):
        row = jnp.broadcast_to(rows_ref[kk, r:r + 1, cols], (2 * SUBLANES, LANES)).astype(BF16)
        return jnp.concatenate([row] * (PEER_KEYS // (2 * SUBLANES)), axis=0)

    def step(act_w, act_r, p_w, p_r):
        def act_slice(i):
            rows = slice(i * MXU_SLICE, (i + 1) * MXU_SLICE)
            act_w[rows, :] = _dot(u_ref[rows, :], hb_ref[...])

        def out_slice(i):
            rows = slice(i * MXU_SLICE, (i + 1) * MXU_SLICE)
            acc_ref[rows, :] += _dot(vt_ref[rows, :], p_r[...])

        mxu_work = [functools.partial(out_slice, i) for i in range(D_MODEL // MXU_SLICE)]
        for i in range(te // MXU_SLICE):
            mxu_work.insert(3 * i, functools.partial(act_slice, i))
        col_blocks = tn // LANES
        n_pieces = keys_per_tile * col_blocks
        n_slices = len(mxu_work)
        for kk in range(keys_per_tile):
            rows = slice(kk * PEER_KEYS, (kk + 1) * PEER_KEYS)
            i1 = tile_mid * keys_per_tile + kk
            for h in range(PEER_HEADS):
                rows_ref[kk, h:h + 1, :] = n_ref[h, pl.ds(i1, 1), :]
                rows_ref[kk, PEER_HEADS + h:PEER_HEADS + h + 1, :] = a_ref[h, pl.ds(i1, 1), :]
            for cb in range(col_blocks):
                piece = kk * col_blocks + cb
                while n_slices - len(mxu_work) < -(-(piece + 1) * n_slices // n_pieces):
                    mxu_work.pop(0)()
                cols = slice(cb * LANES, (cb + 1) * LANES)
                w = zero
                for h in range(PEER_HEADS):
                    picked = jnp.where(r2_ref[h, :, cols] < bcast_row(kk, h, cols), b_ref[h, :, cols], zero)
                    w = w + bcast_row(kk, PEER_HEADS + h, cols) * picked
                p_w[rows, cols] = w * _gelu(act_r[rows, cols]).astype(BF16)
        for work in mxu_work:
            work()

    @pl.when(s % 2 == 0)
    def _():
        step(act0_ref, act1_ref, p1_ref, p0_ref)

    @pl.when(s % 2 == 1)
    def _():
        step(act1_ref, act0_ref, p0_ref, p1_ref)

    @pl.when(s == pl.num_programs(1) - 1)
    def _():
        h = ALPHA * h_ref[...] + acc_ref[...]
        out_ref[...] = _layer_norm_fm(h, g_ref, bb_ref, tn // LANES).T


def _experts(hb, hf, u_b, vt_b, tabs, ln_g, ln_b):
    n = hb.shape[1]
    tn, te = 512, 512
    n_tiles = PEER_EXPERTS // te
    tab = lambda: pl.BlockSpec((PEER_HEADS, PEER_KEYS, tn), lambda i, j: (0, 0, i))
    return pl.pallas_call(
        functools.partial(_experts_kernel, tn=tn, te=te, n_tiles=n_tiles),
        grid=(n // tn, n_tiles + 2),
        in_specs=[
            pl.BlockSpec((D_MODEL, tn), lambda i, j: (0, i)),
            pl.BlockSpec((te, D_MODEL), lambda i, j: (jnp.minimum(j, n_tiles - 1), 0)),
            pl.BlockSpec((D_MODEL, te), lambda i, j: (0, jnp.clip(j - 2, 0, n_tiles - 1))),
            tab(), tab(), tab(), tab(),
            pl.BlockSpec((D_MODEL, tn), lambda i, j: (0, i)),
            pl.BlockSpec((D_MODEL, LANES), lambda i, j: (0, 0)),
            pl.BlockSpec((D_MODEL, LANES), lambda i, j: (0, 0)),
        ],
        out_specs=pl.BlockSpec((tn, D_MODEL), lambda i, j: (i, 0)),
        out_shape=jax.ShapeDtypeStruct((n, D_MODEL), F32),
        scratch_shapes=[pltpu.VMEM((D_MODEL, tn), F32),
                        pltpu.VMEM((te, tn), F32), pltpu.VMEM((te, tn), F32),
                        pltpu.VMEM((te, tn), BF16), pltpu.VMEM((te, tn), BF16),
                        pltpu.VMEM((te // PEER_KEYS, 2 * PEER_HEADS, tn), F32)],
        compiler_params=pltpu.CompilerParams(
            dimension_semantics=("arbitrary", "arbitrary"), vmem_limit_bytes=VMEM_LIMIT),
        name="peer_experts",
    )(hb, u_b, vt_b, *tabs, hf, _lane_bcast(ln_g), _lane_bcast(ln_b))


def kernel(x, w_in, rel_table, cmp_pe_k, cmp_pe_v, cmp_k_w1, cmp_k_w2, cmp_v_w1, cmp_v_w2, pool_w, pool_scale,
           w_out, ln1_g, ln1_b, peer_wq, peer_subkeys, peer_u, peer_v, ln2_g, ln2_b):
    batch, seq, _ = x.shape
    assert w_in.shape[0] == DEPTH == 1 and seq % 512 == 0 and seq // CMP_STRIDE <= 256
    x2 = x.reshape(batch * seq, D_MODEL)
    et, xs, xw = _bias_tables(rel_table)
    qt, vt, gt, zt, kvc, kslc, kwin = _inproj(x2, w_in[0], seq)
    kcmp, vcmpt = _compress(kvc, batch, seq, cmp_pe_k[0], cmp_pe_v[0], cmp_k_w1[0], cmp_k_w2[0],
                            cmp_v_w1[0], cmp_v_w2[0])
    nsat = _nsa(qt, kcmp, vcmpt, kslc, kwin, vt, gt, et, xs, xw, batch, seq)
    hf, hb = _mix(nsat, zt, x2, w_out[0], pool_w[0], pool_scale[0], ln1_g[0], ln1_b[0], seq)
    tabs = _route(hb, peer_wq[0], peer_subkeys[0])
    y = _experts(hb, hf, peer_u[0].astype(BF16), peer_v[0].T.astype(BF16), tabs, ln2_g[0], ln2_b[0])
    return y.reshape(batch, seq, D_MODEL)
```

```python
import functools
import math

import numpy as np
import jax
import jax.numpy as jnp
from jax import lax
from jax.experimental import pallas as pl
from jax.experimental.pallas import tpu as pltpu

F32 = jnp.float32
BF16 = jnp.bfloat16
I32 = jnp.int32

D_MODEL = 1024
NSA_HEADS = 8
KV_HEADS = 2
GQA = NSA_HEADS // KV_HEADS
HEAD_DIM = 64
NSA_WIDTH = NSA_HEADS * HEAD_DIM
KV_WIDTH = KV_HEADS * HEAD_DIM
POOL_WIDTH = D_MODEL - NSA_WIDTH
POOL_WINDOWS = (2, 4, 8, 16)
POOL_GROUP_DIM = POOL_WIDTH // len(POOL_WINDOWS)
N_GATES = 3 * NSA_HEADS
GATE_ROWS = 32
CMP_BLOCK = 32
CMP_STRIDE = 16
CMP_HIDDEN = 256
SLC_BLOCK = 64
SLC_COUNT = 16
WINDOW = 512
Q_BLOCK = 128
NEG = -1e30
FORCE = 1e6
N_BUCKETS = 32
MAX_DISTANCE = 128
PEER_HEADS = 8
PEER_KEYS = 128
PEER_EXPERTS = PEER_KEYS * PEER_KEYS
PEER_QDIM = 256
PEER_TOPK = 16
DEPTH = 1
ALPHA = (2 * DEPTH) ** 0.25
LN_EPS = 1e-5
SQRT_HALF = 0.7071067811865476

LANES = 128
SUBLANES = 8
VMEM_LIMIT = 52 * 1024 * 1024

_FM_Q, _FM_V, _FM_G, _FM_Z, _FM_END = 0, 512, 768, 800, 1312
_TM_WIDTH = 640


def _gelu(x):
    return 0.5 * x * (1.0 + lax.erf(x * SQRT_HALF))


def _dot(a, b):
    return jnp.dot(a, b, preferred_element_type=F32)


def _dot_nt(a, b):
    return lax.dot_general(a, b, (((1,), (1,)), ((), ())), preferred_element_type=F32)


def _bucket_np(dist):
    dist = np.maximum(dist, 0)
    max_exact = N_BUCKETS // 2
    large = max_exact + np.floor(
        np.log(np.maximum(dist, 1) / max_exact) / math.log(MAX_DISTANCE / max_exact)
        * (N_BUCKETS - max_exact)).astype(np.int64)
    large = np.minimum(large, N_BUCKETS - 1)
    return np.where(dist < max_exact, dist, large)


ET_ROWS = 512
XS_ROWS = 896
XS_MASKED_ROW = 640
XW_ROWS = 1152
_CODE_MASKED = N_BUCKETS


def _bias_index_tables():
    qi = np.arange(Q_BLOCK)[None, :]

    def codes(dist, visible):
        return np.where(visible, _bucket_np(dist), _CODE_MASKED).astype(np.int32)

    d_cmp = qi - CMP_STRIDE * (np.arange(ET_ROWS)[:, None] - 256) - (CMP_BLOCK - 1)
    rho = np.arange(XS_ROWS)[:, None]
    d_sel = qi - (rho - 384)
    d_win = qi - (np.arange(XW_ROWS)[:, None] - 512)
    return np.concatenate([
        codes(d_cmp, d_cmp >= 0),
        codes(d_sel, (d_sel >= 0) & (rho < XS_MASKED_ROW)),
        codes(d_win, (d_win >= 0) & (d_win < WINDOW)),
    ], axis=0)


def _overlap_t(n_cmp_rows, n_slc):
    n = np.arange(n_cmp_rows)[None, :]
    j = np.arange(n_slc)[:, None]
    start = n * CMP_STRIDE
    end = start + CMP_BLOCK - 1
    return ((start <= j * SLC_BLOCK + SLC_BLOCK - 1) & (end >= j * SLC_BLOCK)).astype(np.float32)


def _bias_kernel(tab_ref, idx_ref, et_ref, xs_ref, xw_ref):
    g = pl.program_id(0)
    chunk = Q_BLOCK
    for r in range(GQA):
        h = g * GQA + r
        far = tab_ref[N_BUCKETS - 1, h]
        cols = slice(r * LANES, (r + 1) * LANES)
        for c in range((ET_ROWS + XS_ROWS + XW_ROWS) // chunk):
            idx = idx_ref[c * chunk:(c + 1) * chunk, :]
            acc = jnp.where(idx == _CODE_MASKED, NEG, 0.0)
            for b in range(N_BUCKETS - 1):
                acc = jnp.where(idx == b, tab_ref[b, h] - far, acc)
            row = c * chunk
            if row < ET_ROWS:
                et_ref[r, row:row + chunk, :] = acc
            elif row < ET_ROWS + XS_ROWS:
                xs_ref[0, row - ET_ROWS:row - ET_ROWS + chunk, cols] = acc
            else:
                row -= ET_ROWS + XS_ROWS
                xw_ref[0, row:row + chunk, cols] = acc


def _bias_tables(rel_table):
    idx = _bias_index_tables()
    qw = GQA * LANES
    return pl.pallas_call(
        _bias_kernel,
        grid=(KV_HEADS,),
        in_specs=[
            pl.BlockSpec(memory_space=pltpu.SMEM),
            pl.BlockSpec(idx.shape, lambda g: (0, 0)),
        ],
        out_specs=[
            pl.BlockSpec((GQA, ET_ROWS, Q_BLOCK), lambda g: (g, 0, 0)),
            pl.BlockSpec((1, XS_ROWS, qw), lambda g: (g, 0, 0)),
            pl.BlockSpec((1, XW_ROWS, qw), lambda g: (g, 0, 0)),
        ],
        out_shape=[
            jax.ShapeDtypeStruct((NSA_HEADS, ET_ROWS, Q_BLOCK), F32),
            jax.ShapeDtypeStruct((KV_HEADS, XS_ROWS, qw), F32),
            jax.ShapeDtypeStruct((KV_HEADS, XW_ROWS, qw), F32),
        ],
        name="bias_tables",
    )(rel_table, jnp.asarray(idx))


def _inproj_kernel(x_ref, wfm_ref, wtm_ref, qt_ref, vt_ref, gt_ref, zt_ref, kvc_ref,
                   kslc_ref, kwin_ref, *, tn, seq):
    xb = x_ref[...].astype(BF16)
    fm = _dot_nt(wfm_ref[...], xb)
    tm = _dot(xb, wtm_ref[...])
    qt_ref[...] = (fm[_FM_Q:_FM_V] * (HEAD_DIM ** -0.5)).astype(BF16)
    for c in range(2 * KV_HEADS):
        for s in range(tn // LANES):
            vt_ref[c, s] = fm[_FM_V + c * HEAD_DIM:_FM_V + (c + 1) * HEAD_DIM,
                              s * LANES:(s + 1) * LANES].astype(BF16)
    gt_ref[...] = jax.nn.sigmoid(fm[_FM_G:_FM_Z])
    zt_ref[...] = fm[_FM_Z:_FM_END]
    kvc_ref[0] = tm[:, 0:KV_WIDTH]
    kvc_ref[1] = tm[:, KV_WIDTH:2 * KV_WIDTH]
    t0 = (pl.program_id(0) * tn) % seq
    row = lax.broadcasted_iota(I32, (tn, LANES), 0) + t0
    lane = lax.broadcasted_iota(I32, (tn, LANES), 1)
    onehot = jnp.where(lane - HEAD_DIM == (row >> 6), 1.0, 0.0)
    for g in range(KV_HEADS):
        k = tm[:, 256 + g * LANES:256 + (g + 1) * LANES]
        kslc_ref[g] = jnp.where(lane < HEAD_DIM, k, onehot).astype(BF16)
    kwin_ref[...] = tm[:, 512:640].astype(BF16)


def _inproj(x2, w_in, seq):
    n = x2.shape[0]
    tn = 512
    w = w_in
    o = NSA_WIDTH
    w_q = w[:, 0:o]
    w_kc, w_vc, w_ks, w_vs, w_kw, w_vw = [w[:, o + i * KV_WIDTH:o + (i + 1) * KV_WIDTH] for i in range(6)]
    o += 6 * KV_WIDTH
    w_g = w[:, o:o + N_GATES]
    w_z = w[:, o + N_GATES:]
    zpad = jnp.zeros((D_MODEL, HEAD_DIM), w.dtype)
    wfm = jnp.concatenate(
        [w_q, w_vs, w_vw, w_g, jnp.zeros((D_MODEL, GATE_ROWS - N_GATES), w.dtype), w_z], axis=1).T.astype(BF16)
    wtm = jnp.concatenate(
        [w_kc, w_vc, w_ks[:, :HEAD_DIM], zpad, w_ks[:, HEAD_DIM:], zpad, w_kw], axis=1).astype(BF16)
    nt = n // LANES
    return pl.pallas_call(
        functools.partial(_inproj_kernel, tn=tn, seq=seq),
        grid=(n // tn,),
        in_specs=[
            pl.BlockSpec((tn, D_MODEL), lambda i: (i, 0)),
            pl.BlockSpec((_FM_END, D_MODEL), lambda i: (0, 0)),
            pl.BlockSpec((D_MODEL, _TM_WIDTH), lambda i: (0, 0)),
        ],
        out_specs=[
            pl.BlockSpec((NSA_WIDTH, tn), lambda i: (0, i)),
            pl.BlockSpec((2 * KV_HEADS, tn // LANES, HEAD_DIM, LANES), lambda i: (0, i, 0, 0)),
            pl.BlockSpec((GATE_ROWS, tn), lambda i: (0, i)),
            pl.BlockSpec((POOL_WIDTH, tn), lambda i: (0, i)),
            pl.BlockSpec((2, tn, KV_WIDTH), lambda i: (0, i, 0)),
            pl.BlockSpec((KV_HEADS, tn, LANES), lambda i: (0, i, 0)),
            pl.BlockSpec((tn, LANES), lambda i: (i, 0)),
        ],
        out_shape=[
            jax.ShapeDtypeStruct((NSA_WIDTH, n), BF16),
            jax.ShapeDtypeStruct((2 * KV_HEADS, nt, HEAD_DIM, LANES), BF16),
            jax.ShapeDtypeStruct((GATE_ROWS, n), F32),
            jax.ShapeDtypeStruct((POOL_WIDTH, n), F32),
            jax.ShapeDtypeStruct((2, n, KV_WIDTH), F32),
            jax.ShapeDtypeStruct((KV_HEADS, n, LANES), BF16),
            jax.ShapeDtypeStruct((n, LANES), BF16),
        ],
        compiler_params=pltpu.CompilerParams(vmem_limit_bytes=VMEM_LIMIT),
        name="inproj",
    )(x2, wfm, wtm)


def _compress_kernel(kc_ref, vc_ref, w2k_ref, w2v_ref, pek_ref, pev_ref, w1k_ref, w1v_ref,
                     wok0_ref, wok1_ref, wovt_ref, kcmp_ref, vcmpt_ref, *, n_rows):
    half = CMP_BLOCK // 2
    acc_k = jnp.zeros((n_rows, 4 * CMP_HIDDEN), F32)
    acc_v = jnp.zeros((n_rows, 4 * CMP_HIDDEN), F32)
    for l in range(half):
        rows = pl.ds(l, n_rows, stride=CMP_STRIDE)
        acc_k = acc_k + _dot(kc_ref[rows, :].astype(BF16), w2k_ref[l])
        acc_v = acc_v + _dot(vc_ref[rows, :].astype(BF16), w2v_ref[l])
    ck = _dot(pek_ref[...], w1k_ref[...])[0:1]
    cv = _dot(pev_ref[...], w1v_ref[...])[0:1]

    def hidden(acc, c, g):
        a = acc[:, g * CMP_HIDDEN:(g + 1) * CMP_HIDDEN]
        b = acc[:, (2 + g) * CMP_HIDDEN:(3 + g) * CMP_HIDDEN]
        return _gelu(a + pltpu.roll(b, n_rows - 1, axis=0) + c).astype(BF16)

    kcmp_ref[...] = (_dot(hidden(acc_k, ck, 0), wok0_ref[...])
                     + _dot(hidden(acc_k, ck, 1), wok1_ref[...])).astype(BF16)
    for g in range(KV_HEADS):
        vcmpt_ref[g] = _dot_nt(wovt_ref[...], hidden(acc_v, cv, g)).astype(BF16)


def _compress(kvc, batch, seq, pe_k, pe_v, k_w1, k_w2, v_w1, v_w2):
    n_rows = seq // CMP_STRIDE
    half = CMP_BLOCK // 2

    def blockdiag(w1):
        w = w1.reshape(CMP_BLOCK, HEAD_DIM, CMP_HIDDEN)
        z = jnp.zeros((half, HEAD_DIM, CMP_HIDDEN), w1.dtype)
        top = jnp.concatenate([w[:half], z, w[half:], z], axis=2)
        bot = jnp.concatenate([z, w[:half], z, w[half:]], axis=2)
        return jnp.concatenate([top, bot], axis=1).astype(BF16)

    def pe_rows(pe):
        flat = pe.reshape(1, CMP_BLOCK * HEAD_DIM)
        return jnp.concatenate([flat, jnp.zeros((7, flat.shape[1]), pe.dtype)], axis=0).astype(BF16)

    zo = jnp.zeros((CMP_HIDDEN, HEAD_DIM), k_w2.dtype)
    wok0 = jnp.concatenate([k_w2, zo], axis=1).astype(BF16)
    wok1 = jnp.concatenate([zo, k_w2], axis=1).astype(BF16)
    full = lambda *s: pl.BlockSpec(s, lambda b: (0,) * len(s))
    return pl.pallas_call(
        functools.partial(_compress_kernel, n_rows=n_rows),
        grid=(batch,),
        in_specs=[
            pl.BlockSpec((None, seq, KV_WIDTH), lambda b: (0, b, 0)),
            pl.BlockSpec((None, seq, KV_WIDTH), lambda b: (1, b, 0)),
            full(half, KV_WIDTH, 4 * CMP_HIDDEN), full(half, KV_WIDTH, 4 * CMP_HIDDEN),
            full(8, CMP_BLOCK * HEAD_DIM), full(8, CMP_BLOCK * HEAD_DIM),
            full(CMP_BLOCK * HEAD_DIM, CMP_HIDDEN), full(CMP_BLOCK * HEAD_DIM, CMP_HIDDEN),
            full(CMP_HIDDEN, KV_WIDTH), full(CMP_HIDDEN, KV_WIDTH), full(HEAD_DIM, CMP_HIDDEN),
        ],
        out_specs=[
            pl.BlockSpec((None, n_rows, KV_WIDTH), lambda b: (b, 0, 0)),
            pl.BlockSpec((None, KV_HEADS, HEAD_DIM, n_rows), lambda b: (b, 0, 0, 0)),
        ],
        out_shape=[
            jax.ShapeDtypeStruct((batch, n_rows, KV_WIDTH), BF16),
            jax.ShapeDtypeStruct((batch, KV_HEADS, HEAD_DIM, n_rows), BF16),
        ],
        compiler_params=pltpu.CompilerParams(vmem_limit_bytes=VMEM_LIMIT),
        name="compress",
    )(kvc, kvc, blockdiag(k_w1), blockdiag(v_w1), pe_rows(pe_k), pe_rows(pe_v),
      k_w1.astype(BF16), v_w1.astype(BF16), wok0, wok1, v_w2.T.astype(BF16))


SUPER_TILE = 2 * Q_BLOCK
WIN_KEYS = WINDOW + Q_BLOCK


def _softmax_init(s, vt):
    m = jnp.max(s, axis=0, keepdims=True)
    p = jnp.exp(s - m)
    return m, jnp.sum(p, axis=0, keepdims=True), _dot(vt, p.astype(BF16))


def _softmax_step(s, vt, state):
    m, l, acc = state
    m_new = jnp.maximum(m, jnp.max(s, axis=0, keepdims=True))
    alpha = jnp.exp(m - m_new)
    p = jnp.exp(s - m_new)
    return m_new, alpha * l + jnp.sum(p, axis=0, keepdims=True), alpha * acc + _dot(vt, p.astype(BF16))


def _softmax_merge(a, b):
    m = jnp.maximum(a[0], b[0])
    wa = jnp.exp(a[0] - m)
    wb = jnp.exp(b[0] - m)
    return (wa * a[2] + wb * b[2]) * (1.0 / (wa * a[1] + wb * b[1]))


def _nsa_kernel(qt_ref, kcmp_ref, vcmpt_ref, kslc_ref, kwin_ref, vslct_ref, vwint_ref, gt_ref,
                et_ref, xs_ref, xw_ref, ovt_ref, out_ref, *, n_cmp_rows, n_slc):
    g = pl.program_id(1)
    qb = pl.program_id(2)
    qw = GQA * Q_BLOCK
    q4 = qt_ref[...]
    qcat = jnp.concatenate([q4[r * HEAD_DIM:(r + 1) * HEAD_DIM] for r in range(GQA)], axis=1)
    zq = jnp.zeros_like(qcat)
    qg = jnp.concatenate([jnp.where(g == 0, qcat, zq), jnp.where(g == 1, qcat, zq)], axis=0)

    s = _dot(kcmp_ref[...], qg)
    k0 = jnp.maximum(qb * Q_BLOCK - WINDOW, 0)
    row_w = pl.multiple_of(WINDOW - qb * Q_BLOCK + k0, Q_BLOCK)
    s_win = _dot(kwin_ref[pl.ds(pl.multiple_of(k0, Q_BLOCK), WIN_KEYS), :], qg)

    off = pl.multiple_of(256 - 8 * qb, 8)
    s = s + jnp.concatenate([et_ref[r, pl.ds(off, n_cmp_rows), :] for r in range(GQA)], axis=1)
    m = jnp.max(s, axis=0, keepdims=True)
    p = jnp.exp(s - m)
    l = jnp.sum(p, axis=0, keepdims=True)
    p = p * jnp.where(m > 0.5 * NEG, 1.0 / l, 0.0)
    o_cmp = _dot(vcmpt_ref[...], p.astype(BF16))
    p_sum = p[:, 0:Q_BLOCK]
    for r in range(1, GQA):
        p_sum = p_sum + p[:, r * Q_BLOCK:(r + 1) * Q_BLOCK]
    imp = jnp.dot(ovt_ref[...], p_sum, preferred_element_type=F32,
                  precision=lax.Precision.HIGHEST)

    kt0 = k0 >> 7
    vt = jnp.concatenate([vwint_ref[kt0 + i] for i in range(WIN_KEYS // Q_BLOCK)], axis=1)
    m, l, acc = _softmax_init(s_win + xw_ref[pl.ds(row_w, WIN_KEYS), :], vt)
    o_win = acc * (1.0 / l)

    j_i = lax.broadcasted_iota(I32, (n_slc, Q_BLOCK), 0)
    t1 = qb * Q_BLOCK + lax.broadcasted_iota(I32, (n_slc, Q_BLOCK), 1)
    cur = t1 >> 6
    forced = (j_i == 0) | (j_i == cur) | (j_i == cur - 1)
    imp = jnp.where(forced, FORCE, imp)
    imp = jnp.where(j_i <= cur, imp, NEG)
    cnt = jnp.zeros((n_slc, Q_BLOCK), F32)
    for jp in range(n_slc):
        row = imp[jp:jp + 1, :]
        beats = (row > imp) | ((row == imp) & (j_i > jp))
        cnt = cnt + jnp.where(beats, 1.0, 0.0)
    selneg = jnp.where(cnt < float(SLC_COUNT), 0.0, NEG).astype(BF16)
    qsel = [qcat, jnp.concatenate([selneg] * GQA, axis=1)]
    if n_slc < LANES - HEAD_DIM:
        qsel.append(jnp.zeros((LANES - HEAD_DIM - n_slc, qw), BF16))
    qsel = jnp.concatenate(qsel, axis=0)

    def sel_scores(st):
        k = kslc_ref[pl.ds(pl.multiple_of(st * SUPER_TILE, SUPER_TILE), SUPER_TILE), :]
        vt = jnp.concatenate([vslct_ref[2 * st], vslct_ref[2 * st + 1]], axis=1)
        return _dot(k, qsel), vt

    odd = qb & 1
    st_diag = qb >> 1
    st_prev = jnp.maximum(st_diag - 1, 0)
    n_far = st_prev
    row_a = pl.multiple_of(384 - Q_BLOCK * odd, Q_BLOCK)
    row_b = pl.multiple_of(jnp.where(st_diag >= 1, Q_BLOCK - Q_BLOCK * odd, XS_MASKED_ROW), Q_BLOCK)
    s_a, vt_a = sel_scores(st_diag)
    s_b, vt_b = sel_scores(st_prev)
    chain_a = _softmax_init(s_a + xs_ref[pl.ds(row_a, SUPER_TILE), :], vt_a)
    chain_b = _softmax_init(s_b + xs_ref[pl.ds(row_b, SUPER_TILE), :], vt_b)

    def far_pair(i, chains):
        a, b = chains
        s_a, vt_a = sel_scores(2 * i)
        s_b, vt_b = sel_scores(2 * i + 1)
        return _softmax_step(s_a, vt_a, a), _softmax_step(s_b, vt_b, b)

    chain_a, chain_b = lax.fori_loop(0, n_far >> 1, far_pair, (chain_a, chain_b))
    chain_a = lax.cond((n_far & 1) == 1,
                       lambda a: _softmax_step(*sel_scores(n_far - 1), a), lambda a: a, chain_a)
    o_slc = _softmax_merge(chain_a, chain_b)

    outs = []
    for r in range(GQA):
        cols = slice(r * Q_BLOCK, (r + 1) * Q_BLOCK)
        gate = [gt_ref[pl.ds(c * NSA_HEADS + g * GQA + r, 1), :] for c in range(3)]
        outs.append(gate[0] * o_cmp[:, cols] + gate[1] * o_slc[:, cols] + gate[2] * o_win[:, cols])
    out_ref[...] = jnp.concatenate(outs, axis=0).astype(BF16)


def _nsa(qt, kcmp, vcmpt, kslc, kwin, vt, gt, et, xs, xw, batch, seq):
    assert seq % SUPER_TILE == 0 and seq >= WIN_KEYS and seq // CMP_STRIDE <= 256 and seq // SLC_BLOCK <= HEAD_DIM
    nq = seq // Q_BLOCK
    n_cmp_rows = seq // CMP_STRIDE
    n_slc = seq // SLC_BLOCK
    n = batch * seq
    ovt = jnp.asarray(_overlap_t(n_cmp_rows, n_slc))
    gw = GQA * HEAD_DIM
    return pl.pallas_call(
        functools.partial(_nsa_kernel, n_cmp_rows=n_cmp_rows, n_slc=n_slc),
        grid=(batch, KV_HEADS, nq),
        in_specs=[
            pl.BlockSpec((gw, Q_BLOCK), lambda b, g, q: (g, b * nq + q)),
            pl.BlockSpec((None, n_cmp_rows, KV_WIDTH), lambda b, g, q: (b, 0, 0)),
            pl.BlockSpec((None, None, HEAD_DIM, n_cmp_rows), lambda b, g, q: (b, g, 0, 0)),
            pl.BlockSpec((None, seq, LANES), lambda b, g, q: (g, b, 0)),
            pl.BlockSpec((seq, LANES), lambda b, g, q: (b, 0)),
            pl.BlockSpec((None, nq, HEAD_DIM, LANES), lambda b, g, q: (g, b, 0, 0)),
            pl.BlockSpec((None, nq, HEAD_DIM, LANES), lambda b, g, q: (KV_HEADS + g, b, 0, 0)),
            pl.BlockSpec((GATE_ROWS, Q_BLOCK), lambda b, g, q: (0, b * nq + q)),
            pl.BlockSpec((GQA, ET_ROWS, Q_BLOCK), lambda b, g, q: (g, 0, 0)),
            pl.BlockSpec((None, XS_ROWS, GQA * LANES), lambda b, g, q: (g, 0, 0)),
            pl.BlockSpec((None, XW_ROWS, GQA * LANES), lambda b, g, q: (g, 0, 0)),
            pl.BlockSpec((n_slc, n_cmp_rows), lambda b, g, q: (0, 0)),
        ],
        out_specs=pl.BlockSpec((gw, Q_BLOCK), lambda b, g, q: (g, b * nq + q)),
        out_shape=jax.ShapeDtypeStruct((NSA_WIDTH, n), BF16),
        compiler_params=pltpu.CompilerParams(vmem_limit_bytes=VMEM_LIMIT),
        name="nsa",
    )(qt, kcmp, vcmpt, kslc, kwin, vt, vt, gt, et, xs, xw, ovt)


def _layer_norm_fm(h, g_ref, b_ref, reps):
    mu = jnp.mean(h, axis=0, keepdims=True)
    hc = h - mu
    var = jnp.mean(hc * hc, axis=0, keepdims=True)
    gain = jnp.concatenate([g_ref[...]] * reps, axis=1)
    bias = jnp.concatenate([b_ref[...]] * reps, axis=1)
    return hc * lax.rsqrt(var + LN_EPS) * gain + bias


def _mix_kernel(nsat_ref, zt_ref, halo_ref, x_ref, won_ref, wop_ref, pwt_ref, ps_ref, g_ref, b_ref,
                h_ref, hb_ref, *, tn, seq):
    t0 = (pl.program_id(0) * tn) % seq
    reps = tn // LANES
    zc = jnp.concatenate([halo_ref[...], zt_ref[...]], axis=1)
    t_ext = t0 - LANES + lax.broadcasted_iota(I32, (1, LANES + tn), 1)
    zc = jnp.where(t_ext >= 0, zc, 0.0)
    pooled = []
    for gi, win in enumerate(POOL_WINDOWS):
        zg = zc[gi * POOL_GROUP_DIM:(gi + 1) * POOL_GROUP_DIM]
        s = zg
        sh = 1
        while sh < win:
            s = s + pltpu.roll(s, sh, axis=1)
            sh *= 2
        cnt = jnp.clip(t_ext + 1, 1, win).astype(F32)
        y = (s / cnt - zg)[:, LANES:]
        scale = jnp.concatenate([ps_ref[gi * POOL_GROUP_DIM:(gi + 1) * POOL_GROUP_DIM]] * reps, axis=1)
        pooled.append((_dot(pwt_ref[gi], y.astype(BF16)) * scale).astype(BF16))
    pool_t = jnp.concatenate(pooled, axis=0)
    mix = _dot(won_ref[...], nsat_ref[...]) + _dot(wop_ref[...], pool_t)
    h = ALPHA * x_ref[...].T + mix
    h = _layer_norm_fm(h, g_ref, b_ref, reps)
    h_ref[...] = h
    hb_ref[...] = h.astype(BF16)


def _lane_bcast(v):
    return jnp.broadcast_to(v.reshape(-1, 1), (v.shape[0], LANES)).astype(F32)


def _mix(nsat, zt, x2, w_out, pool_w, pool_scale, ln_g, ln_b, seq):
    n = x2.shape[0]
    tn = 256
    wot = w_out.T.astype(BF16)
    full = lambda *s: pl.BlockSpec(s, lambda i: (0,) * len(s))
    return pl.pallas_call(
        functools.partial(_mix_kernel, tn=tn, seq=seq),
        grid=(n // tn,),
        in_specs=[
            pl.BlockSpec((NSA_WIDTH, tn), lambda i: (0, i)),
            pl.BlockSpec((POOL_WIDTH, tn), lambda i: (0, i)),
            pl.BlockSpec((POOL_WIDTH, LANES), lambda i: (0, jnp.maximum(i * (tn // LANES) - 1, 0))),
            pl.BlockSpec((tn, D_MODEL), lambda i: (i, 0)),
            full(D_MODEL, NSA_WIDTH), full(D_MODEL, POOL_WIDTH),
            full(len(POOL_WINDOWS), POOL_GROUP_DIM, POOL_GROUP_DIM),
            full(POOL_WIDTH, LANES), full(D_MODEL, LANES), full(D_MODEL, LANES),
        ],
        out_specs=[pl.BlockSpec((D_MODEL, tn), lambda i: (0, i))] * 2,
        out_shape=[jax.ShapeDtypeStruct((D_MODEL, n), F32), jax.ShapeDtypeStruct((D_MODEL, n), BF16)],
        compiler_params=pltpu.CompilerParams(vmem_limit_bytes=VMEM_LIMIT),
        name="mix",
    )(nsat, zt, zt, x2, wot[:, :NSA_WIDTH], wot[:, NSA_WIDTH:],
      jnp.swapaxes(pool_w, 1, 2).astype(BF16), _lane_bcast(pool_scale), _lane_bcast(ln_g), _lane_bcast(ln_b))


ROUTE_HEAD_GROUP = 4
_REMOVED = -(2.0 ** 100)
_PAD_SCORE = -3e38


def _top16_by_index(s):
    rows = lax.broadcasted_iota(I32, s.shape, 0).astype(F32)
    row16 = lax.broadcasted_iota(I32, (PEER_TOPK, s.shape[1]), 0)
    cur = s
    rank = jnp.full(s.shape, float(PEER_TOPK), F32)
    vals = jnp.zeros((PEER_TOPK, s.shape[1]), F32)
    for r in range(PEER_TOPK):
        mx = jnp.max(cur, axis=0, keepdims=True)
        first = jnp.min(jnp.where(cur == mx, rows, float(s.shape[0])), axis=0, keepdims=True)
        hit = rows == first
        vals = jnp.where(row16 == r, mx, vals)
        rank = jnp.where(hit, float(r), rank)
        cur = jnp.where(hit, -jnp.inf, cur)
    return vals, rank, jnp.full((1, s.shape[1]), float(PEER_TOPK), F32)


def _top16_by_value(scores):
    width = scores[0].shape[1]
    row16 = lax.broadcasted_iota(I32, (PEER_TOPK, width), 0)
    cur = list(scores)
    vals = [jnp.zeros((PEER_TOPK, width), F32) for _ in scores]
    for r in range(PEER_TOPK):
        mx = [jnp.max(c, axis=0, keepdims=True) for c in cur]
        vals = [jnp.where(row16 == r, m, v) for m, v in zip(mx, vals)]
        cur = [jnp.where(c == m, _REMOVED * (r + 1), c) for c, m in zip(cur, mx)]
    out = []
    for c, v in zip(cur, vals):
        member = c <= _REMOVED
        rank = jnp.where(member, c * (1.0 / _REMOVED) - 1.0, float(PEER_TOPK))
        out.append((v, rank, jnp.sum(jnp.where(member, 1.0, 0.0), axis=0, keepdims=True)))
    return out


def _candidate_table(v1, v2):
    tn = v1.shape[1]
    row8 = lax.broadcasted_iota(I32, (SUBLANES, tn), 0)
    groups = [v1[0:1, :] + v2]
    real = [PEER_TOPK]
    for a in range(1, SUBLANES):
        nb = PEER_TOPK // (a + 1)
        groups.append(jnp.where(row8 < nb, v1[a:a + 1, :] + v2[0:SUBLANES, :], _PAD_SCORE))
        real.append(nb)
    groups.append(v1[SUBLANES:, :] + v2[0:1, :])
    real.append(PEER_TOPK - SUBLANES)
    return jnp.concatenate(groups, axis=0), real


def _peer_scores(h, hb_ref, wqt_ref, sk1_ref, sk2_ref):
    half = PEER_QDIM // 2
    q = _dot(wqt_ref[h * PEER_QDIM:(h + 1) * PEER_QDIM, :], hb_ref[...])
    return _dot(sk1_ref[...], q[:half].astype(BF16)), _dot(sk2_ref[...], q[half:].astype(BF16))


def _route_heads(heads, scores, outs, *, tn, by_value):
    n_ref, a_ref, r2_ref, b_ref = outs
    want = float(PEER_TOPK)
    miscount = jnp.zeros((1, tn), F32)
    per_head = []
    for h, (s1, s2) in zip(heads, scores):
        if by_value:
            (v1, rank1, count1), (v2, rank2, count2) = _top16_by_value([s1, s2])
        else:
            (v1, rank1, count1), (v2, rank2, count2) = _top16_by_index(s1), _top16_by_index(s2)
        miscount = jnp.maximum(miscount, jnp.maximum(jnp.abs(count1 - want), jnp.abs(count2 - want)))
        cs, real = _candidate_table(v1, v2)
        per_head.append((h, s1, s2, v1, v2, rank1, rank2, cs))

    if by_value:
        cur = [ph[-1] for ph in per_head]
        for _ in range(PEER_TOPK):
            cur = [jnp.where(c == jnp.max(c, axis=0, keepdims=True), -jnp.inf, c) for c in cur]
        sels = [c == -jnp.inf for c in cur]
    else:
        sels = []
        for ph in per_head:
            cs = ph[-1]
            c_i = lax.broadcasted_iota(I32, cs.shape, 0)
            cnt = jnp.zeros(cs.shape, F32)
            base = 0
            for g, n_real in enumerate(real):
                for c in range(base, base + n_real):
                    row = cs[c:c + 1, :]
                    beats = (row > cs) | ((row == cs) & (c_i > c))
                    cnt = cnt + jnp.where(beats, 1.0, 0.0)
                base += PEER_TOPK if g == 0 else SUBLANES
            sels.append(cnt < want)

    for (h, s1, s2, v1, v2, rank1, rank2, cs), sel in zip(per_head, sels):
        self_f = jnp.where(sel, 1.0, 0.0)
        miscount = jnp.maximum(miscount, jnp.abs(jnp.sum(self_f, axis=0, keepdims=True) - want))
        top = v1[0:1, :] + v2[0:1, :]
        z = jnp.sum(jnp.where(sel, jnp.exp(cs - top), 0.0), axis=0, keepdims=True)
        nmap = jnp.zeros((PEER_KEYS, tn), F32)
        base = 0
        for a in range(PEER_TOPK):
            if a < SUBLANES:
                size = PEER_TOPK if a == 0 else SUBLANES
                n_a = jnp.sum(self_f[base:base + size, :], axis=0, keepdims=True)
                base += size
            else:
                n_a = self_f[base + a - SUBLANES:base + a - SUBLANES + 1, :]
            nmap = jnp.where(rank1 == float(a), n_a, nmap)
        n_ref[h] = nmap
        a_ref[h] = jnp.where(rank1 < want, jnp.exp(s1 - v1[0:1, :]), 0.0)
        r2_ref[h] = rank2.astype(BF16)
        b_ref[h] = (jnp.where(rank2 < want, jnp.exp(s2 - v2[0:1, :]), 0.0) / z).astype(BF16)
    return miscount


def _route_kernel(hb_ref, wqt_ref, sk1_ref, sk2_ref, n_ref, a_ref, r2_ref, b_ref, *, tn):
    scores = functools.partial(_peer_scores, hb_ref=hb_ref, wqt_ref=wqt_ref, sk1_ref=sk1_ref, sk2_ref=sk2_ref)
    outs = (n_ref, a_ref, r2_ref, b_ref)
    groups = [range(h0, h0 + ROUTE_HEAD_GROUP) for h0 in range(0, PEER_HEADS, ROUTE_HEAD_GROUP)]
    miscount = jnp.zeros((1, tn), F32)
    ready = [scores(h) for h in groups[0]]
    for gi, heads in enumerate(groups):
        ahead = [scores(h) for h in groups[gi + 1]] if gi + 1 < len(groups) else None
        miscount = jnp.maximum(miscount, _route_heads(heads, ready, outs, tn=tn, by_value=True))
        ready = ahead

    @pl.when(jnp.max(miscount) > 0.0)
    def _():
        for h in range(PEER_HEADS):
            _route_heads([h], [scores(h)], outs, tn=tn, by_value=False)


def _route(hb, wq, subkeys):
    n = hb.shape[1]
    tn = LANES
    full = lambda *s: pl.BlockSpec(s, lambda i: (0,) * len(s))
    tab = lambda: pl.BlockSpec((PEER_HEADS, PEER_KEYS, tn), lambda i: (0, 0, i))
    shape = (PEER_HEADS, PEER_KEYS, n)
    return pl.pallas_call(
        functools.partial(_route_kernel, tn=tn),
        grid=(n // tn,),
        in_specs=[
            pl.BlockSpec((D_MODEL, tn), lambda i: (0, i)),
            full(PEER_HEADS * PEER_QDIM, D_MODEL), full(PEER_KEYS, PEER_QDIM // 2), full(PEER_KEYS, PEER_QDIM // 2),
        ],
        out_specs=[tab(), tab(), tab(), tab()],
        out_shape=[
            jax.ShapeDtypeStruct(shape, F32),
            jax.ShapeDtypeStruct(shape, F32),
            jax.ShapeDtypeStruct(shape, BF16),
            jax.ShapeDtypeStruct(shape, BF16),
        ],
        compiler_params=pltpu.CompilerParams(vmem_limit_bytes=VMEM_LIMIT),
        name="peer_route",
    )(hb, wq.T.astype(BF16), subkeys[0].astype(BF16), subkeys[1].astype(BF16))


def _experts_kernel(hb_ref, u_ref, vt_ref, n_ref, a_ref, r2_ref, b_ref, h_ref, g_ref, bb_ref,
                    out_ref, acc_ref, *, tn, te):
    j = pl.program_id(1)

    @pl.when(j == 0)
    def _():
        acc_ref[...] = jnp.zeros_like(acc_ref)

    act = _gelu(_dot(u_ref[...], hb_ref[...]))
    chunks = te // PEER_KEYS
    zero = jnp.zeros((PEER_KEYS, tn), BF16)

    def key_row(ref, h, i1):
        row = jnp.broadcast_to(ref[h, pl.ds(i1, 1), :], (2 * SUBLANES, tn)).astype(BF16)
        return jnp.concatenate([row] * (PEER_KEYS // (2 * SUBLANES)), axis=0)

    ps = []
    for c in range(chunks):
        i1 = j * chunks + c
        w = zero
        for h in range(PEER_HEADS):
            w = w + key_row(a_ref, h, i1) * jnp.where(r2_ref[h] < key_row(n_ref, h, i1), b_ref[h], zero)
        ps.append(w * act[c * PEER_KEYS:(c + 1) * PEER_KEYS].astype(BF16))
    acc_ref[...] += _dot(vt_ref[...], jnp.concatenate(ps, axis=0))

    @pl.when(j == pl.num_programs(1) - 1)
    def _():
        h = ALPHA * h_ref[...] + acc_ref[...]
        out_ref[...] = _layer_norm_fm(h, g_ref, bb_ref, tn // LANES).T


def _experts(hb, hf, u_b, vt_b, tabs, ln_g, ln_b):
    n = hb.shape[1]
    tn, te = 512, 1024
    tab = lambda: pl.BlockSpec((PEER_HEADS, PEER_KEYS, tn), lambda i, j: (0, 0, i))
    return pl.pallas_call(
        functools.partial(_experts_kernel, tn=tn, te=te),
        grid=(n // tn, PEER_EXPERTS // te),
        in_specs=[
            pl.BlockSpec((D_MODEL, tn), lambda i, j: (0, i)),
            pl.BlockSpec((te, D_MODEL), lambda i, j: (j, 0)),
            pl.BlockSpec((D_MODEL, te), lambda i, j: (0, j)),
            tab(), tab(), tab(), tab(),
            pl.BlockSpec((D_MODEL, tn), lambda i, j: (0, i)),
            pl.BlockSpec((D_MODEL, LANES), lambda i, j: (0, 0)),
            pl.BlockSpec((D_MODEL, LANES), lambda i, j: (0, 0)),
        ],
        out_specs=pl.BlockSpec((tn, D_MODEL), lambda i, j: (i, 0)),
        out_shape=jax.ShapeDtypeStruct((n, D_MODEL), F32),
        scratch_shapes=[pltpu.VMEM((D_MODEL, tn), F32)],
        compiler_params=pltpu.CompilerParams(
            dimension_semantics=("arbitrary", "arbitrary"), vmem_limit_bytes=VMEM_LIMIT),
        name="peer_experts",
    )(hb, u_b, vt_b, *tabs, hf, _lane_bcast(ln_g), _lane_bcast(ln_b))


def kernel(x, w_in, rel_table, cmp_pe_k, cmp_pe_v, cmp_k_w1, cmp_k_w2, cmp_v_w1, cmp_v_w2, pool_w, pool_scale,
           w_out, ln1_g, ln1_b, peer_wq, peer_subkeys, peer_u, peer_v, ln2_g, ln2_b):
    batch, seq, _ = x.shape
    assert w_in.shape[0] == DEPTH == 1 and seq % 512 == 0 and seq // CMP_STRIDE <= 256
    x2 = x.reshape(batch * seq, D_MODEL)
    et, xs, xw = _bias_tables(rel_table)
    qt, vt, gt, zt, kvc, kslc, kwin = _inproj(x2, w_in[0], seq)
    kcmp, vcmpt = _compress(kvc, batch, seq, cmp_pe_k[0], cmp_pe_v[0], cmp_k_w1[0], cmp_k_w2[0],
                            cmp_v_w1[0], cmp_v_w2[0])
    nsat = _nsa(qt, kcmp, vcmpt, kslc, kwin, vt, gt, et, xs, xw, batch, seq)
    hf, hb = _mix(nsat, zt, x2, w_out[0], pool_w[0], pool_scale[0], ln1_g[0], ln1_b[0], seq)
    tabs = _route(hb, peer_wq[0], peer_subkeys[0])
    y = _experts(hb, hf, peer_u[0].astype(BF16), peer_v[0].T.astype(BF16), tabs, ln2_g[0], ln2_b[0])
    return y.reshape(batch, seq, D_MODEL)
```

```python
import functools
import math

import numpy as np
import jax
import jax.numpy as jnp
from jax import lax
from jax.experimental import pallas as pl
from jax.experimental.pallas import tpu as pltpu

F32 = jnp.float32
BF16 = jnp.bfloat16
I32 = jnp.int32

D_MODEL = 1024
NSA_HEADS = 8
KV_HEADS = 2
GQA = NSA_HEADS // KV_HEADS
HEAD_DIM = 64
NSA_WIDTH = NSA_HEADS * HEAD_DIM
KV_WIDTH = KV_HEADS * HEAD_DIM
POOL_WIDTH = D_MODEL - NSA_WIDTH
POOL_WINDOWS = (2, 4, 8, 16)
POOL_GROUP_DIM = POOL_WIDTH // len(POOL_WINDOWS)
N_GATES = 3 * NSA_HEADS
GATE_ROWS = 32
CMP_BLOCK = 32
CMP_STRIDE = 16
CMP_HIDDEN = 256
SLC_BLOCK = 64
SLC_COUNT = 16
WINDOW = 512
Q_BLOCK = 128
NEG = -1e30
FORCE = 1e6
N_BUCKETS = 32
MAX_DISTANCE = 128
PEER_HEADS = 8
PEER_KEYS = 128
PEER_EXPERTS = PEER_KEYS * PEER_KEYS
PEER_QDIM = 256
PEER_TOPK = 16
DEPTH = 1
ALPHA = (2 * DEPTH) ** 0.25
LN_EPS = 1e-5
SQRT_HALF = 0.7071067811865476

LANES = 128
SUBLANES = 8
EXPERT_PARTS = 2
VMEM_LIMIT = 52 * 1024 * 1024

_FM_Q, _FM_V, _FM_G, _FM_Z, _FM_END = 0, 512, 768, 800, 1312
_TM_WIDTH = 640


def _gelu(x):
    return 0.5 * x * (1.0 + lax.erf(x * SQRT_HALF))


def _dot(a, b):
    return jnp.dot(a, b, preferred_element_type=F32)


def _dot_nt(a, b):
    return lax.dot_general(a, b, (((1,), (1,)), ((), ())), preferred_element_type=F32)


def _bucket_np(dist):
    dist = np.maximum(dist, 0)
    max_exact = N_BUCKETS // 2
    large = max_exact + np.floor(
        np.log(np.maximum(dist, 1) / max_exact) / math.log(MAX_DISTANCE / max_exact)
        * (N_BUCKETS - max_exact)).astype(np.int64)
    large = np.minimum(large, N_BUCKETS - 1)
    return np.where(dist < max_exact, dist, large)


ET_ROWS = 512
XS_ROWS = 896
XS_MASKED_ROW = 640
XW_ROWS = 1152
_CODE_MASKED = N_BUCKETS


def _bias_index_tables():
    qi = np.arange(Q_BLOCK)[None, :]

    def codes(dist, visible):
        return np.where(visible, _bucket_np(dist), _CODE_MASKED).astype(np.int32)

    d_cmp = qi - CMP_STRIDE * (np.arange(ET_ROWS)[:, None] - 256) - (CMP_BLOCK - 1)
    rho = np.arange(XS_ROWS)[:, None]
    d_sel = qi - (rho - 384)
    d_win = qi - (np.arange(XW_ROWS)[:, None] - 512)
    return np.concatenate([
        codes(d_cmp, d_cmp >= 0),
        codes(d_sel, (d_sel >= 0) & (rho < XS_MASKED_ROW)),
        codes(d_win, (d_win >= 0) & (d_win < WINDOW)),
    ], axis=0)


def _overlap_t(n_cmp_rows, n_slc):
    n = np.arange(n_cmp_rows)[None, :]
    j = np.arange(n_slc)[:, None]
    start = n * CMP_STRIDE
    end = start + CMP_BLOCK - 1
    return ((start <= j * SLC_BLOCK + SLC_BLOCK - 1) & (end >= j * SLC_BLOCK)).astype(np.float32)


def _bias_kernel(tab_ref, idx_ref, et_ref, xs_ref, xw_ref):
    g = pl.program_id(0)
    chunk = Q_BLOCK
    for r in range(GQA):
        h = g * GQA + r
        far = tab_ref[N_BUCKETS - 1, h]
        cols = slice(r * LANES, (r + 1) * LANES)
        for c in range((ET_ROWS + XS_ROWS + XW_ROWS) // chunk):
            idx = idx_ref[c * chunk:(c + 1) * chunk, :]
            acc = jnp.where(idx == _CODE_MASKED, NEG, 0.0)
            for b in range(N_BUCKETS - 1):
                acc = jnp.where(idx == b, tab_ref[b, h] - far, acc)
            row = c * chunk
            if row < ET_ROWS:
                et_ref[r, row:row + chunk, :] = acc
            elif row < ET_ROWS + XS_ROWS:
                xs_ref[0, row - ET_ROWS:row - ET_ROWS + chunk, cols] = acc
            else:
                row -= ET_ROWS + XS_ROWS
                xw_ref[0, row:row + chunk, cols] = acc


def _bias_tables(rel_table):
    idx = _bias_index_tables()
    qw = GQA * LANES
    return pl.pallas_call(
        _bias_kernel,
        grid=(KV_HEADS,),
        in_specs=[
            pl.BlockSpec(memory_space=pltpu.SMEM),
            pl.BlockSpec(idx.shape, lambda g: (0, 0)),
        ],
        out_specs=[
            pl.BlockSpec((GQA, ET_ROWS, Q_BLOCK), lambda g: (g, 0, 0)),
            pl.BlockSpec((1, XS_ROWS, qw), lambda g: (g, 0, 0)),
            pl.BlockSpec((1, XW_ROWS, qw), lambda g: (g, 0, 0)),
        ],
        out_shape=[
            jax.ShapeDtypeStruct((NSA_HEADS, ET_ROWS, Q_BLOCK), F32),
            jax.ShapeDtypeStruct((KV_HEADS, XS_ROWS, qw), F32),
            jax.ShapeDtypeStruct((KV_HEADS, XW_ROWS, qw), F32),
        ],
        name="bias_tables",
    )(rel_table, jnp.asarray(idx))


def _inproj_kernel(x_ref, wfm_ref, wtm_ref, qt_ref, vt_ref, gt_ref, zt_ref, kvc_ref,
                   kslc_ref, kwin_ref, *, tn, seq):
    xb = x_ref[...].astype(BF16)
    fm = _dot_nt(wfm_ref[...], xb)
    tm = _dot(xb, wtm_ref[...])
    qt_ref[...] = (fm[_FM_Q:_FM_V] * (HEAD_DIM ** -0.5)).astype(BF16)
    for c in range(2 * KV_HEADS):
        for s in range(tn // LANES):
            vt_ref[c, s] = fm[_FM_V + c * HEAD_DIM:_FM_V + (c + 1) * HEAD_DIM,
                              s * LANES:(s + 1) * LANES].astype(BF16)
    gt_ref[...] = jax.nn.sigmoid(fm[_FM_G:_FM_Z])
    zt_ref[...] = fm[_FM_Z:_FM_END]
    kvc_ref[0] = tm[:, 0:KV_WIDTH]
    kvc_ref[1] = tm[:, KV_WIDTH:2 * KV_WIDTH]
    t0 = (pl.program_id(0) * tn) % seq
    row = lax.broadcasted_iota(I32, (tn, LANES), 0) + t0
    lane = lax.broadcasted_iota(I32, (tn, LANES), 1)
    onehot = jnp.where(lane - HEAD_DIM == (row >> 6), 1.0, 0.0)
    for g in range(KV_HEADS):
        k = tm[:, 256 + g * LANES:256 + (g + 1) * LANES]
        kslc_ref[g] = jnp.where(lane < HEAD_DIM, k, onehot).astype(BF16)
    kwin_ref[...] = tm[:, 512:640].astype(BF16)


def _inproj(x2, w_in, seq):
    n = x2.shape[0]
    tn = 512
    w = w_in
    o = NSA_WIDTH
    w_q = w[:, 0:o]
    w_kc, w_vc, w_ks, w_vs, w_kw, w_vw = [w[:, o + i * KV_WIDTH:o + (i + 1) * KV_WIDTH] for i in range(6)]
    o += 6 * KV_WIDTH
    w_g = w[:, o:o + N_GATES]
    w_z = w[:, o + N_GATES:]
    zpad = jnp.zeros((D_MODEL, HEAD_DIM), w.dtype)
    wfm = jnp.concatenate(
        [w_q, w_vs, w_vw, w_g, jnp.zeros((D_MODEL, GATE_ROWS - N_GATES), w.dtype), w_z], axis=1).T.astype(BF16)
    wtm = jnp.concatenate(
        [w_kc, w_vc, w_ks[:, :HEAD_DIM], zpad, w_ks[:, HEAD_DIM:], zpad, w_kw], axis=1).astype(BF16)
    nt = n // LANES
    return pl.pallas_call(
        functools.partial(_inproj_kernel, tn=tn, seq=seq),
        grid=(n // tn,),
        in_specs=[
            pl.BlockSpec((tn, D_MODEL), lambda i: (i, 0)),
            pl.BlockSpec((_FM_END, D_MODEL), lambda i: (0, 0)),
            pl.BlockSpec((D_MODEL, _TM_WIDTH), lambda i: (0, 0)),
        ],
        out_specs=[
            pl.BlockSpec((NSA_WIDTH, tn), lambda i: (0, i)),
            pl.BlockSpec((2 * KV_HEADS, tn // LANES, HEAD_DIM, LANES), lambda i: (0, i, 0, 0)),
            pl.BlockSpec((GATE_ROWS, tn), lambda i: (0, i)),
            pl.BlockSpec((POOL_WIDTH, tn), lambda i: (0, i)),
            pl.BlockSpec((2, tn, KV_WIDTH), lambda i: (0, i, 0)),
            pl.BlockSpec((KV_HEADS, tn, LANES), lambda i: (0, i, 0)),
            pl.BlockSpec((tn, LANES), lambda i: (i, 0)),
        ],
        out_shape=[
            jax.ShapeDtypeStruct((NSA_WIDTH, n), BF16),
            jax.ShapeDtypeStruct((2 * KV_HEADS, nt, HEAD_DIM, LANES), BF16),
            jax.ShapeDtypeStruct((GATE_ROWS, n), F32),
            jax.ShapeDtypeStruct((POOL_WIDTH, n), F32),
            jax.ShapeDtypeStruct((2, n, KV_WIDTH), F32),
            jax.ShapeDtypeStruct((KV_HEADS, n, LANES), BF16),
            jax.ShapeDtypeStruct((n, LANES), BF16),
        ],
        compiler_params=pltpu.CompilerParams(vmem_limit_bytes=VMEM_LIMIT),
        name="inproj",
    )(x2, wfm, wtm)


def _compress_kernel(kc_ref, vc_ref, w2k_ref, w2v_ref, pek_ref, pev_ref, w1k_ref, w1v_ref,
                     wok0_ref, wok1_ref, wovt_ref, kcmp_ref, vcmpt_ref, *, n_rows):
    half = CMP_BLOCK // 2
    acc_k = jnp.zeros((n_rows, 4 * CMP_HIDDEN), F32)
    acc_v = jnp.zeros((n_rows, 4 * CMP_HIDDEN), F32)
    for l in range(half):
        rows = pl.ds(l, n_rows, stride=CMP_STRIDE)
        acc_k = acc_k + _dot(kc_ref[rows, :].astype(BF16), w2k_ref[l])
        acc_v = acc_v + _dot(vc_ref[rows, :].astype(BF16), w2v_ref[l])
    ck = _dot(pek_ref[...], w1k_ref[...])[0:1]
    cv = _dot(pev_ref[...], w1v_ref[...])[0:1]

    def hidden(acc, c, g):
        a = acc[:, g * CMP_HIDDEN:(g + 1) * CMP_HIDDEN]
        b = acc[:, (2 + g) * CMP_HIDDEN:(3 + g) * CMP_HIDDEN]
        return _gelu(a + pltpu.roll(b, n_rows - 1, axis=0) + c).astype(BF16)

    kcmp_ref[...] = (_dot(hidden(acc_k, ck, 0), wok0_ref[...])
                     + _dot(hidden(acc_k, ck, 1), wok1_ref[...])).astype(BF16)
    for g in range(KV_HEADS):
        vcmpt_ref[g] = _dot_nt(wovt_ref[...], hidden(acc_v, cv, g)).astype(BF16)


def _compress(kvc, batch, seq, pe_k, pe_v, k_w1, k_w2, v_w1, v_w2):
    n_rows = seq // CMP_STRIDE
    half = CMP_BLOCK // 2

    def blockdiag(w1):
        w = w1.reshape(CMP_BLOCK, HEAD_DIM, CMP_HIDDEN)
        z = jnp.zeros((half, HEAD_DIM, CMP_HIDDEN), w1.dtype)
        top = jnp.concatenate([w[:half], z, w[half:], z], axis=2)
        bot = jnp.concatenate([z, w[:half], z, w[half:]], axis=2)
        return jnp.concatenate([top, bot], axis=1).astype(BF16)

    def pe_rows(pe):
        flat = pe.reshape(1, CMP_BLOCK * HEAD_DIM)
        return jnp.concatenate([flat, jnp.zeros((7, flat.shape[1]), pe.dtype)], axis=0).astype(BF16)

    zo = jnp.zeros((CMP_HIDDEN, HEAD_DIM), k_w2.dtype)
    wok0 = jnp.concatenate([k_w2, zo], axis=1).astype(BF16)
    wok1 = jnp.concatenate([zo, k_w2], axis=1).astype(BF16)
    full = lambda *s: pl.BlockSpec(s, lambda b: (0,) * len(s))
    return pl.pallas_call(
        functools.partial(_compress_kernel, n_rows=n_rows),
        grid=(batch,),
        in_specs=[
            pl.BlockSpec((None, seq, KV_WIDTH), lambda b: (0, b, 0)),
            pl.BlockSpec((None, seq, KV_WIDTH), lambda b: (1, b, 0)),
            full(half, KV_WIDTH, 4 * CMP_HIDDEN), full(half, KV_WIDTH, 4 * CMP_HIDDEN),
            full(8, CMP_BLOCK * HEAD_DIM), full(8, CMP_BLOCK * HEAD_DIM),
            full(CMP_BLOCK * HEAD_DIM, CMP_HIDDEN), full(CMP_BLOCK * HEAD_DIM, CMP_HIDDEN),
            full(CMP_HIDDEN, KV_WIDTH), full(CMP_HIDDEN, KV_WIDTH), full(HEAD_DIM, CMP_HIDDEN),
        ],
        out_specs=[
            pl.BlockSpec((None, n_rows, KV_WIDTH), lambda b: (b, 0, 0)),
            pl.BlockSpec((None, KV_HEADS, HEAD_DIM, n_rows), lambda b: (b, 0, 0, 0)),
        ],
        out_shape=[
            jax.ShapeDtypeStruct((batch, n_rows, KV_WIDTH), BF16),
            jax.ShapeDtypeStruct((batch, KV_HEADS, HEAD_DIM, n_rows), BF16),
        ],
        compiler_params=pltpu.CompilerParams(vmem_limit_bytes=VMEM_LIMIT),
        name="compress",
    )(kvc, kvc, blockdiag(k_w1), blockdiag(v_w1), pe_rows(pe_k), pe_rows(pe_v),
      k_w1.astype(BF16), v_w1.astype(BF16), wok0, wok1, v_w2.T.astype(BF16))


SUPER_TILE = 2 * Q_BLOCK
WIN_KEYS = WINDOW + Q_BLOCK


def _softmax_init(s, vt):
    m = jnp.max(s, axis=0, keepdims=True)
    p = jnp.exp(s - m)
    return m, jnp.sum(p, axis=0, keepdims=True), _dot(vt, p.astype(BF16))


def _softmax_step(s, vt, state):
    m, l, acc = state
    m_new = jnp.maximum(m, jnp.max(s, axis=0, keepdims=True))
    alpha = jnp.exp(m - m_new)
    p = jnp.exp(s - m_new)
    return m_new, alpha * l + jnp.sum(p, axis=0, keepdims=True), alpha * acc + _dot(vt, p.astype(BF16))


def _softmax_merge(a, b):
    m = jnp.maximum(a[0], b[0])
    wa = jnp.exp(a[0] - m)
    wb = jnp.exp(b[0] - m)
    return (wa * a[2] + wb * b[2]) * (1.0 / (wa * a[1] + wb * b[1]))


def _nsa_kernel(qt_ref, kcmp_ref, vcmpt_ref, kslc_ref, kwin_ref, vslct_ref, vwint_ref, gt_ref,
                et_ref, xs_ref, xw_ref, ovt_ref, out_ref, *, n_cmp_rows, n_slc):
    g = pl.program_id(1)
    qb = pl.program_id(2)
    qw = GQA * Q_BLOCK
    q4 = qt_ref[...]
    qcat = jnp.concatenate([q4[r * HEAD_DIM:(r + 1) * HEAD_DIM] for r in range(GQA)], axis=1)
    zq = jnp.zeros_like(qcat)
    qg = jnp.concatenate([jnp.where(g == 0, qcat, zq), jnp.where(g == 1, qcat, zq)], axis=0)

    s = _dot(kcmp_ref[...], qg)
    k0 = jnp.maximum(qb * Q_BLOCK - WINDOW, 0)
    row_w = pl.multiple_of(WINDOW - qb * Q_BLOCK + k0, Q_BLOCK)
    s_win = _dot(kwin_ref[pl.ds(pl.multiple_of(k0, Q_BLOCK), WIN_KEYS), :], qg)

    off = pl.multiple_of(256 - 8 * qb, 8)
    s = s + jnp.concatenate([et_ref[r, pl.ds(off, n_cmp_rows), :] for r in range(GQA)], axis=1)
    m = jnp.max(s, axis=0, keepdims=True)
    p = jnp.exp(s - m)
    l = jnp.sum(p, axis=0, keepdims=True)
    p = p * jnp.where(m > 0.5 * NEG, 1.0 / l, 0.0)
    o_cmp = _dot(vcmpt_ref[...], p.astype(BF16))
    p_sum = p[:, 0:Q_BLOCK]
    for r in range(1, GQA):
        p_sum = p_sum + p[:, r * Q_BLOCK:(r + 1) * Q_BLOCK]
    imp = jnp.dot(ovt_ref[...], p_sum, preferred_element_type=F32,
                  precision=lax.Precision.HIGHEST)

    kt0 = k0 >> 7
    vt = jnp.concatenate([vwint_ref[kt0 + i] for i in range(WIN_KEYS // Q_BLOCK)], axis=1)
    m, l, acc = _softmax_init(s_win + xw_ref[pl.ds(row_w, WIN_KEYS), :], vt)
    o_win = acc * (1.0 / l)

    j_i = lax.broadcasted_iota(I32, (n_slc, Q_BLOCK), 0)
    t1 = qb * Q_BLOCK + lax.broadcasted_iota(I32, (n_slc, Q_BLOCK), 1)
    cur = t1 >> 6
    forced = (j_i == 0) | (j_i == cur) | (j_i == cur - 1)
    imp = jnp.where(forced, FORCE, imp)
    imp = jnp.where(j_i <= cur, imp, NEG)
    cnt = jnp.zeros((n_slc, Q_BLOCK), F32)
    for jp in range(n_slc):
        row = imp[jp:jp + 1, :]
        beats = (row > imp) | ((row == imp) & (j_i > jp))
        cnt = cnt + jnp.where(beats, 1.0, 0.0)
    selneg = jnp.where(cnt < float(SLC_COUNT), 0.0, NEG).astype(BF16)
    qsel = [qcat, jnp.concatenate([selneg] * GQA, axis=1)]
    if n_slc < LANES - HEAD_DIM:
        qsel.append(jnp.zeros((LANES - HEAD_DIM - n_slc, qw), BF16))
    qsel = jnp.concatenate(qsel, axis=0)

    def sel_scores(st):
        k = kslc_ref[pl.ds(pl.multiple_of(st * SUPER_TILE, SUPER_TILE), SUPER_TILE), :]
        vt = jnp.concatenate([vslct_ref[2 * st], vslct_ref[2 * st + 1]], axis=1)
        return _dot(k, qsel), vt

    odd = qb & 1
    st_diag = qb >> 1
    st_prev = jnp.maximum(st_diag - 1, 0)
    n_far = st_prev
    row_a = pl.multiple_of(384 - Q_BLOCK * odd, Q_BLOCK)
    row_b = pl.multiple_of(jnp.where(st_diag >= 1, Q_BLOCK - Q_BLOCK * odd, XS_MASKED_ROW), Q_BLOCK)
    s_a, vt_a = sel_scores(st_diag)
    s_b, vt_b = sel_scores(st_prev)
    chain_a = _softmax_init(s_a + xs_ref[pl.ds(row_a, SUPER_TILE), :], vt_a)
    chain_b = _softmax_init(s_b + xs_ref[pl.ds(row_b, SUPER_TILE), :], vt_b)

    def far_pair(i, chains):
        a, b = chains
        s_a, vt_a = sel_scores(2 * i)
        s_b, vt_b = sel_scores(2 * i + 1)
        return _softmax_step(s_a, vt_a, a), _softmax_step(s_b, vt_b, b)

    chain_a, chain_b = lax.fori_loop(0, n_far >> 1, far_pair, (chain_a, chain_b))
    chain_a = lax.cond((n_far & 1) == 1,
                       lambda a: _softmax_step(*sel_scores(n_far - 1), a), lambda a: a, chain_a)
    o_slc = _softmax_merge(chain_a, chain_b)

    outs = []
    for r in range(GQA):
        cols = slice(r * Q_BLOCK, (r + 1) * Q_BLOCK)
        gate = [gt_ref[pl.ds(c * NSA_HEADS + g * GQA + r, 1), :] for c in range(3)]
        outs.append(gate[0] * o_cmp[:, cols] + gate[1] * o_slc[:, cols] + gate[2] * o_win[:, cols])
    out_ref[...] = jnp.concatenate(outs, axis=0).astype(BF16)


def _nsa(qt, kcmp, vcmpt, kslc, kwin, vt, gt, et, xs, xw, batch, seq):
    assert seq % SUPER_TILE == 0 and seq >= WIN_KEYS and seq // CMP_STRIDE <= 256 and seq // SLC_BLOCK <= HEAD_DIM
    nq = seq // Q_BLOCK
    n_cmp_rows = seq // CMP_STRIDE
    n_slc = seq // SLC_BLOCK
    n = batch * seq
    ovt = jnp.asarray(_overlap_t(n_cmp_rows, n_slc))
    gw = GQA * HEAD_DIM
    return pl.pallas_call(
        functools.partial(_nsa_kernel, n_cmp_rows=n_cmp_rows, n_slc=n_slc),
        grid=(batch, KV_HEADS, nq),
        in_specs=[
            pl.BlockSpec((gw, Q_BLOCK), lambda b, g, q: (g, b * nq + q)),
            pl.BlockSpec((None, n_cmp_rows, KV_WIDTH), lambda b, g, q: (b, 0, 0)),
            pl.BlockSpec((None, None, HEAD_DIM, n_cmp_rows), lambda b, g, q: (b, g, 0, 0)),
            pl.BlockSpec((None, seq, LANES), lambda b, g, q: (g, b, 0)),
            pl.BlockSpec((seq, LANES), lambda b, g, q: (b, 0)),
            pl.BlockSpec((None, nq, HEAD_DIM, LANES), lambda b, g, q: (g, b, 0, 0)),
            pl.BlockSpec((None, nq, HEAD_DIM, LANES), lambda b, g, q: (KV_HEADS + g, b, 0, 0)),
            pl.BlockSpec((GATE_ROWS, Q_BLOCK), lambda b, g, q: (0, b * nq + q)),
            pl.BlockSpec((GQA, ET_ROWS, Q_BLOCK), lambda b, g, q: (g, 0, 0)),
            pl.BlockSpec((None, XS_ROWS, GQA * LANES), lambda b, g, q: (g, 0, 0)),
            pl.BlockSpec((None, XW_ROWS, GQA * LANES), lambda b, g, q: (g, 0, 0)),
            pl.BlockSpec((n_slc, n_cmp_rows), lambda b, g, q: (0, 0)),
        ],
        out_specs=pl.BlockSpec((gw, Q_BLOCK), lambda b, g, q: (g, b * nq + q)),
        out_shape=jax.ShapeDtypeStruct((NSA_WIDTH, n), BF16),
        compiler_params=pltpu.CompilerParams(vmem_limit_bytes=VMEM_LIMIT),
        name="nsa",
    )(qt, kcmp, vcmpt, kslc, kwin, vt, vt, gt, et, xs, xw, ovt)


def _layer_norm_fm(h, g_ref, b_ref, reps):
    mu = jnp.mean(h, axis=0, keepdims=True)
    hc = h - mu
    var = jnp.mean(hc * hc, axis=0, keepdims=True)
    gain = jnp.concatenate([g_ref[...]] * reps, axis=1)
    bias = jnp.concatenate([b_ref[...]] * reps, axis=1)
    return hc * lax.rsqrt(var + LN_EPS) * gain + bias


def _mix_kernel(nsat_ref, zt_ref, halo_ref, x_ref, won_ref, wop_ref, pwt_ref, ps_ref, g_ref, b_ref,
                h_ref, hb_ref, *, tn, seq):
    t0 = (pl.program_id(0) * tn) % seq
    reps = tn // LANES
    zc = jnp.concatenate([halo_ref[...], zt_ref[...]], axis=1)
    t_ext = t0 - LANES + lax.broadcasted_iota(I32, (1, LANES + tn), 1)
    zc = jnp.where(t_ext >= 0, zc, 0.0)
    pooled = []
    for gi, win in enumerate(POOL_WINDOWS):
        zg = zc[gi * POOL_GROUP_DIM:(gi + 1) * POOL_GROUP_DIM]
        s = zg
        sh = 1
        while sh < win:
            s = s + pltpu.roll(s, sh, axis=1)
            sh *= 2
        cnt = jnp.clip(t_ext + 1, 1, win).astype(F32)
        y = (s / cnt - zg)[:, LANES:]
        scale = jnp.concatenate([ps_ref[gi * POOL_GROUP_DIM:(gi + 1) * POOL_GROUP_DIM]] * reps, axis=1)
        pooled.append((_dot(pwt_ref[gi], y.astype(BF16)) * scale).astype(BF16))
    pool_t = jnp.concatenate(pooled, axis=0)
    mix = _dot(won_ref[...], nsat_ref[...]) + _dot(wop_ref[...], pool_t)
    h = ALPHA * x_ref[...].T + mix
    h = _layer_norm_fm(h, g_ref, b_ref, reps)
    h_ref[...] = h
    hb_ref[...] = h.astype(BF16)


def _lane_bcast(v):
    return jnp.broadcast_to(v.reshape(-1, 1), (v.shape[0], LANES)).astype(F32)


def _mix(nsat, zt, x2, w_out, pool_w, pool_scale, ln_g, ln_b, seq):
    n = x2.shape[0]
    tn = 256
    wot = w_out.T.astype(BF16)
    full = lambda *s: pl.BlockSpec(s, lambda i: (0,) * len(s))
    return pl.pallas_call(
        functools.partial(_mix_kernel, tn=tn, seq=seq),
        grid=(n // tn,),
        in_specs=[
            pl.BlockSpec((NSA_WIDTH, tn), lambda i: (0, i)),
            pl.BlockSpec((POOL_WIDTH, tn), lambda i: (0, i)),
            pl.BlockSpec((POOL_WIDTH, LANES), lambda i: (0, jnp.maximum(i * (tn // LANES) - 1, 0))),
            pl.BlockSpec((tn, D_MODEL), lambda i: (i, 0)),
            full(D_MODEL, NSA_WIDTH), full(D_MODEL, POOL_WIDTH),
            full(len(POOL_WINDOWS), POOL_GROUP_DIM, POOL_GROUP_DIM),
            full(POOL_WIDTH, LANES), full(D_MODEL, LANES), full(D_MODEL, LANES),
        ],
        out_specs=[pl.BlockSpec((D_MODEL, tn), lambda i: (0, i))] * 2,
        out_shape=[jax.ShapeDtypeStruct((D_MODEL, n), F32), jax.ShapeDtypeStruct((D_MODEL, n), BF16)],
        compiler_params=pltpu.CompilerParams(vmem_limit_bytes=VMEM_LIMIT),
        name="mix",
    )(nsat, zt, zt, x2, wot[:, :NSA_WIDTH], wot[:, NSA_WIDTH:],
      jnp.swapaxes(pool_w, 1, 2).astype(BF16), _lane_bcast(pool_scale), _lane_bcast(ln_g), _lane_bcast(ln_b))


ROUTE_HEAD_GROUP = 4
_REMOVED = -(2.0 ** 100)
_PAD_SCORE = -3e38


def _top16_by_index(s):
    rows = lax.broadcasted_iota(I32, s.shape, 0).astype(F32)
    row16 = lax.broadcasted_iota(I32, (PEER_TOPK, s.shape[1]), 0)
    cur = s
    rank = jnp.full(s.shape, float(PEER_TOPK), F32)
    vals = jnp.zeros((PEER_TOPK, s.shape[1]), F32)
    for r in range(PEER_TOPK):
        mx = jnp.max(cur, axis=0, keepdims=True)
        first = jnp.min(jnp.where(cur == mx, rows, float(s.shape[0])), axis=0, keepdims=True)
        hit = rows == first
        vals = jnp.where(row16 == r, mx, vals)
        rank = jnp.where(hit, float(r), rank)
        cur = jnp.where(hit, -jnp.inf, cur)
    return vals, rank, jnp.full((1, s.shape[1]), float(PEER_TOPK), F32)


def _top16_by_value(scores):
    width = scores[0].shape[1]
    row16 = lax.broadcasted_iota(I32, (PEER_TOPK, width), 0)
    cur = list(scores)
    vals = [jnp.zeros((PEER_TOPK, width), F32) for _ in scores]
    for r in range(PEER_TOPK):
        mx = [jnp.max(c, axis=0, keepdims=True) for c in cur]
        vals = [jnp.where(row16 == r, m, v) for m, v in zip(mx, vals)]
        cur = [jnp.where(c == m, _REMOVED * (r + 1), c) for c, m in zip(cur, mx)]
    out = []
    for c, v in zip(cur, vals):
        member = c <= _REMOVED
        rank = jnp.where(member, c * (1.0 / _REMOVED) - 1.0, float(PEER_TOPK))
        out.append((v, rank, jnp.sum(jnp.where(member, 1.0, 0.0), axis=0, keepdims=True)))
    return out


def _candidate_table(v1, v2):
    tn = v1.shape[1]
    row8 = lax.broadcasted_iota(I32, (SUBLANES, tn), 0)
    groups = [v1[0:1, :] + v2]
    real = [PEER_TOPK]
    for a in range(1, SUBLANES):
        nb = PEER_TOPK // (a + 1)
        groups.append(jnp.where(row8 < nb, v1[a:a + 1, :] + v2[0:SUBLANES, :], _PAD_SCORE))
        real.append(nb)
    groups.append(v1[SUBLANES:, :] + v2[0:1, :])
    real.append(PEER_TOPK - SUBLANES)
    return jnp.concatenate(groups, axis=0), real


def _peer_scores(h, hb_ref, wqt_ref, sk1_ref, sk2_ref):
    half = PEER_QDIM // 2
    q = _dot(wqt_ref[h * PEER_QDIM:(h + 1) * PEER_QDIM, :], hb_ref[...])
    return _dot(sk1_ref[...], q[:half].astype(BF16)), _dot(sk2_ref[...], q[half:].astype(BF16))


def _route_heads(heads, scores, outs, *, tn, by_value):
    n_ref, a_ref, r2_ref, b_ref = outs
    want = float(PEER_TOPK)
    miscounts = []
    per_head = []
    for h, (s1, s2) in zip(heads, scores):
        if by_value:
            (v1, rank1, count1), (v2, rank2, count2) = _top16_by_value([s1, s2])
        else:
            (v1, rank1, count1), (v2, rank2, count2) = _top16_by_index(s1), _top16_by_index(s2)
        miscounts.append(jnp.maximum(jnp.abs(count1 - want), jnp.abs(count2 - want)))
        cs, real = _candidate_table(v1, v2)
        per_head.append((h, s1, s2, v1, v2, rank1, rank2, cs))

    if by_value:
        cur = [ph[-1] for ph in per_head]
        for _ in range(PEER_TOPK):
            cur = [jnp.where(c == jnp.max(c, axis=0, keepdims=True), -jnp.inf, c) for c in cur]
        sels = [c == -jnp.inf for c in cur]
    else:
        sels = []
        for ph in per_head:
            cs = ph[-1]
            c_i = lax.broadcasted_iota(I32, cs.shape, 0)
            cnt = jnp.zeros(cs.shape, F32)
            base = 0
            for g, n_real in enumerate(real):
                for c in range(base, base + n_real):
                    row = cs[c:c + 1, :]
                    beats = (row > cs) | ((row == cs) & (c_i > c))
                    cnt = cnt + jnp.where(beats, 1.0, 0.0)
                base += PEER_TOPK if g == 0 else SUBLANES
            sels.append(cnt < want)

    for i, ((h, s1, s2, v1, v2, rank1, rank2, cs), sel) in enumerate(zip(per_head, sels)):
        self_f = jnp.where(sel, 1.0, 0.0)
        miscounts[i] = jnp.maximum(miscounts[i], jnp.abs(jnp.sum(self_f, axis=0, keepdims=True) - want))
        top = v1[0:1, :] + v2[0:1, :]
        z = jnp.sum(jnp.where(sel, jnp.exp(cs - top), 0.0), axis=0, keepdims=True)
        nmap = jnp.zeros((PEER_KEYS, tn), F32)
        base = 0
        for a in range(PEER_TOPK):
            if a < SUBLANES:
                size = PEER_TOPK if a == 0 else SUBLANES
                n_a = jnp.sum(self_f[base:base + size, :], axis=0, keepdims=True)
                base += size
            else:
                n_a = self_f[base + a - SUBLANES:base + a - SUBLANES + 1, :]
            nmap = jnp.where(rank1 == float(a), n_a, nmap)
        n_ref[h] = nmap
        a_ref[h] = jnp.where(rank1 < want, jnp.exp(s1 - v1[0:1, :]), 0.0)
        r2_ref[h] = rank2.astype(BF16)
        b_ref[h] = (jnp.where(rank2 < want, jnp.exp(s2 - v2[0:1, :]), 0.0) / z).astype(BF16)
    return miscounts


def _route_kernel(hb_ref, wqt_ref, sk1_ref, sk2_ref, n_ref, a_ref, r2_ref, b_ref, *, tn):
    scores = functools.partial(_peer_scores, hb_ref=hb_ref, wqt_ref=wqt_ref, sk1_ref=sk1_ref, sk2_ref=sk2_ref)
    outs = (n_ref, a_ref, r2_ref, b_ref)
    groups = [range(h0, h0 + ROUTE_HEAD_GROUP) for h0 in range(0, PEER_HEADS, ROUTE_HEAD_GROUP)]
    miscounts = []
    ready = [scores(h) for h in groups[0]]
    for gi, heads in enumerate(groups):
        ahead = [scores(h) for h in groups[gi + 1]] if gi + 1 < len(groups) else None
        miscounts += _route_heads(heads, ready, outs, tn=tn, by_value=True)
        ready = ahead

    any_miscount = functools.reduce(jnp.maximum, miscounts)

    @pl.when(jnp.max(any_miscount) > 0.0)
    def _():
        for h in range(PEER_HEADS):
            @pl.when(jnp.max(miscounts[h]) > 0.0)
            def _(h=h):
                _route_heads([h], [scores(h)], outs, tn=tn, by_value=False)


def _route(hb, wq, subkeys):
    n = hb.shape[1]
    tn = LANES
    full = lambda *s: pl.BlockSpec(s, lambda i: (0,) * len(s))
    tab = lambda: pl.BlockSpec((PEER_HEADS, PEER_KEYS, tn), lambda i: (0, 0, i))
    shape = (PEER_HEADS, PEER_KEYS, n)
    return pl.pallas_call(
        functools.partial(_route_kernel, tn=tn),
        grid=(n // tn,),
        in_specs=[
            pl.BlockSpec((D_MODEL, tn), lambda i: (0, i)),
            full(PEER_HEADS * PEER_QDIM, D_MODEL), full(PEER_KEYS, PEER_QDIM // 2), full(PEER_KEYS, PEER_QDIM // 2),
        ],
        out_specs=[tab(), tab(), tab(), tab()],
        out_shape=[
            jax.ShapeDtypeStruct(shape, F32),
            jax.ShapeDtypeStruct(shape, F32),
            jax.ShapeDtypeStruct(shape, BF16),
            jax.ShapeDtypeStruct(shape, BF16),
        ],
        compiler_params=pltpu.CompilerParams(vmem_limit_bytes=VMEM_LIMIT),
        name="peer_route",
    )(hb, wq.T.astype(BF16), subkeys[0].astype(BF16), subkeys[1].astype(BF16))


def _experts_kernel(hb_ref, u_ref, vt_ref, n_ref, a_ref, r2_ref, b_ref, h_ref, g_ref, bb_ref,
                    out_ref, acc_ref, *, tn, te):
    j = pl.program_id(1)

    @pl.when(j == 0)
    def _():
        acc_ref[...] = jnp.zeros_like(acc_ref)

    hb = hb_ref[...]
    part = te // EXPERT_PARTS
    acts = [_dot(u_ref[i * part:(i + 1) * part, :], hb) for i in range(EXPERT_PARTS)]
    chunks = part // PEER_KEYS
    zero = jnp.zeros((PEER_KEYS, tn), BF16)

    def key_row(ref, h, i1):
        row = jnp.broadcast_to(ref[h, pl.ds(i1, 1), :], (2 * SUBLANES, tn)).astype(BF16)
        return jnp.concatenate([row] * (PEER_KEYS // (2 * SUBLANES)), axis=0)

    for i in range(EXPERT_PARTS):
        act = _gelu(acts[i])
        ps = []
        for c in range(chunks):
            i1 = (EXPERT_PARTS * j + i) * chunks + c
            w = zero
            for h in range(PEER_HEADS):
                w = w + key_row(a_ref, h, i1) * jnp.where(r2_ref[h] < key_row(n_ref, h, i1), b_ref[h], zero)
            ps.append(w * act[c * PEER_KEYS:(c + 1) * PEER_KEYS].astype(BF16))
        acc_ref[...] += _dot(vt_ref[:, i * part:(i + 1) * part], jnp.concatenate(ps, axis=0))

    @pl.when(j == pl.num_programs(1) - 1)
    def _():
        h = ALPHA * h_ref[...] + acc_ref[...]
        out_ref[...] = _layer_norm_fm(h, g_ref, bb_ref, tn // LANES).T


def _experts(hb, hf, u_b, vt_b, tabs, ln_g, ln_b):
    n = hb.shape[1]
    tn, te = 512, 1024
    tab = lambda: pl.BlockSpec((PEER_HEADS, PEER_KEYS, tn), lambda i, j: (0, 0, i))
    return pl.pallas_call(
        functools.partial(_experts_kernel, tn=tn, te=te),
        grid=(n // tn, PEER_EXPERTS // te),
        in_specs=[
            pl.BlockSpec((D_MODEL, tn), lambda i, j: (0, i)),
            pl.BlockSpec((te, D_MODEL), lambda i, j: (j, 0)),
            pl.BlockSpec((D_MODEL, te), lambda i, j: (0, j)),
            tab(), tab(), tab(), tab(),
            pl.BlockSpec((D_MODEL, tn), lambda i, j: (0, i)),
            pl.BlockSpec((D_MODEL, LANES), lambda i, j: (0, 0)),
            pl.BlockSpec((D_MODEL, LANES), lambda i, j: (0, 0)),
        ],
        out_specs=pl.BlockSpec((tn, D_MODEL), lambda i, j: (i, 0)),
        out_shape=jax.ShapeDtypeStruct((n, D_MODEL), F32),
        scratch_shapes=[pltpu.VMEM((D_MODEL, tn), F32)],
        compiler_params=pltpu.CompilerParams(
            dimension_semantics=("arbitrary", "arbitrary"), vmem_limit_bytes=VMEM_LIMIT),
        name="peer_experts",
    )(hb, u_b, vt_b, *tabs, hf, _lane_bcast(ln_g), _lane_bcast(ln_b))


def kernel(x, w_in, rel_table, cmp_pe_k, cmp_pe_v, cmp_k_w1, cmp_k_w2, cmp_v_w1, cmp_v_w2, pool_w, pool_scale,
           w_out, ln1_g, ln1_b, peer_wq, peer_subkeys, peer_u, peer_v, ln2_g, ln2_b):
    batch, seq, _ = x.shape
    assert w_in.shape[0] == DEPTH == 1 and seq % 512 == 0 and seq // CMP_STRIDE <= 256
    x2 = x.reshape(batch * seq, D_MODEL)
    et, xs, xw = _bias_tables(rel_table)
    qt, vt, gt, zt, kvc, kslc, kwin = _inproj(x2, w_in[0], seq)
    kcmp, vcmpt = _compress(kvc, batch, seq, cmp_pe_k[0], cmp_pe_v[0], cmp_k_w1[0], cmp_k_w2[0],
                            cmp_v_w1[0], cmp_v_w2[0])
    nsat = _nsa(qt, kcmp, vcmpt, kslc, kwin, vt, gt, et, xs, xw, batch, seq)
    hf, hb = _mix(nsat, zt, x2, w_out[0], pool_w[0], pool_scale[0], ln1_g[0], ln1_b[0], seq)
    tabs = _route(hb, peer_wq[0], peer_subkeys[0])
    y = _experts(hb, hf, peer_u[0].astype(BF16), peer_v[0].T.astype(BF16), tabs, ln2_g[0], ln2_b[0])
    return y.reshape(batch, seq, D_MODEL)
```

```python
import functools
import math

import numpy as np
import jax
import jax.numpy as jnp
from jax import lax
from jax.experimental import pallas as pl
from jax.experimental.pallas import tpu as pltpu

F32 = jnp.float32
BF16 = jnp.bfloat16
I32 = jnp.int32

D_MODEL = 1024
NSA_HEADS = 8
KV_HEADS = 2
GQA = NSA_HEADS // KV_HEADS
HEAD_DIM = 64
NSA_WIDTH = NSA_HEADS * HEAD_DIM
KV_WIDTH = KV_HEADS * HEAD_DIM
POOL_WIDTH = D_MODEL - NSA_WIDTH
POOL_WINDOWS = (2, 4, 8, 16)
POOL_GROUP_DIM = POOL_WIDTH // len(POOL_WINDOWS)
N_GATES = 3 * NSA_HEADS
GATE_ROWS = 32
CMP_BLOCK = 32
CMP_STRIDE = 16
CMP_HIDDEN = 256
SLC_BLOCK = 64
SLC_COUNT = 16
WINDOW = 512
Q_BLOCK = 128
NEG = -1e30
FORCE = 1e6
N_BUCKETS = 32
MAX_DISTANCE = 128
PEER_HEADS = 8
PEER_KEYS = 128
PEER_EXPERTS = PEER_KEYS * PEER_KEYS
PEER_QDIM = 256
PEER_TOPK = 16
DEPTH = 1
ALPHA = (2 * DEPTH) ** 0.25
LN_EPS = 1e-5
SQRT_HALF = 0.7071067811865476

LANES = 128
SUBLANES = 8
BF16_ROWS = SUBLANES
EXPERT_PARTS = 2
VMEM_LIMIT = 52 * 1024 * 1024

_FM_Q, _FM_V, _FM_G, _FM_Z, _FM_END = 0, 512, 768, 800, 1312
_TM_WIDTH = 640


def _gelu(x):
    return 0.5 * x * (1.0 + lax.erf(x * SQRT_HALF))


def _dot(a, b):
    return jnp.dot(a, b, preferred_element_type=F32)


def _dot_nt(a, b):
    return lax.dot_general(a, b, (((1,), (1,)), ((), ())), preferred_element_type=F32)


def _bucket_np(dist):
    dist = np.maximum(dist, 0)
    max_exact = N_BUCKETS // 2
    large = max_exact + np.floor(
        np.log(np.maximum(dist, 1) / max_exact) / math.log(MAX_DISTANCE / max_exact)
        * (N_BUCKETS - max_exact)).astype(np.int64)
    large = np.minimum(large, N_BUCKETS - 1)
    return np.where(dist < max_exact, dist, large)


ET_ROWS = 512
XS_ROWS = 896
XS_MASKED_ROW = 640
XW_ROWS = 1152
_CODE_MASKED = N_BUCKETS


def _bias_index_tables():
    qi = np.arange(Q_BLOCK)[None, :]

    def codes(dist, visible):
        return np.where(visible, _bucket_np(dist), _CODE_MASKED).astype(np.int32)

    d_cmp = qi - CMP_STRIDE * (np.arange(ET_ROWS)[:, None] - 256) - (CMP_BLOCK - 1)
    rho = np.arange(XS_ROWS)[:, None]
    d_sel = qi - (rho - 384)
    d_win = qi - (np.arange(XW_ROWS)[:, None] - 512)
    return np.concatenate([
        codes(d_cmp, d_cmp >= 0),
        codes(d_sel, (d_sel >= 0) & (rho < XS_MASKED_ROW)),
        codes(d_win, (d_win >= 0) & (d_win < WINDOW)),
    ], axis=0)


def _overlap_t(n_cmp_rows, n_slc):
    n = np.arange(n_cmp_rows)[None, :]
    j = np.arange(n_slc)[:, None]
    start = n * CMP_STRIDE
    end = start + CMP_BLOCK - 1
    return ((start <= j * SLC_BLOCK + SLC_BLOCK - 1) & (end >= j * SLC_BLOCK)).astype(np.float32)


def _bias_kernel(tab_ref, idx_ref, et_ref, xs_ref, xw_ref):
    g = pl.program_id(0)
    chunk = Q_BLOCK
    for r in range(GQA):
        h = g * GQA + r
        far = tab_ref[N_BUCKETS - 1, h]
        cols = slice(r * LANES, (r + 1) * LANES)
        for c in range((ET_ROWS + XS_ROWS + XW_ROWS) // chunk):
            idx = idx_ref[c * chunk:(c + 1) * chunk, :]
            acc = jnp.where(idx == _CODE_MASKED, NEG, 0.0)
            for b in range(N_BUCKETS - 1):
                acc = jnp.where(idx == b, tab_ref[b, h] - far, acc)
            row = c * chunk
            if row < ET_ROWS:
                et_ref[r, row:row + chunk, :] = acc
            elif row < ET_ROWS + XS_ROWS:
                xs_ref[0, row - ET_ROWS:row - ET_ROWS + chunk, cols] = acc
            else:
                row -= ET_ROWS + XS_ROWS
                xw_ref[0, row:row + chunk, cols] = acc


def _bias_tables(rel_table):
    idx = _bias_index_tables()
    qw = GQA * LANES
    return pl.pallas_call(
        _bias_kernel,
        grid=(KV_HEADS,),
        in_specs=[
            pl.BlockSpec(memory_space=pltpu.SMEM),
            pl.BlockSpec(idx.shape, lambda g: (0, 0)),
        ],
        out_specs=[
            pl.BlockSpec((GQA, ET_ROWS, Q_BLOCK), lambda g: (g, 0, 0)),
            pl.BlockSpec((1, XS_ROWS, qw), lambda g: (g, 0, 0)),
            pl.BlockSpec((1, XW_ROWS, qw), lambda g: (g, 0, 0)),
        ],
        out_shape=[
            jax.ShapeDtypeStruct((NSA_HEADS, ET_ROWS, Q_BLOCK), F32),
            jax.ShapeDtypeStruct((KV_HEADS, XS_ROWS, qw), F32),
            jax.ShapeDtypeStruct((KV_HEADS, XW_ROWS, qw), F32),
        ],
        name="bias_tables",
    )(rel_table, jnp.asarray(idx))


def _inproj_kernel(x_ref, wfm_ref, wtm_ref, qt_ref, vt_ref, gt_ref, zt_ref, kvc_ref,
                   kslc_ref, kwin_ref, *, tn, seq):
    xb = x_ref[...].astype(BF16)
    fm = _dot_nt(wfm_ref[...], xb)
    tm = _dot(xb, wtm_ref[...])
    qt_ref[...] = (fm[_FM_Q:_FM_V] * (HEAD_DIM ** -0.5)).astype(BF16)
    for c in range(2 * KV_HEADS):
        for s in range(tn // LANES):
            vt_ref[c, s] = fm[_FM_V + c * HEAD_DIM:_FM_V + (c + 1) * HEAD_DIM,
                              s * LANES:(s + 1) * LANES].astype(BF16)
    gt_ref[...] = jax.nn.sigmoid(fm[_FM_G:_FM_Z])
    zt_ref[...] = fm[_FM_Z:_FM_END]
    kvc_ref[0] = tm[:, 0:KV_WIDTH]
    kvc_ref[1] = tm[:, KV_WIDTH:2 * KV_WIDTH]
    t0 = (pl.program_id(0) * tn) % seq
    row = lax.broadcasted_iota(I32, (tn, LANES), 0) + t0
    lane = lax.broadcasted_iota(I32, (tn, LANES), 1)
    onehot = jnp.where(lane - HEAD_DIM == (row >> 6), 1.0, 0.0)
    for g in range(KV_HEADS):
        k = tm[:, 256 + g * LANES:256 + (g + 1) * LANES]
        kslc_ref[g] = jnp.where(lane < HEAD_DIM, k, onehot).astype(BF16)
    kwin_ref[...] = tm[:, 512:640].astype(BF16)


def _inproj(x2, w_in, seq):
    n = x2.shape[0]
    tn = 512
    w = w_in
    o = NSA_WIDTH
    w_q = w[:, 0:o]
    w_kc, w_vc, w_ks, w_vs, w_kw, w_vw = [w[:, o + i * KV_WIDTH:o + (i + 1) * KV_WIDTH] for i in range(6)]
    o += 6 * KV_WIDTH
    w_g = w[:, o:o + N_GATES]
    w_z = w[:, o + N_GATES:]
    zpad = jnp.zeros((D_MODEL, HEAD_DIM), w.dtype)
    wfm = jnp.concatenate(
        [w_q, w_vs, w_vw, w_g, jnp.zeros((D_MODEL, GATE_ROWS - N_GATES), w.dtype), w_z], axis=1).T.astype(BF16)
    wtm = jnp.concatenate(
        [w_kc, w_vc, w_ks[:, :HEAD_DIM], zpad, w_ks[:, HEAD_DIM:], zpad, w_kw], axis=1).astype(BF16)
    nt = n // LANES
    return pl.pallas_call(
        functools.partial(_inproj_kernel, tn=tn, seq=seq),
        grid=(n // tn,),
        in_specs=[
            pl.BlockSpec((tn, D_MODEL), lambda i: (i, 0)),
            pl.BlockSpec((_FM_END, D_MODEL), lambda i: (0, 0)),
            pl.BlockSpec((D_MODEL, _TM_WIDTH), lambda i: (0, 0)),
        ],
        out_specs=[
            pl.BlockSpec((NSA_WIDTH, tn), lambda i: (0, i)),
            pl.BlockSpec((2 * KV_HEADS, tn // LANES, HEAD_DIM, LANES), lambda i: (0, i, 0, 0)),
            pl.BlockSpec((GATE_ROWS, tn), lambda i: (0, i)),
            pl.BlockSpec((POOL_WIDTH, tn), lambda i: (0, i)),
            pl.BlockSpec((2, tn, KV_WIDTH), lambda i: (0, i, 0)),
            pl.BlockSpec((KV_HEADS, tn, LANES), lambda i: (0, i, 0)),
            pl.BlockSpec((tn, LANES), lambda i: (i, 0)),
        ],
        out_shape=[
            jax.ShapeDtypeStruct((NSA_WIDTH, n), BF16),
            jax.ShapeDtypeStruct((2 * KV_HEADS, nt, HEAD_DIM, LANES), BF16),
            jax.ShapeDtypeStruct((GATE_ROWS, n), F32),
            jax.ShapeDtypeStruct((POOL_WIDTH, n), F32),
            jax.ShapeDtypeStruct((2, n, KV_WIDTH), F32),
            jax.ShapeDtypeStruct((KV_HEADS, n, LANES), BF16),
            jax.ShapeDtypeStruct((n, LANES), BF16),
        ],
        compiler_params=pltpu.CompilerParams(vmem_limit_bytes=VMEM_LIMIT),
        name="inproj",
    )(x2, wfm, wtm)


def _compress_kernel(kc_ref, vc_ref, w2k_ref, w2v_ref, pek_ref, pev_ref, w1k_ref, w1v_ref,
                     wok0_ref, wok1_ref, wovt_ref, kcmp_ref, vcmpt_ref, *, n_rows):
    half = CMP_BLOCK // 2
    acc_k = jnp.zeros((n_rows, 4 * CMP_HIDDEN), F32)
    acc_v = jnp.zeros((n_rows, 4 * CMP_HIDDEN), F32)
    for l in range(half):
        rows = pl.ds(l, n_rows, stride=CMP_STRIDE)
        acc_k = acc_k + _dot(kc_ref[rows, :].astype(BF16), w2k_ref[l])
        acc_v = acc_v + _dot(vc_ref[rows, :].astype(BF16), w2v_ref[l])
    ck = _dot(pek_ref[...], w1k_ref[...])[0:1]
    cv = _dot(pev_ref[...], w1v_ref[...])[0:1]

    def hidden(acc, c, g):
        a = acc[:, g * CMP_HIDDEN:(g + 1) * CMP_HIDDEN]
        b = acc[:, (2 + g) * CMP_HIDDEN:(3 + g) * CMP_HIDDEN]
        return _gelu(a + pltpu.roll(b, n_rows - 1, axis=0) + c).astype(BF16)

    kcmp_ref[...] = (_dot(hidden(acc_k, ck, 0), wok0_ref[...])
                     + _dot(hidden(acc_k, ck, 1), wok1_ref[...])).astype(BF16)
    for g in range(KV_HEADS):
        vcmpt_ref[g] = _dot_nt(wovt_ref[...], hidden(acc_v, cv, g)).astype(BF16)


def _compress(kvc, batch, seq, pe_k, pe_v, k_w1, k_w2, v_w1, v_w2):
    n_rows = seq // CMP_STRIDE
    half = CMP_BLOCK // 2

    def blockdiag(w1):
        w = w1.reshape(CMP_BLOCK, HEAD_DIM, CMP_HIDDEN)
        z = jnp.zeros((half, HEAD_DIM, CMP_HIDDEN), w1.dtype)
        top = jnp.concatenate([w[:half], z, w[half:], z], axis=2)
        bot = jnp.concatenate([z, w[:half], z, w[half:]], axis=2)
        return jnp.concatenate([top, bot], axis=1).astype(BF16)

    def pe_rows(pe):
        flat = pe.reshape(1, CMP_BLOCK * HEAD_DIM)
        return jnp.concatenate([flat, jnp.zeros((7, flat.shape[1]), pe.dtype)], axis=0).astype(BF16)

    zo = jnp.zeros((CMP_HIDDEN, HEAD_DIM), k_w2.dtype)
    wok0 = jnp.concatenate([k_w2, zo], axis=1).astype(BF16)
    wok1 = jnp.concatenate([zo, k_w2], axis=1).astype(BF16)
    full = lambda *s: pl.BlockSpec(s, lambda b: (0,) * len(s))
    return pl.pallas_call(
        functools.partial(_compress_kernel, n_rows=n_rows),
        grid=(batch,),
        in_specs=[
            pl.BlockSpec((None, seq, KV_WIDTH), lambda b: (0, b, 0)),
            pl.BlockSpec((None, seq, KV_WIDTH), lambda b: (1, b, 0)),
            full(half, KV_WIDTH, 4 * CMP_HIDDEN), full(half, KV_WIDTH, 4 * CMP_HIDDEN),
            full(8, CMP_BLOCK * HEAD_DIM), full(8, CMP_BLOCK * HEAD_DIM),
            full(CMP_BLOCK * HEAD_DIM, CMP_HIDDEN), full(CMP_BLOCK * HEAD_DIM, CMP_HIDDEN),
            full(CMP_HIDDEN, KV_WIDTH), full(CMP_HIDDEN, KV_WIDTH), full(HEAD_DIM, CMP_HIDDEN),
        ],
        out_specs=[
            pl.BlockSpec((None, n_rows, KV_WIDTH), lambda b: (b, 0, 0)),
            pl.BlockSpec((None, KV_HEADS, HEAD_DIM, n_rows), lambda b: (b, 0, 0, 0)),
        ],
        out_shape=[
            jax.ShapeDtypeStruct((batch, n_rows, KV_WIDTH), BF16),
            jax.ShapeDtypeStruct((batch, KV_HEADS, HEAD_DIM, n_rows), BF16),
        ],
        compiler_params=pltpu.CompilerParams(vmem_limit_bytes=VMEM_LIMIT),
        name="compress",
    )(kvc, kvc, blockdiag(k_w1), blockdiag(v_w1), pe_rows(pe_k), pe_rows(pe_v),
      k_w1.astype(BF16), v_w1.astype(BF16), wok0, wok1, v_w2.T.astype(BF16))


SUPER_TILE = 2 * Q_BLOCK
WIN_KEYS = WINDOW + Q_BLOCK


def _softmax_init(s, vt):
    m = jnp.max(s, axis=0, keepdims=True)
    p = jnp.exp(s - m)
    return m, jnp.sum(p, axis=0, keepdims=True), _dot(vt, p.astype(BF16))


def _softmax_step(s, vt, state):
    m, l, acc = state
    m_new = jnp.maximum(m, jnp.max(s, axis=0, keepdims=True))
    alpha = jnp.exp(m - m_new)
    p = jnp.exp(s - m_new)
    return m_new, alpha * l + jnp.sum(p, axis=0, keepdims=True), alpha * acc + _dot(vt, p.astype(BF16))


def _softmax_merge(a, b):
    m = jnp.maximum(a[0], b[0])
    wa = jnp.exp(a[0] - m)
    wb = jnp.exp(b[0] - m)
    return (wa * a[2] + wb * b[2]) * (1.0 / (wa * a[1] + wb * b[1]))


def _nsa_kernel(qt_ref, kcmp_ref, vcmpt_ref, kslc_ref, kwin_ref, vslct_ref, vwint_ref, gt_ref,
                et_ref, xs_ref, xw_ref, ovt_ref, out_ref, *, n_cmp_rows, n_slc):
    g = pl.program_id(1)
    qb = pl.program_id(2)
    qw = GQA * Q_BLOCK
    q4 = qt_ref[...]
    qcat = jnp.concatenate([q4[r * HEAD_DIM:(r + 1) * HEAD_DIM] for r in range(GQA)], axis=1)
    zq = jnp.zeros_like(qcat)
    qg = jnp.concatenate([jnp.where(g == 0, qcat, zq), jnp.where(g == 1, qcat, zq)], axis=0)

    s = _dot(kcmp_ref[...], qg)
    k0 = jnp.maximum(qb * Q_BLOCK - WINDOW, 0)
    row_w = pl.multiple_of(WINDOW - qb * Q_BLOCK + k0, Q_BLOCK)
    s_win = _dot(kwin_ref[pl.ds(pl.multiple_of(k0, Q_BLOCK), WIN_KEYS), :], qg)

    off = pl.multiple_of(256 - 8 * qb, 8)
    s = s + jnp.concatenate([et_ref[r, pl.ds(off, n_cmp_rows), :] for r in range(GQA)], axis=1)
    m = jnp.max(s, axis=0, keepdims=True)
    p = jnp.exp(s - m)
    l = jnp.sum(p, axis=0, keepdims=True)
    p = p * jnp.where(m > 0.5 * NEG, 1.0 / l, 0.0)
    o_cmp = _dot(vcmpt_ref[...], p.astype(BF16))
    p_sum = p[:, 0:Q_BLOCK]
    for r in range(1, GQA):
        p_sum = p_sum + p[:, r * Q_BLOCK:(r + 1) * Q_BLOCK]
    imp = jnp.dot(ovt_ref[...], p_sum, preferred_element_type=F32,
                  precision=lax.Precision.HIGHEST)

    kt0 = k0 >> 7
    vt = jnp.concatenate([vwint_ref[kt0 + i] for i in range(WIN_KEYS // Q_BLOCK)], axis=1)
    m, l, acc = _softmax_init(s_win + xw_ref[pl.ds(row_w, WIN_KEYS), :], vt)
    o_win = acc * (1.0 / l)

    j_i = lax.broadcasted_iota(I32, (n_slc, Q_BLOCK), 0)
    t1 = qb * Q_BLOCK + lax.broadcasted_iota(I32, (n_slc, Q_BLOCK), 1)
    cur = t1 >> 6
    forced = (j_i == 0) | (j_i == cur) | (j_i == cur - 1)
    imp = jnp.where(forced, FORCE, imp)
    imp = jnp.where(j_i <= cur, imp, NEG)
    cnt = jnp.zeros((n_slc, Q_BLOCK), F32)
    for jp in range(n_slc):
        row = imp[jp:jp + 1, :]
        beats = (row > imp) | ((row == imp) & (j_i > jp))
        cnt = cnt + jnp.where(beats, 1.0, 0.0)
    selneg = jnp.where(cnt < float(SLC_COUNT), 0.0, NEG).astype(BF16)
    qsel = [qcat, jnp.concatenate([selneg] * GQA, axis=1)]
    if n_slc < LANES - HEAD_DIM:
        qsel.append(jnp.zeros((LANES - HEAD_DIM - n_slc, qw), BF16))
    qsel = jnp.concatenate(qsel, axis=0)

    def sel_scores(st):
        k = kslc_ref[pl.ds(pl.multiple_of(st * SUPER_TILE, SUPER_TILE), SUPER_TILE), :]
        vt = jnp.concatenate([vslct_ref[2 * st], vslct_ref[2 * st + 1]], axis=1)
        return _dot(k, qsel), vt

    odd = qb & 1
    st_diag = qb >> 1
    st_prev = jnp.maximum(st_diag - 1, 0)
    n_far = st_prev
    row_a = pl.multiple_of(384 - Q_BLOCK * odd, Q_BLOCK)
    row_b = pl.multiple_of(jnp.where(st_diag >= 1, Q_BLOCK - Q_BLOCK * odd, XS_MASKED_ROW), Q_BLOCK)
    s_a, vt_a = sel_scores(st_diag)
    s_b, vt_b = sel_scores(st_prev)
    chain_a = _softmax_init(s_a + xs_ref[pl.ds(row_a, SUPER_TILE), :], vt_a)
    chain_b = _softmax_init(s_b + xs_ref[pl.ds(row_b, SUPER_TILE), :], vt_b)

    def far_pair(i, chains):
        a, b = chains
        s_a, vt_a = sel_scores(2 * i)
        s_b, vt_b = sel_scores(2 * i + 1)
        return _softmax_step(s_a, vt_a, a), _softmax_step(s_b, vt_b, b)

    chain_a, chain_b = lax.fori_loop(0, n_far >> 1, far_pair, (chain_a, chain_b))
    chain_a = lax.cond((n_far & 1) == 1,
                       lambda a: _softmax_step(*sel_scores(n_far - 1), a), lambda a: a, chain_a)
    o_slc = _softmax_merge(chain_a, chain_b)

    outs = []
    for r in range(GQA):
        cols = slice(r * Q_BLOCK, (r + 1) * Q_BLOCK)
        gate = [gt_ref[pl.ds(c * NSA_HEADS + g * GQA + r, 1), :] for c in range(3)]
        outs.append(gate[0] * o_cmp[:, cols] + gate[1] * o_slc[:, cols] + gate[2] * o_win[:, cols])
    out_ref[...] = jnp.concatenate(outs, axis=0).astype(BF16)


def _nsa(qt, kcmp, vcmpt, kslc, kwin, vt, gt, et, xs, xw, batch, seq):
    assert seq % SUPER_TILE == 0 and seq >= WIN_KEYS and seq // CMP_STRIDE <= 256 and seq // SLC_BLOCK <= HEAD_DIM
    nq = seq // Q_BLOCK
    n_cmp_rows = seq // CMP_STRIDE
    n_slc = seq // SLC_BLOCK
    n = batch * seq
    ovt = jnp.asarray(_overlap_t(n_cmp_rows, n_slc))
    gw = GQA * HEAD_DIM
    return pl.pallas_call(
        functools.partial(_nsa_kernel, n_cmp_rows=n_cmp_rows, n_slc=n_slc),
        grid=(batch, KV_HEADS, nq),
        in_specs=[
            pl.BlockSpec((gw, Q_BLOCK), lambda b, g, q: (g, b * nq + q)),
            pl.BlockSpec((None, n_cmp_rows, KV_WIDTH), lambda b, g, q: (b, 0, 0)),
            pl.BlockSpec((None, None, HEAD_DIM, n_cmp_rows), lambda b, g, q: (b, g, 0, 0)),
            pl.BlockSpec((None, seq, LANES), lambda b, g, q: (g, b, 0)),
            pl.BlockSpec((seq, LANES), lambda b, g, q: (b, 0)),
            pl.BlockSpec((None, nq, HEAD_DIM, LANES), lambda b, g, q: (g, b, 0, 0)),
            pl.BlockSpec((None, nq, HEAD_DIM, LANES), lambda b, g, q: (KV_HEADS + g, b, 0, 0)),
            pl.BlockSpec((GATE_ROWS, Q_BLOCK), lambda b, g, q: (0, b * nq + q)),
            pl.BlockSpec((GQA, ET_ROWS, Q_BLOCK), lambda b, g, q: (g, 0, 0)),
            pl.BlockSpec((None, XS_ROWS, GQA * LANES), lambda b, g, q: (g, 0, 0)),
            pl.BlockSpec((None, XW_ROWS, GQA * LANES), lambda b, g, q: (g, 0, 0)),
            pl.BlockSpec((n_slc, n_cmp_rows), lambda b, g, q: (0, 0)),
        ],
        out_specs=pl.BlockSpec((gw, Q_BLOCK), lambda b, g, q: (g, b * nq + q)),
        out_shape=jax.ShapeDtypeStruct((NSA_WIDTH, n), BF16),
        compiler_params=pltpu.CompilerParams(vmem_limit_bytes=VMEM_LIMIT),
        name="nsa",
    )(qt, kcmp, vcmpt, kslc, kwin, vt, vt, gt, et, xs, xw, ovt)


def _layer_norm_fm(h, g_ref, b_ref, reps):
    mu = jnp.mean(h, axis=0, keepdims=True)
    hc = h - mu
    var = jnp.mean(hc * hc, axis=0, keepdims=True)
    gain = jnp.concatenate([g_ref[...]] * reps, axis=1)
    bias = jnp.concatenate([b_ref[...]] * reps, axis=1)
    return hc * lax.rsqrt(var + LN_EPS) * gain + bias


def _mix_kernel(nsat_ref, zt_ref, halo_ref, x_ref, won_ref, wop_ref, pwt_ref, ps_ref, g_ref, b_ref,
                h_ref, hb_ref, *, tn, seq):
    t0 = (pl.program_id(0) * tn) % seq
    reps = tn // LANES
    zc = jnp.concatenate([halo_ref[...], zt_ref[...]], axis=1)
    t_ext = t0 - LANES + lax.broadcasted_iota(I32, (1, LANES + tn), 1)
    zc = jnp.where(t_ext >= 0, zc, 0.0)
    pooled = []
    for gi, win in enumerate(POOL_WINDOWS):
        zg = zc[gi * POOL_GROUP_DIM:(gi + 1) * POOL_GROUP_DIM]
        s = zg
        sh = 1
        while sh < win:
            s = s + pltpu.roll(s, sh, axis=1)
            sh *= 2
        cnt = jnp.clip(t_ext + 1, 1, win).astype(F32)
        y = (s / cnt - zg)[:, LANES:]
        scale = jnp.concatenate([ps_ref[gi * POOL_GROUP_DIM:(gi + 1) * POOL_GROUP_DIM]] * reps, axis=1)
        pooled.append((_dot(pwt_ref[gi], y.astype(BF16)) * scale).astype(BF16))
    pool_t = jnp.concatenate(pooled, axis=0)
    mix = _dot(won_ref[...], nsat_ref[...]) + _dot(wop_ref[...], pool_t)
    h = ALPHA * x_ref[...].T + mix
    h = _layer_norm_fm(h, g_ref, b_ref, reps)
    h_ref[...] = h
    hb_ref[...] = h.astype(BF16)


def _lane_bcast(v):
    return jnp.broadcast_to(v.reshape(-1, 1), (v.shape[0], LANES)).astype(F32)


def _mix(nsat, zt, x2, w_out, pool_w, pool_scale, ln_g, ln_b, seq):
    n = x2.shape[0]
    tn = 256
    wot = w_out.T.astype(BF16)
    full = lambda *s: pl.BlockSpec(s, lambda i: (0,) * len(s))
    return pl.pallas_call(
        functools.partial(_mix_kernel, tn=tn, seq=seq),
        grid=(n // tn,),
        in_specs=[
            pl.BlockSpec((NSA_WIDTH, tn), lambda i: (0, i)),
            pl.BlockSpec((POOL_WIDTH, tn), lambda i: (0, i)),
            pl.BlockSpec((POOL_WIDTH, LANES), lambda i: (0, jnp.maximum(i * (tn // LANES) - 1, 0))),
            pl.BlockSpec((tn, D_MODEL), lambda i: (i, 0)),
            full(D_MODEL, NSA_WIDTH), full(D_MODEL, POOL_WIDTH),
            full(len(POOL_WINDOWS), POOL_GROUP_DIM, POOL_GROUP_DIM),
            full(POOL_WIDTH, LANES), full(D_MODEL, LANES), full(D_MODEL, LANES),
        ],
        out_specs=[pl.BlockSpec((D_MODEL, tn), lambda i: (0, i))] * 2,
        out_shape=[jax.ShapeDtypeStruct((D_MODEL, n), F32), jax.ShapeDtypeStruct((D_MODEL, n), BF16)],
        compiler_params=pltpu.CompilerParams(vmem_limit_bytes=VMEM_LIMIT),
        name="mix",
    )(nsat, zt, zt, x2, wot[:, :NSA_WIDTH], wot[:, NSA_WIDTH:],
      jnp.swapaxes(pool_w, 1, 2).astype(BF16), _lane_bcast(pool_scale), _lane_bcast(ln_g), _lane_bcast(ln_b))


ROUTE_HEAD_GROUP = 4
_REMOVED = -(2.0 ** 100)
_PAD_SCORE = -3e38


def _top16_by_index(s):
    rows = lax.broadcasted_iota(I32, s.shape, 0).astype(F32)
    row16 = lax.broadcasted_iota(I32, (PEER_TOPK, s.shape[1]), 0)
    cur = s
    rank = jnp.full(s.shape, float(PEER_TOPK), F32)
    vals = jnp.zeros((PEER_TOPK, s.shape[1]), F32)
    for r in range(PEER_TOPK):
        mx = jnp.max(cur, axis=0, keepdims=True)
        first = jnp.min(jnp.where(cur == mx, rows, float(s.shape[0])), axis=0, keepdims=True)
        hit = rows == first
        vals = jnp.where(row16 == r, mx, vals)
        rank = jnp.where(hit, float(r), rank)
        cur = jnp.where(hit, -jnp.inf, cur)
    return vals, rank, jnp.full((1, s.shape[1]), float(PEER_TOPK), F32)


def _top16_by_value(scores):
    width = scores[0].shape[1]
    row16 = lax.broadcasted_iota(I32, (PEER_TOPK, width), 0)
    cur = list(scores)
    vals = [jnp.zeros((PEER_TOPK, width), F32) for _ in scores]
    for r in range(PEER_TOPK):
        mx = [jnp.max(c, axis=0, keepdims=True) for c in cur]
        vals = [jnp.where(row16 == r, m, v) for m, v in zip(mx, vals)]
        cur = [jnp.where(c == m, _REMOVED * (r + 1), c) for c, m in zip(cur, mx)]
    out = []
    for c, v in zip(cur, vals):
        member = c <= _REMOVED
        rank = jnp.where(member, c * (1.0 / _REMOVED) - 1.0, float(PEER_TOPK))
        out.append((v, rank, jnp.sum(jnp.where(member, 1.0, 0.0), axis=0, keepdims=True)))
    return out


def _candidate_table(v1, v2):
    tn = v1.shape[1]
    row8 = lax.broadcasted_iota(I32, (SUBLANES, tn), 0)
    groups = [v1[0:1, :] + v2]
    real = [PEER_TOPK]
    for a in range(1, SUBLANES):
        nb = PEER_TOPK // (a + 1)
        groups.append(jnp.where(row8 < nb, v1[a:a + 1, :] + v2[0:SUBLANES, :], _PAD_SCORE))
        real.append(nb)
    groups.append(v1[SUBLANES:, :] + v2[0:1, :])
    real.append(PEER_TOPK - SUBLANES)
    return jnp.concatenate(groups, axis=0), real


def _peer_scores(h, hb_ref, wqt_ref, sk1_ref, sk2_ref):
    half = PEER_QDIM // 2
    q = _dot(wqt_ref[h * PEER_QDIM:(h + 1) * PEER_QDIM, :], hb_ref[...])
    return _dot(sk1_ref[...], q[:half].astype(BF16)), _dot(sk2_ref[...], q[half:].astype(BF16))


def _route_heads(heads, scores, outs, *, tn, by_value):
    n_ref, a_ref, r2_ref, b_ref = outs
    want = float(PEER_TOPK)
    miscounts = []
    per_head = []
    for h, (s1, s2) in zip(heads, scores):
        if by_value:
            (v1, rank1, count1), (v2, rank2, count2) = _top16_by_value([s1, s2])
        else:
            (v1, rank1, count1), (v2, rank2, count2) = _top16_by_index(s1), _top16_by_index(s2)
        miscounts.append(jnp.maximum(jnp.abs(count1 - want), jnp.abs(count2 - want)))
        cs, real = _candidate_table(v1, v2)
        per_head.append((h, s1, s2, v1, v2, rank1, rank2, cs))

    if by_value:
        cur = [ph[-1] for ph in per_head]
        for _ in range(PEER_TOPK):
            cur = [jnp.where(c == jnp.max(c, axis=0, keepdims=True), -jnp.inf, c) for c in cur]
        sels = [c == -jnp.inf for c in cur]
    else:
        sels = []
        for ph in per_head:
            cs = ph[-1]
            c_i = lax.broadcasted_iota(I32, cs.shape, 0)
            cnt = jnp.zeros(cs.shape, F32)
            base = 0
            for g, n_real in enumerate(real):
                for c in range(base, base + n_real):
                    row = cs[c:c + 1, :]
                    beats = (row > cs) | ((row == cs) & (c_i > c))
                    cnt = cnt + jnp.where(beats, 1.0, 0.0)
                base += PEER_TOPK if g == 0 else SUBLANES
            sels.append(cnt < want)

    for i, ((h, s1, s2, v1, v2, rank1, rank2, cs), sel) in enumerate(zip(per_head, sels)):
        self_f = jnp.where(sel, 1.0, 0.0)
        miscounts[i] = jnp.maximum(miscounts[i], jnp.abs(jnp.sum(self_f, axis=0, keepdims=True) - want))
        top = v1[0:1, :] + v2[0:1, :]
        z = jnp.sum(jnp.where(sel, jnp.exp(cs - top), 0.0), axis=0, keepdims=True)
        nmap = jnp.zeros((PEER_KEYS, tn), F32)
        base = 0
        for a in range(PEER_TOPK):
            if a < SUBLANES:
                size = PEER_TOPK if a == 0 else SUBLANES
                n_a = jnp.sum(self_f[base:base + size, :], axis=0, keepdims=True)
                base += size
            else:
                n_a = self_f[base + a - SUBLANES:base + a - SUBLANES + 1, :]
            nmap = jnp.where(rank1 == float(a), n_a, nmap)
        n_ref[h] = nmap
        a_ref[h] = jnp.where(rank1 < want, jnp.exp(s1 - v1[0:1, :]), 0.0)
        r2_ref[h] = rank2.astype(BF16)
        b_ref[h] = (jnp.where(rank2 < want, jnp.exp(s2 - v2[0:1, :]), 0.0) / z).astype(BF16)
    return miscounts


def _route_kernel(hb_ref, wqt_ref, sk1_ref, sk2_ref, n_ref, a_ref, r2_ref, b_ref, *, tn):
    scores = functools.partial(_peer_scores, hb_ref=hb_ref, wqt_ref=wqt_ref, sk1_ref=sk1_ref, sk2_ref=sk2_ref)
    outs = (n_ref, a_ref, r2_ref, b_ref)
    groups = [range(h0, h0 + ROUTE_HEAD_GROUP) for h0 in range(0, PEER_HEADS, ROUTE_HEAD_GROUP)]
    miscounts = []
    ready = [scores(h) for h in groups[0]]
    for gi, heads in enumerate(groups):
        ahead = [scores(h) for h in groups[gi + 1]] if gi + 1 < len(groups) else None
        miscounts += _route_heads(heads, ready, outs, tn=tn, by_value=True)
        ready = ahead

    any_miscount = functools.reduce(jnp.maximum, miscounts)

    @pl.when(jnp.max(any_miscount) > 0.0)
    def _():
        for h in range(PEER_HEADS):
            @pl.when(jnp.max(miscounts[h]) > 0.0)
            def _(h=h):
                _route_heads([h], [scores(h)], outs, tn=tn, by_value=False)


def _route(hb, wq, subkeys):
    n = hb.shape[1]
    tn = LANES
    full = lambda *s: pl.BlockSpec(s, lambda i: (0,) * len(s))
    tab = lambda: pl.BlockSpec((PEER_HEADS, PEER_KEYS, tn), lambda i: (0, 0, i))
    shape = (PEER_HEADS, PEER_KEYS, n)
    return pl.pallas_call(
        functools.partial(_route_kernel, tn=tn),
        grid=(n // tn,),
        in_specs=[
            pl.BlockSpec((D_MODEL, tn), lambda i: (0, i)),
            full(PEER_HEADS * PEER_QDIM, D_MODEL), full(PEER_KEYS, PEER_QDIM // 2), full(PEER_KEYS, PEER_QDIM // 2),
        ],
        out_specs=[tab(), tab(), tab(), tab()],
        out_shape=[
            jax.ShapeDtypeStruct(shape, F32),
            jax.ShapeDtypeStruct(shape, F32),
            jax.ShapeDtypeStruct(shape, BF16),
            jax.ShapeDtypeStruct(shape, BF16),
        ],
        compiler_params=pltpu.CompilerParams(vmem_limit_bytes=VMEM_LIMIT),
        name="peer_route",
    )(hb, wq.T.astype(BF16), subkeys[0].astype(BF16), subkeys[1].astype(BF16))


def _experts_kernel(hb_ref, u_ref, vt_ref, n_ref, a_ref, r2_ref, b_ref, h_ref, g_ref, bb_ref,
                    out_ref, acc_ref, rows_ref, *, tn, te):
    j = pl.program_id(1)
    keys_per_step = te // PEER_KEYS

    def stage_rows(tile, slot):
        for c in range(keys_per_step):
            for h in range(PEER_HEADS):
                for k, ref in enumerate((n_ref, a_ref)):
                    row = ref[h, pl.ds(tile * keys_per_step + c, 1), :]
                    rows_ref[slot, k, c, h] = jnp.broadcast_to(row, (BF16_ROWS, tn)).astype(BF16)

    @pl.when(j == 0)
    def _():
        acc_ref[...] = jnp.zeros_like(acc_ref)
        stage_rows(0, 0)

    slot = j & 1
    hb = hb_ref[...]
    part = te // EXPERT_PARTS
    acts = [_dot(u_ref[i * part:(i + 1) * part, :], hb) for i in range(EXPERT_PARTS)]
    chunks = part // PEER_KEYS
    tiles = PEER_KEYS // BF16_ROWS
    zero = jnp.zeros((tiles, BF16_ROWS, tn), BF16)

    for i in range(EXPERT_PARTS):
        act = _gelu(acts[i])
        ps = []
        for c in range(chunks):
            key = i * chunks + c
            w = zero
            for h in range(PEER_HEADS):
                picked = jnp.where(r2_ref[h] < rows_ref[slot, 0, key, h][None], b_ref[h], zero)
                w = w + rows_ref[slot, 1, key, h][None] * picked
            ps.append(w.reshape(PEER_KEYS, tn) * act[c * PEER_KEYS:(c + 1) * PEER_KEYS].astype(BF16))
        acc_ref[...] += _dot(vt_ref[:, i * part:(i + 1) * part], jnp.concatenate(ps, axis=0))

    stage_rows(jnp.minimum(j + 1, pl.num_programs(1) - 1), 1 - slot)

    @pl.when(j == pl.num_programs(1) - 1)
    def _():
        h = ALPHA * h_ref[...] + acc_ref[...]
        out_ref[...] = _layer_norm_fm(h, g_ref, bb_ref, tn // LANES).T


def _experts(hb, hf, u_b, vt_b, tabs, ln_g, ln_b):
    n = hb.shape[1]
    tn, te = 512, 1024
    tab = lambda: pl.BlockSpec((PEER_HEADS, PEER_KEYS, tn), lambda i, j: (0, 0, i))
    tiles = PEER_KEYS // BF16_ROWS
    tiled = lambda: pl.BlockSpec((PEER_HEADS, tiles, BF16_ROWS, tn), lambda i, j: (0, 0, 0, i))
    n_tab, a_tab, r2_tab, b_tab = tabs
    tabs = (n_tab, a_tab, r2_tab.reshape(PEER_HEADS, tiles, BF16_ROWS, n), b_tab.reshape(PEER_HEADS, tiles, BF16_ROWS, n))
    return pl.pallas_call(
        functools.partial(_experts_kernel, tn=tn, te=te),
        grid=(n // tn, PEER_EXPERTS // te),
        in_specs=[
            pl.BlockSpec((D_MODEL, tn), lambda i, j: (0, i)),
            pl.BlockSpec((te, D_MODEL), lambda i, j: (j, 0)),
            pl.BlockSpec((D_MODEL, te), lambda i, j: (0, j)),
            tab(), tab(), tiled(), tiled(),
            pl.BlockSpec((D_MODEL, tn), lambda i, j: (0, i)),
            pl.BlockSpec((D_MODEL, LANES), lambda i, j: (0, 0)),
            pl.BlockSpec((D_MODEL, LANES), lambda i, j: (0, 0)),
        ],
        out_specs=pl.BlockSpec((tn, D_MODEL), lambda i, j: (i, 0)),
        out_shape=jax.ShapeDtypeStruct((n, D_MODEL), F32),
        scratch_shapes=[pltpu.VMEM((D_MODEL, tn), F32),
                        pltpu.VMEM((2, 2, te // PEER_KEYS, PEER_HEADS, BF16_ROWS, tn), BF16)],
        compiler_params=pltpu.CompilerParams(
            dimension_semantics=("arbitrary", "arbitrary"), vmem_limit_bytes=VMEM_LIMIT),
        name="peer_experts",
    )(hb, u_b, vt_b, *tabs, hf, _lane_bcast(ln_g), _lane_bcast(ln_b))


def kernel(x, w_in, rel_table, cmp_pe_k, cmp_pe_v, cmp_k_w1, cmp_k_w2, cmp_v_w1, cmp_v_w2, pool_w, pool_scale,
           w_out, ln1_g, ln1_b, peer_wq, peer_subkeys, peer_u, peer_v, ln2_g, ln2_b):
    batch, seq, _ = x.shape
    assert w_in.shape[0] == DEPTH == 1 and seq % 512 == 0 and seq // CMP_STRIDE <= 256
    x2 = x.reshape(batch * seq, D_MODEL)
    et, xs, xw = _bias_tables(rel_table)
    qt, vt, gt, zt, kvc, kslc, kwin = _inproj(x2, w_in[0], seq)
    kcmp, vcmpt = _compress(kvc, batch, seq, cmp_pe_k[0], cmp_pe_v[0], cmp_k_w1[0], cmp_k_w2[0],
                            cmp_v_w1[0], cmp_v_w2[0])
    nsat = _nsa(qt, kcmp, vcmpt, kslc, kwin, vt, gt, et, xs, xw, batch, seq)
    hf, hb = _mix(nsat, zt, x2, w_out[0], pool_w[0], pool_scale[0], ln1_g[0], ln1_b[0], seq)
    tabs = _route(hb, peer_wq[0], peer_subkeys[0])
    y = _experts(hb, hf, peer_u[0].astype(BF16), peer_v[0].T.astype(BF16), tabs, ln2_g[0], ln2_b[0])
    return y.reshape(batch, seq, D_MODEL)
```

```python
import functools
import math

import numpy as np
import jax
import jax.numpy as jnp
from jax import lax
from jax.experimental import pallas as pl
from jax.experimental.pallas import tpu as pltpu

F32 = jnp.float32
BF16 = jnp.bfloat16
I32 = jnp.int32

D_MODEL = 1024
NSA_HEADS = 8
KV_HEADS = 2
GQA = NSA_HEADS // KV_HEADS
HEAD_DIM = 64
NSA_WIDTH = NSA_HEADS * HEAD_DIM
KV_WIDTH = KV_HEADS * HEAD_DIM
POOL_WIDTH = D_MODEL - NSA_WIDTH
POOL_WINDOWS = (2, 4, 8, 16)
POOL_GROUP_DIM = POOL_WIDTH // len(POOL_WINDOWS)
N_GATES = 3 * NSA_HEADS
GATE_ROWS = 32
CMP_BLOCK = 32
CMP_STRIDE = 16
CMP_HIDDEN = 256
SLC_BLOCK = 64
SLC_COUNT = 16
WINDOW = 512
Q_BLOCK = 128
NEG = -1e30
FORCE = 1e6
N_BUCKETS = 32
MAX_DISTANCE = 128
PEER_HEADS = 8
PEER_KEYS = 128
PEER_EXPERTS = PEER_KEYS * PEER_KEYS
PEER_QDIM = 256
PEER_TOPK = 16
DEPTH = 1
ALPHA = (2 * DEPTH) ** 0.25
LN_EPS = 1e-5
SQRT_HALF = 0.7071067811865476

LANES = 128
SUBLANES = 8
BF16_ROWS = SUBLANES
EXPERT_PARTS = 2
VMEM_LIMIT = 52 * 1024 * 1024

_FM_Q, _FM_V, _FM_G, _FM_Z, _FM_END = 0, 512, 768, 800, 1312
_TM_WIDTH = 640


def _gelu(x):
    return 0.5 * x * (1.0 + lax.erf(x * SQRT_HALF))


def _dot(a, b):
    return jnp.dot(a, b, preferred_element_type=F32)


def _dot_nt(a, b):
    return lax.dot_general(a, b, (((1,), (1,)), ((), ())), preferred_element_type=F32)


def _bucket_np(dist):
    dist = np.maximum(dist, 0)
    max_exact = N_BUCKETS // 2
    large = max_exact + np.floor(
        np.log(np.maximum(dist, 1) / max_exact) / math.log(MAX_DISTANCE / max_exact)
        * (N_BUCKETS - max_exact)).astype(np.int64)
    large = np.minimum(large, N_BUCKETS - 1)
    return np.where(dist < max_exact, dist, large)


ET_ROWS = 512
XS_ROWS = 896
XS_MASKED_ROW = 640
XW_ROWS = 1152
_CODE_MASKED = N_BUCKETS


def _bias_index_tables():
    qi = np.arange(Q_BLOCK)[None, :]

    def codes(dist, visible):
        return np.where(visible, _bucket_np(dist), _CODE_MASKED).astype(np.int32)

    d_cmp = qi - CMP_STRIDE * (np.arange(ET_ROWS)[:, None] - 256) - (CMP_BLOCK - 1)
    rho = np.arange(XS_ROWS)[:, None]
    d_sel = qi - (rho - 384)
    d_win = qi - (np.arange(XW_ROWS)[:, None] - 512)
    return np.concatenate([
        codes(d_cmp, d_cmp >= 0),
        codes(d_sel, (d_sel >= 0) & (rho < XS_MASKED_ROW)),
        codes(d_win, (d_win >= 0) & (d_win < WINDOW)),
    ], axis=0)


def _overlap_t(n_cmp_rows, n_slc):
    n = np.arange(n_cmp_rows)[None, :]
    j = np.arange(n_slc)[:, None]
    start = n * CMP_STRIDE
    end = start + CMP_BLOCK - 1
    return ((start <= j * SLC_BLOCK + SLC_BLOCK - 1) & (end >= j * SLC_BLOCK)).astype(np.float32)


def _bias_kernel(tab_ref, idx_ref, et_ref, xs_ref, xw_ref):
    g = pl.program_id(0)
    chunk = Q_BLOCK
    for r in range(GQA):
        h = g * GQA + r
        far = tab_ref[N_BUCKETS - 1, h]
        cols = slice(r * LANES, (r + 1) * LANES)
        for c in range((ET_ROWS + XS_ROWS + XW_ROWS) // chunk):
            idx = idx_ref[c * chunk:(c + 1) * chunk, :]
            acc = jnp.where(idx == _CODE_MASKED, NEG, 0.0)
            for b in range(N_BUCKETS - 1):
                acc = jnp.where(idx == b, tab_ref[b, h] - far, acc)
            row = c * chunk
            if row < ET_ROWS:
                et_ref[r, row:row + chunk, :] = acc
            elif row < ET_ROWS + XS_ROWS:
                xs_ref[0, row - ET_ROWS:row - ET_ROWS + chunk, cols] = acc
            else:
                row -= ET_ROWS + XS_ROWS
                xw_ref[0, row:row + chunk, cols] = acc


def _bias_tables(rel_table):
    idx = _bias_index_tables()
    qw = GQA * LANES
    return pl.pallas_call(
        _bias_kernel,
        grid=(KV_HEADS,),
        in_specs=[
            pl.BlockSpec(memory_space=pltpu.SMEM),
            pl.BlockSpec(idx.shape, lambda g: (0, 0)),
        ],
        out_specs=[
            pl.BlockSpec((GQA, ET_ROWS, Q_BLOCK), lambda g: (g, 0, 0)),
            pl.BlockSpec((1, XS_ROWS, qw), lambda g: (g, 0, 0)),
            pl.BlockSpec((1, XW_ROWS, qw), lambda g: (g, 0, 0)),
        ],
        out_shape=[
            jax.ShapeDtypeStruct((NSA_HEADS, ET_ROWS, Q_BLOCK), F32),
            jax.ShapeDtypeStruct((KV_HEADS, XS_ROWS, qw), F32),
            jax.ShapeDtypeStruct((KV_HEADS, XW_ROWS, qw), F32),
        ],
        name="bias_tables",
    )(rel_table, jnp.asarray(idx))


def _inproj_kernel(x_ref, wfm_ref, wtm_ref, qt_ref, vt_ref, gt_ref, zt_ref, kvc_ref,
                   kslc_ref, kwin_ref, *, tn, seq):
    xb = x_ref[...].astype(BF16)
    fm = _dot_nt(wfm_ref[...], xb)
    tm = _dot(xb, wtm_ref[...])
    qt_ref[...] = (fm[_FM_Q:_FM_V] * (HEAD_DIM ** -0.5)).astype(BF16)
    for c in range(2 * KV_HEADS):
        for s in range(tn // LANES):
            vt_ref[c, s] = fm[_FM_V + c * HEAD_DIM:_FM_V + (c + 1) * HEAD_DIM,
                              s * LANES:(s + 1) * LANES].astype(BF16)
    gt_ref[...] = jax.nn.sigmoid(fm[_FM_G:_FM_Z])
    zt_ref[...] = fm[_FM_Z:_FM_END]
    kvc_ref[0] = tm[:, 0:KV_WIDTH]
    kvc_ref[1] = tm[:, KV_WIDTH:2 * KV_WIDTH]
    t0 = (pl.program_id(0) * tn) % seq
    row = lax.broadcasted_iota(I32, (tn, LANES), 0) + t0
    lane = lax.broadcasted_iota(I32, (tn, LANES), 1)
    onehot = jnp.where(lane - HEAD_DIM == (row >> 6), 1.0, 0.0)
    for g in range(KV_HEADS):
        k = tm[:, 256 + g * LANES:256 + (g + 1) * LANES]
        kslc_ref[g] = jnp.where(lane < HEAD_DIM, k, onehot).astype(BF16)
    kwin_ref[...] = tm[:, 512:640].astype(BF16)


def _inproj(x2, w_in, seq):
    n = x2.shape[0]
    tn = 512
    w = w_in
    o = NSA_WIDTH
    w_q = w[:, 0:o]
    w_kc, w_vc, w_ks, w_vs, w_kw, w_vw = [w[:, o + i * KV_WIDTH:o + (i + 1) * KV_WIDTH] for i in range(6)]
    o += 6 * KV_WIDTH
    w_g = w[:, o:o + N_GATES]
    w_z = w[:, o + N_GATES:]
    zpad = jnp.zeros((D_MODEL, HEAD_DIM), w.dtype)
    wfm = jnp.concatenate(
        [w_q, w_vs, w_vw, w_g, jnp.zeros((D_MODEL, GATE_ROWS - N_GATES), w.dtype), w_z], axis=1).T.astype(BF16)
    wtm = jnp.concatenate(
        [w_kc, w_vc, w_ks[:, :HEAD_DIM], zpad, w_ks[:, HEAD_DIM:], zpad, w_kw], axis=1).astype(BF16)
    nt = n // LANES
    return pl.pallas_call(
        functools.partial(_inproj_kernel, tn=tn, seq=seq),
        grid=(n // tn,),
        in_specs=[
            pl.BlockSpec((tn, D_MODEL), lambda i: (i, 0)),
            pl.BlockSpec((_FM_END, D_MODEL), lambda i: (0, 0)),
            pl.BlockSpec((D_MODEL, _TM_WIDTH), lambda i: (0, 0)),
        ],
        out_specs=[
            pl.BlockSpec((NSA_WIDTH, tn), lambda i: (0, i)),
            pl.BlockSpec((2 * KV_HEADS, tn // LANES, HEAD_DIM, LANES), lambda i: (0, i, 0, 0)),
            pl.BlockSpec((GATE_ROWS, tn), lambda i: (0, i)),
            pl.BlockSpec((POOL_WIDTH, tn), lambda i: (0, i)),
            pl.BlockSpec((2, tn, KV_WIDTH), lambda i: (0, i, 0)),
            pl.BlockSpec((KV_HEADS, tn, LANES), lambda i: (0, i, 0)),
            pl.BlockSpec((tn, LANES), lambda i: (i, 0)),
        ],
        out_shape=[
            jax.ShapeDtypeStruct((NSA_WIDTH, n), BF16),
            jax.ShapeDtypeStruct((2 * KV_HEADS, nt, HEAD_DIM, LANES), BF16),
            jax.ShapeDtypeStruct((GATE_ROWS, n), F32),
            jax.ShapeDtypeStruct((POOL_WIDTH, n), F32),
            jax.ShapeDtypeStruct((2, n, KV_WIDTH), F32),
            jax.ShapeDtypeStruct((KV_HEADS, n, LANES), BF16),
            jax.ShapeDtypeStruct((n, LANES), BF16),
        ],
        compiler_params=pltpu.CompilerParams(vmem_limit_bytes=VMEM_LIMIT),
        name="inproj",
    )(x2, wfm, wtm)


def _compress_kernel(kc_ref, vc_ref, w2k_ref, w2v_ref, pek_ref, pev_ref, w1k_ref, w1v_ref,
                     wok0_ref, wok1_ref, wovt_ref, kcmp_ref, vcmpt_ref, *, n_rows):
    half = CMP_BLOCK // 2
    acc_k = jnp.zeros((n_rows, 4 * CMP_HIDDEN), F32)
    acc_v = jnp.zeros((n_rows, 4 * CMP_HIDDEN), F32)
    for l in range(half):
        rows = pl.ds(l, n_rows, stride=CMP_STRIDE)
        acc_k = acc_k + _dot(kc_ref[rows, :].astype(BF16), w2k_ref[l])
        acc_v = acc_v + _dot(vc_ref[rows, :].astype(BF16), w2v_ref[l])
    ck = _dot(pek_ref[...], w1k_ref[...])[0:1]
    cv = _dot(pev_ref[...], w1v_ref[...])[0:1]

    def hidden(acc, c, g):
        a = acc[:, g * CMP_HIDDEN:(g + 1) * CMP_HIDDEN]
        b = acc[:, (2 + g) * CMP_HIDDEN:(3 + g) * CMP_HIDDEN]
        return _gelu(a + pltpu.roll(b, n_rows - 1, axis=0) + c).astype(BF16)

    kcmp_ref[...] = (_dot(hidden(acc_k, ck, 0), wok0_ref[...])
                     + _dot(hidden(acc_k, ck, 1), wok1_ref[...])).astype(BF16)
    for g in range(KV_HEADS):
        vcmpt_ref[g] = _dot_nt(wovt_ref[...], hidden(acc_v, cv, g)).astype(BF16)


def _compress(kvc, batch, seq, pe_k, pe_v, k_w1, k_w2, v_w1, v_w2):
    n_rows = seq // CMP_STRIDE
    half = CMP_BLOCK // 2

    def blockdiag(w1):
        w = w1.reshape(CMP_BLOCK, HEAD_DIM, CMP_HIDDEN)
        z = jnp.zeros((half, HEAD_DIM, CMP_HIDDEN), w1.dtype)
        top = jnp.concatenate([w[:half], z, w[half:], z], axis=2)
        bot = jnp.concatenate([z, w[:half], z, w[half:]], axis=2)
        return jnp.concatenate([top, bot], axis=1).astype(BF16)

    def pe_rows(pe):
        flat = pe.reshape(1, CMP_BLOCK * HEAD_DIM)
        return jnp.concatenate([flat, jnp.zeros((7, flat.shape[1]), pe.dtype)], axis=0).astype(BF16)

    zo = jnp.zeros((CMP_HIDDEN, HEAD_DIM), k_w2.dtype)
    wok0 = jnp.concatenate([k_w2, zo], axis=1).astype(BF16)
    wok1 = jnp.concatenate([zo, k_w2], axis=1).astype(BF16)
    full = lambda *s: pl.BlockSpec(s, lambda b: (0,) * len(s))
    return pl.pallas_call(
        functools.partial(_compress_kernel, n_rows=n_rows),
        grid=(batch,),
        in_specs=[
            pl.BlockSpec((None, seq, KV_WIDTH), lambda b: (0, b, 0)),
            pl.BlockSpec((None, seq, KV_WIDTH), lambda b: (1, b, 0)),
            full(half, KV_WIDTH, 4 * CMP_HIDDEN), full(half, KV_WIDTH, 4 * CMP_HIDDEN),
            full(8, CMP_BLOCK * HEAD_DIM), full(8, CMP_BLOCK * HEAD_DIM),
            full(CMP_BLOCK * HEAD_DIM, CMP_HIDDEN), full(CMP_BLOCK * HEAD_DIM, CMP_HIDDEN),
            full(CMP_HIDDEN, KV_WIDTH), full(CMP_HIDDEN, KV_WIDTH), full(HEAD_DIM, CMP_HIDDEN),
        ],
        out_specs=[
            pl.BlockSpec((None, n_rows, KV_WIDTH), lambda b: (b, 0, 0)),
            pl.BlockSpec((None, KV_HEADS, HEAD_DIM, n_rows), lambda b: (b, 0, 0, 0)),
        ],
        out_shape=[
            jax.ShapeDtypeStruct((batch, n_rows, KV_WIDTH), BF16),
            jax.ShapeDtypeStruct((batch, KV_HEADS, HEAD_DIM, n_rows), BF16),
        ],
        compiler_params=pltpu.CompilerParams(vmem_limit_bytes=VMEM_LIMIT),
        name="compress",
    )(kvc, kvc, blockdiag(k_w1), blockdiag(v_w1), pe_rows(pe_k), pe_rows(pe_v),
      k_w1.astype(BF16), v_w1.astype(BF16), wok0, wok1, v_w2.T.astype(BF16))


SUPER_TILE = 2 * Q_BLOCK
WIN_KEYS = WINDOW + Q_BLOCK


def _softmax_init(s, vt):
    m = jnp.max(s, axis=0, keepdims=True)
    p = jnp.exp(s - m)
    return m, jnp.sum(p, axis=0, keepdims=True), _dot(vt, p.astype(BF16))


def _softmax_step(s, vt, state):
    m, l, acc = state
    m_new = jnp.maximum(m, jnp.max(s, axis=0, keepdims=True))
    alpha = jnp.exp(m - m_new)
    p = jnp.exp(s - m_new)
    return m_new, alpha * l + jnp.sum(p, axis=0, keepdims=True), alpha * acc + _dot(vt, p.astype(BF16))


def _softmax_merge(a, b):
    m = jnp.maximum(a[0], b[0])
    wa = jnp.exp(a[0] - m)
    wb = jnp.exp(b[0] - m)
    return (wa * a[2] + wb * b[2]) * (1.0 / (wa * a[1] + wb * b[1]))


def _nsa_kernel(qt_ref, kcmp_ref, vcmpt_ref, kslc_ref, kwin_ref, vslct_ref, vwint_ref, gt_ref,
                et_ref, xs_ref, xw_ref, ovt_ref, out_ref, *, n_cmp_rows, n_slc):
    g = pl.program_id(1)
    qb = pl.program_id(2)
    qw = GQA * Q_BLOCK
    q4 = qt_ref[...]
    qcat = jnp.concatenate([q4[r * HEAD_DIM:(r + 1) * HEAD_DIM] for r in range(GQA)], axis=1)
    zq = jnp.zeros_like(qcat)
    qg = jnp.concatenate([jnp.where(g == 0, qcat, zq), jnp.where(g == 1, qcat, zq)], axis=0)

    s = _dot(kcmp_ref[...], qg)
    off = pl.multiple_of(256 - 8 * qb, 8)
    s = s + jnp.concatenate([et_ref[r, pl.ds(off, n_cmp_rows), :] for r in range(GQA)], axis=1)
    m = jnp.max(s, axis=0, keepdims=True)
    p = jnp.exp(s - m)
    l = jnp.sum(p, axis=0, keepdims=True)
    p = p * jnp.where(m > 0.5 * NEG, 1.0 / l, 0.0)
    o_cmp = _dot(vcmpt_ref[...], p.astype(BF16))
    p_sum = p[:, 0:Q_BLOCK]
    for r in range(1, GQA):
        p_sum = p_sum + p[:, r * Q_BLOCK:(r + 1) * Q_BLOCK]
    imp = jnp.dot(ovt_ref[...], p_sum, preferred_element_type=F32,
                  precision=lax.Precision.HIGHEST)

    k0 = jnp.maximum(qb * Q_BLOCK - WINDOW, 0)
    row_w = pl.multiple_of(WINDOW - qb * Q_BLOCK + k0, Q_BLOCK)
    s_win = _dot(kwin_ref[pl.ds(pl.multiple_of(k0, Q_BLOCK), WIN_KEYS), :], qg)

    j_i = lax.broadcasted_iota(I32, (n_slc, Q_BLOCK), 0)
    t1 = qb * Q_BLOCK + lax.broadcasted_iota(I32, (n_slc, Q_BLOCK), 1)
    cur = t1 >> 6
    forced = (j_i == 0) | (j_i == cur) | (j_i == cur - 1)
    imp = jnp.where(forced, FORCE, imp)
    imp = jnp.where(j_i <= cur, imp, NEG)
    cnt = jnp.zeros((n_slc, Q_BLOCK), F32)
    for jp in range(n_slc):
        row = imp[jp:jp + 1, :]
        beats = (row > imp) | ((row == imp) & (j_i > jp))
        cnt = cnt + jnp.where(beats, 1.0, 0.0)
    selneg = jnp.where(cnt < float(SLC_COUNT), 0.0, NEG).astype(BF16)
    qsel = [qcat, jnp.concatenate([selneg] * GQA, axis=1)]
    if n_slc < LANES - HEAD_DIM:
        qsel.append(jnp.zeros((LANES - HEAD_DIM - n_slc, qw), BF16))
    qsel = jnp.concatenate(qsel, axis=0)

    def sel_scores(st):
        k = kslc_ref[pl.ds(pl.multiple_of(st * SUPER_TILE, SUPER_TILE), SUPER_TILE), :]
        vt = jnp.concatenate([vslct_ref[2 * st], vslct_ref[2 * st + 1]], axis=1)
        return _dot(k, qsel), vt

    odd = qb & 1
    st_diag = qb >> 1
    st_prev = jnp.maximum(st_diag - 1, 0)
    n_far = st_prev
    row_a = pl.multiple_of(384 - Q_BLOCK * odd, Q_BLOCK)
    row_b = pl.multiple_of(jnp.where(st_diag >= 1, Q_BLOCK - Q_BLOCK * odd, XS_MASKED_ROW), Q_BLOCK)
    s_a, vt_a = sel_scores(st_diag)
    s_b, vt_b = sel_scores(st_prev)

    kt0 = k0 >> 7
    vt = jnp.concatenate([vwint_ref[kt0 + i] for i in range(WIN_KEYS // Q_BLOCK)], axis=1)
    m, l, acc = _softmax_init(s_win + xw_ref[pl.ds(row_w, WIN_KEYS), :], vt)
    o_win = acc * (1.0 / l)

    chain_a = _softmax_init(s_a + xs_ref[pl.ds(row_a, SUPER_TILE), :], vt_a)
    chain_b = _softmax_init(s_b + xs_ref[pl.ds(row_b, SUPER_TILE), :], vt_b)

    def far_steps(first, count, chains):
        scored = [sel_scores(first + k) for k in range(count)]
        chains = list(chains)
        for k, (s, vt) in enumerate(scored):
            chains[k % 2] = _softmax_step(s, vt, chains[k % 2])
        return tuple(chains)

    chains = lax.fori_loop(0, n_far >> 2, lambda i, c: far_steps(4 * i, 4, c), (chain_a, chain_b))
    rest = n_far & ~3
    chains = lax.cond((n_far & 2) != 0, lambda c: far_steps(rest, 2, c), lambda c: c, chains)
    chain_a, chain_b = lax.cond((n_far & 1) != 0, lambda c: far_steps(n_far - 1, 1, c), lambda c: c, chains)
    o_slc = _softmax_merge(chain_a, chain_b)

    outs = []
    for r in range(GQA):
        cols = slice(r * Q_BLOCK, (r + 1) * Q_BLOCK)
        gate = [gt_ref[pl.ds(c * NSA_HEADS + g * GQA + r, 1), :] for c in range(3)]
        outs.append(gate[0] * o_cmp[:, cols] + gate[1] * o_slc[:, cols] + gate[2] * o_win[:, cols])
    out_ref[...] = jnp.concatenate(outs, axis=0).astype(BF16)


def _nsa(qt, kcmp, vcmpt, kslc, kwin, vt, gt, et, xs, xw, batch, seq):
    assert seq % SUPER_TILE == 0 and seq >= WIN_KEYS and seq // CMP_STRIDE <= 256 and seq // SLC_BLOCK <= HEAD_DIM
    nq = seq // Q_BLOCK
    n_cmp_rows = seq // CMP_STRIDE
    n_slc = seq // SLC_BLOCK
    n = batch * seq
    ovt = jnp.asarray(_overlap_t(n_cmp_rows, n_slc))
    gw = GQA * HEAD_DIM
    return pl.pallas_call(
        functools.partial(_nsa_kernel, n_cmp_rows=n_cmp_rows, n_slc=n_slc),
        grid=(batch, KV_HEADS, nq),
        in_specs=[
            pl.BlockSpec((gw, Q_BLOCK), lambda b, g, q: (g, b * nq + q)),
            pl.BlockSpec((None, n_cmp_rows, KV_WIDTH), lambda b, g, q: (b, 0, 0)),
            pl.BlockSpec((None, None, HEAD_DIM, n_cmp_rows), lambda b, g, q: (b, g, 0, 0)),
            pl.BlockSpec((None, seq, LANES), lambda b, g, q: (g, b, 0)),
            pl.BlockSpec((seq, LANES), lambda b, g, q: (b, 0)),
            pl.BlockSpec((None, nq, HEAD_DIM, LANES), lambda b, g, q: (g, b, 0, 0)),
            pl.BlockSpec((None, nq, HEAD_DIM, LANES), lambda b, g, q: (KV_HEADS + g, b, 0, 0)),
            pl.BlockSpec((GATE_ROWS, Q_BLOCK), lambda b, g, q: (0, b * nq + q)),
            pl.BlockSpec((GQA, ET_ROWS, Q_BLOCK), lambda b, g, q: (g, 0, 0)),
            pl.BlockSpec((None, XS_ROWS, GQA * LANES), lambda b, g, q: (g, 0, 0)),
            pl.BlockSpec((None, XW_ROWS, GQA * LANES), lambda b, g, q: (g, 0, 0)),
            pl.BlockSpec((n_slc, n_cmp_rows), lambda b, g, q: (0, 0)),
        ],
        out_specs=pl.BlockSpec((gw, Q_BLOCK), lambda b, g, q: (g, b * nq + q)),
        out_shape=jax.ShapeDtypeStruct((NSA_WIDTH, n), BF16),
        compiler_params=pltpu.CompilerParams(vmem_limit_bytes=VMEM_LIMIT),
        name="nsa",
    )(qt, kcmp, vcmpt, kslc, kwin, vt, vt, gt, et, xs, xw, ovt)


def _layer_norm_fm(h, g_ref, b_ref, reps):
    mu = jnp.mean(h, axis=0, keepdims=True)
    hc = h - mu
    var = jnp.mean(hc * hc, axis=0, keepdims=True)
    gain = jnp.concatenate([g_ref[...]] * reps, axis=1)
    bias = jnp.concatenate([b_ref[...]] * reps, axis=1)
    return hc * lax.rsqrt(var + LN_EPS) * gain + bias


def _mix_kernel(nsat_ref, zt_ref, halo_ref, x_ref, won_ref, wop_ref, pwt_ref, ps_ref, g_ref, b_ref,
                h_ref, hb_ref, *, tn, seq):
    t0 = (pl.program_id(0) * tn) % seq
    reps = tn // LANES
    zc = jnp.concatenate([halo_ref[...], zt_ref[...]], axis=1)
    t_ext = t0 - LANES + lax.broadcasted_iota(I32, (1, LANES + tn), 1)
    zc = jnp.where(t_ext >= 0, zc, 0.0)
    pooled = []
    for gi, win in enumerate(POOL_WINDOWS):
        zg = zc[gi * POOL_GROUP_DIM:(gi + 1) * POOL_GROUP_DIM]
        s = zg
        sh = 1
        while sh < win:
            s = s + pltpu.roll(s, sh, axis=1)
            sh *= 2
        cnt = jnp.clip(t_ext + 1, 1, win).astype(F32)
        y = (s / cnt - zg)[:, LANES:]
        scale = jnp.concatenate([ps_ref[gi * POOL_GROUP_DIM:(gi + 1) * POOL_GROUP_DIM]] * reps, axis=1)
        pooled.append((_dot(pwt_ref[gi], y.astype(BF16)) * scale).astype(BF16))
    pool_t = jnp.concatenate(pooled, axis=0)
    mix = _dot(won_ref[...], nsat_ref[...]) + _dot(wop_ref[...], pool_t)
    h = ALPHA * x_ref[...].T + mix
    h = _layer_norm_fm(h, g_ref, b_ref, reps)
    h_ref[...] = h
    hb_ref[...] = h.astype(BF16)


def _lane_bcast(v):
    return jnp.broadcast_to(v.reshape(-1, 1), (v.shape[0], LANES)).astype(F32)


def _mix(nsat, zt, x2, w_out, pool_w, pool_scale, ln_g, ln_b, seq):
    n = x2.shape[0]
    tn = 256
    wot = w_out.T.astype(BF16)
    full = lambda *s: pl.BlockSpec(s, lambda i: (0,) * len(s))
    return pl.pallas_call(
        functools.partial(_mix_kernel, tn=tn, seq=seq),
        grid=(n // tn,),
        in_specs=[
            pl.BlockSpec((NSA_WIDTH, tn), lambda i: (0, i)),
            pl.BlockSpec((POOL_WIDTH, tn), lambda i: (0, i)),
            pl.BlockSpec((POOL_WIDTH, LANES), lambda i: (0, jnp.maximum(i * (tn // LANES) - 1, 0))),
            pl.BlockSpec((tn, D_MODEL), lambda i: (i, 0)),
            full(D_MODEL, NSA_WIDTH), full(D_MODEL, POOL_WIDTH),
            full(len(POOL_WINDOWS), POOL_GROUP_DIM, POOL_GROUP_DIM),
            full(POOL_WIDTH, LANES), full(D_MODEL, LANES), full(D_MODEL, LANES),
        ],
        out_specs=[pl.BlockSpec((D_MODEL, tn), lambda i: (0, i))] * 2,
        out_shape=[jax.ShapeDtypeStruct((D_MODEL, n), F32), jax.ShapeDtypeStruct((D_MODEL, n), BF16)],
        compiler_params=pltpu.CompilerParams(vmem_limit_bytes=VMEM_LIMIT),
        name="mix",
    )(nsat, zt, zt, x2, wot[:, :NSA_WIDTH], wot[:, NSA_WIDTH:],
      jnp.swapaxes(pool_w, 1, 2).astype(BF16), _lane_bcast(pool_scale), _lane_bcast(ln_g), _lane_bcast(ln_b))


ROUTE_HEAD_GROUP = 4
_REMOVED = -(2.0 ** 100)
_PAD_SCORE = -3e38


def _top16_by_index(s):
    rows = lax.broadcasted_iota(I32, s.shape, 0).astype(F32)
    row16 = lax.broadcasted_iota(I32, (PEER_TOPK, s.shape[1]), 0)
    cur = s
    rank = jnp.full(s.shape, float(PEER_TOPK), F32)
    vals = jnp.zeros((PEER_TOPK, s.shape[1]), F32)
    for r in range(PEER_TOPK):
        mx = jnp.max(cur, axis=0, keepdims=True)
        first = jnp.min(jnp.where(cur == mx, rows, float(s.shape[0])), axis=0, keepdims=True)
        hit = rows == first
        vals = jnp.where(row16 == r, mx, vals)
        rank = jnp.where(hit, float(r), rank)
        cur = jnp.where(hit, -jnp.inf, cur)
    return vals, rank, jnp.full((1, s.shape[1]), float(PEER_TOPK), F32)


def _top16_by_value(scores):
    width = scores[0].shape[1]
    row16 = lax.broadcasted_iota(I32, (PEER_TOPK, width), 0)
    cur = list(scores)
    vals = [jnp.zeros((PEER_TOPK, width), F32) for _ in scores]
    for r in range(PEER_TOPK):
        mx = [jnp.max(c, axis=0, keepdims=True) for c in cur]
        vals = [jnp.where(row16 == r, m, v) for m, v in zip(mx, vals)]
        cur = [jnp.where(c == m, _REMOVED * (r + 1), c) for c, m in zip(cur, mx)]
    out = []
    for c, v in zip(cur, vals):
        member = c <= _REMOVED
        rank = jnp.where(member, c * (1.0 / _REMOVED) - 1.0, float(PEER_TOPK))
        out.append((v, rank, jnp.sum(jnp.where(member, 1.0, 0.0), axis=0, keepdims=True)))
    return out


def _candidate_table(v1, v2):
    tn = v1.shape[1]
    row8 = lax.broadcasted_iota(I32, (SUBLANES, tn), 0)
    groups = [v1[0:1, :] + v2]
    real = [PEER_TOPK]
    for a in range(1, SUBLANES):
        nb = PEER_TOPK // (a + 1)
        groups.append(jnp.where(row8 < nb, v1[a:a + 1, :] + v2[0:SUBLANES, :], _PAD_SCORE))
        real.append(nb)
    groups.append(v1[SUBLANES:, :] + v2[0:1, :])
    real.append(PEER_TOPK - SUBLANES)
    return jnp.concatenate(groups, axis=0), real


def _peer_scores(h, hb_ref, wqt_ref, sk1_ref, sk2_ref):
    half = PEER_QDIM // 2
    q = _dot(wqt_ref[h * PEER_QDIM:(h + 1) * PEER_QDIM, :], hb_ref[...])
    return _dot(sk1_ref[...], q[:half].astype(BF16)), _dot(sk2_ref[...], q[half:].astype(BF16))


def _route_heads(heads, scores, outs, *, tn, by_value):
    n_ref, a_ref, r2_ref, b_ref = outs
    want = float(PEER_TOPK)
    miscounts = []
    per_head = []
    for h, (s1, s2) in zip(heads, scores):
        if by_value:
            (v1, rank1, count1), (v2, rank2, count2) = _top16_by_value([s1, s2])
        else:
            (v1, rank1, count1), (v2, rank2, count2) = _top16_by_index(s1), _top16_by_index(s2)
        miscounts.append(jnp.maximum(jnp.abs(count1 - want), jnp.abs(count2 - want)))
        cs, real = _candidate_table(v1, v2)
        per_head.append((h, s1, s2, v1, v2, rank1, rank2, cs))

    if by_value:
        cur = [ph[-1] for ph in per_head]
        for _ in range(PEER_TOPK):
            cur = [jnp.where(c == jnp.max(c, axis=0, keepdims=True), -jnp.inf, c) for c in cur]
        sels = [c == -jnp.inf for c in cur]
    else:
        sels = []
        for ph in per_head:
            cs = ph[-1]
            c_i = lax.broadcasted_iota(I32, cs.shape, 0)
            cnt = jnp.zeros(cs.shape, F32)
            base = 0
            for g, n_real in enumerate(real):
                for c in range(base, base + n_real):
                    row = cs[c:c + 1, :]
                    beats = (row > cs) | ((row == cs) & (c_i > c))
                    cnt = cnt + jnp.where(beats, 1.0, 0.0)
                base += PEER_TOPK if g == 0 else SUBLANES
            sels.append(cnt < want)

    for i, ((h, s1, s2, v1, v2, rank1, rank2, cs), sel) in enumerate(zip(per_head, sels)):
        self_f = jnp.where(sel, 1.0, 0.0)
        miscounts[i] = jnp.maximum(miscounts[i], jnp.abs(jnp.sum(self_f, axis=0, keepdims=True) - want))
        top = v1[0:1, :] + v2[0:1, :]
        z = jnp.sum(jnp.where(sel, jnp.exp(cs - top), 0.0), axis=0, keepdims=True)
        nmap = jnp.zeros((PEER_KEYS, tn), F32)
        base = 0
        for a in range(PEER_TOPK):
            if a < SUBLANES:
                size = PEER_TOPK if a == 0 else SUBLANES
                n_a = jnp.sum(self_f[base:base + size, :], axis=0, keepdims=True)
                base += size
            else:
                n_a = self_f[base + a - SUBLANES:base + a - SUBLANES + 1, :]
            nmap = jnp.where(rank1 == float(a), n_a, nmap)
        n_ref[h] = nmap
        a_ref[h] = jnp.where(rank1 < want, jnp.exp(s1 - v1[0:1, :]), 0.0)
        r2_ref[h] = rank2.astype(BF16)
        b_ref[h] = (jnp.where(rank2 < want, jnp.exp(s2 - v2[0:1, :]), 0.0) / z).astype(BF16)
    return miscounts


def _route_kernel(hb_ref, wqt_ref, sk1_ref, sk2_ref, n_ref, a_ref, r2_ref, b_ref, *, tn):
    scores = functools.partial(_peer_scores, hb_ref=hb_ref, wqt_ref=wqt_ref, sk1_ref=sk1_ref, sk2_ref=sk2_ref)
    outs = (n_ref, a_ref, r2_ref, b_ref)
    groups = [range(h0, h0 + ROUTE_HEAD_GROUP) for h0 in range(0, PEER_HEADS, ROUTE_HEAD_GROUP)]
    miscounts = []
    ready = [scores(h) for h in groups[0]]
    for gi, heads in enumerate(groups):
        ahead = [scores(h) for h in groups[gi + 1]] if gi + 1 < len(groups) else None
        miscounts += _route_heads(heads, ready, outs, tn=tn, by_value=True)
        ready = ahead

    any_miscount = functools.reduce(jnp.maximum, miscounts)

    @pl.when(jnp.max(any_miscount) > 0.0)
    def _():
        for h in range(PEER_HEADS):
            @pl.when(jnp.max(miscounts[h]) > 0.0)
            def _(h=h):
                _route_heads([h], [scores(h)], outs, tn=tn, by_value=False)


def _route(hb, wq, subkeys):
    n = hb.shape[1]
    tn = LANES
    full = lambda *s: pl.BlockSpec(s, lambda i: (0,) * len(s))
    tab = lambda: pl.BlockSpec((PEER_HEADS, PEER_KEYS, tn), lambda i: (0, 0, i))
    shape = (PEER_HEADS, PEER_KEYS, n)
    return pl.pallas_call(
        functools.partial(_route_kernel, tn=tn),
        grid=(n // tn,),
        in_specs=[
            pl.BlockSpec((D_MODEL, tn), lambda i: (0, i)),
            full(PEER_HEADS * PEER_QDIM, D_MODEL), full(PEER_KEYS, PEER_QDIM // 2), full(PEER_KEYS, PEER_QDIM // 2),
        ],
        out_specs=[tab(), tab(), tab(), tab()],
        out_shape=[
            jax.ShapeDtypeStruct(shape, F32),
            jax.ShapeDtypeStruct(shape, F32),
            jax.ShapeDtypeStruct(shape, BF16),
            jax.ShapeDtypeStruct(shape, BF16),
        ],
        compiler_params=pltpu.CompilerParams(vmem_limit_bytes=VMEM_LIMIT),
        name="peer_route",
    )(hb, wq.T.astype(BF16), subkeys[0].astype(BF16), subkeys[1].astype(BF16))


def _experts_kernel(hb_ref, u_ref, vt_ref, n_ref, a_ref, r2_ref, b_ref, h_ref, g_ref, bb_ref,
                    out_ref, acc_ref, rows_ref, *, tn, te):
    j = pl.program_id(1)
    keys_per_step = te // PEER_KEYS

    def stage_rows(tile, slot):
        for c in range(keys_per_step):
            for h in range(PEER_HEADS):
                for k, ref in enumerate((n_ref, a_ref)):
                    row = ref[h, pl.ds(tile * keys_per_step + c, 1), :]
                    rows_ref[slot, k, c, h] = jnp.broadcast_to(row, (BF16_ROWS, tn)).astype(BF16)

    @pl.when(j == 0)
    def _():
        acc_ref[...] = jnp.zeros_like(acc_ref)
        stage_rows(0, 0)

    slot = j & 1
    hb = hb_ref[...]
    part = te // EXPERT_PARTS
    acts = [_dot(u_ref[i * part:(i + 1) * part, :], hb) for i in range(EXPERT_PARTS)]
    chunks = part // PEER_KEYS
    tiles = PEER_KEYS // BF16_ROWS
    zero = jnp.zeros((tiles, BF16_ROWS, tn), BF16)

    for i in range(EXPERT_PARTS):
        act = _gelu(acts[i])
        ps = []
        for c in range(chunks):
            key = i * chunks + c
            w = zero
            for h in range(PEER_HEADS):
                picked = jnp.where(r2_ref[h] < rows_ref[slot, 0, key, h][None], b_ref[h], zero)
                w = w + rows_ref[slot, 1, key, h][None] * picked
            ps.append(w.reshape(PEER_KEYS, tn) * act[c * PEER_KEYS:(c + 1) * PEER_KEYS].astype(BF16))
        acc_ref[...] += _dot(vt_ref[:, i * part:(i + 1) * part], jnp.concatenate(ps, axis=0))

    stage_rows(jnp.minimum(j + 1, pl.num_programs(1) - 1), 1 - slot)

    @pl.when(j == pl.num_programs(1) - 1)
    def _():
        h = ALPHA * h_ref[...] + acc_ref[...]
        out_ref[...] = _layer_norm_fm(h, g_ref, bb_ref, tn // LANES).T


def _experts(hb, hf, u_b, vt_b, tabs, ln_g, ln_b):
    n = hb.shape[1]
    tn, te = 512, 1024
    tab = lambda: pl.BlockSpec((PEER_HEADS, PEER_KEYS, tn), lambda i, j: (0, 0, i))
    tiles = PEER_KEYS // BF16_ROWS
    tiled = lambda: pl.BlockSpec((PEER_HEADS, tiles, BF16_ROWS, tn), lambda i, j: (0, 0, 0, i))
    n_tab, a_tab, r2_tab, b_tab = tabs
    tabs = (n_tab, a_tab, r2_tab.reshape(PEER_HEADS, tiles, BF16_ROWS, n), b_tab.reshape(PEER_HEADS, tiles, BF16_ROWS, n))
    return pl.pallas_call(
        functools.partial(_experts_kernel, tn=tn, te=te),
        grid=(n // tn, PEER_EXPERTS // te),
        in_specs=[
            pl.BlockSpec((D_MODEL, tn), lambda i, j: (0, i)),
            pl.BlockSpec((te, D_MODEL), lambda i, j: (j, 0)),
            pl.BlockSpec((D_MODEL, te), lambda i, j: (0, j)),
            tab(), tab(), tiled(), tiled(),
            pl.BlockSpec((D_MODEL, tn), lambda i, j: (0, i)),
            pl.BlockSpec((D_MODEL, LANES), lambda i, j: (0, 0)),
            pl.BlockSpec((D_MODEL, LANES), lambda i, j: (0, 0)),
        ],
        out_specs=pl.BlockSpec((tn, D_MODEL), lambda i, j: (i, 0)),
        out_shape=jax.ShapeDtypeStruct((n, D_MODEL), F32),
        scratch_shapes=[pltpu.VMEM((D_MODEL, tn), F32),
                        pltpu.VMEM((2, 2, te // PEER_KEYS, PEER_HEADS, BF16_ROWS, tn), BF16)],
        compiler_params=pltpu.CompilerParams(
            dimension_semantics=("arbitrary", "arbitrary"), vmem_limit_bytes=VMEM_LIMIT),
        name="peer_experts",
    )(hb, u_b, vt_b, *tabs, hf, _lane_bcast(ln_g), _lane_bcast(ln_b))


def kernel(x, w_in, rel_table, cmp_pe_k, cmp_pe_v, cmp_k_w1, cmp_k_w2, cmp_v_w1, cmp_v_w2, pool_w, pool_scale,
           w_out, ln1_g, ln1_b, peer_wq, peer_subkeys, peer_u, peer_v, ln2_g, ln2_b):
    batch, seq, _ = x.shape
    assert w_in.shape[0] == DEPTH == 1 and seq % 512 == 0 and seq // CMP_STRIDE <= 256
    x2 = x.reshape(batch * seq, D_MODEL)
    et, xs, xw = _bias_tables(rel_table)
    qt, vt, gt, zt, kvc, kslc, kwin = _inproj(x2, w_in[0], seq)
    kcmp, vcmpt = _compress(kvc, batch, seq, cmp_pe_k[0], cmp_pe_v[0], cmp_k_w1[0], cmp_k_w2[0],
                            cmp_v_w1[0], cmp_v_w2[0])
    nsat = _nsa(qt, kcmp, vcmpt, kslc, kwin, vt, gt, et, xs, xw, batch, seq)
    hf, hb = _mix(nsat, zt, x2, w_out[0], pool_w[0], pool_scale[0], ln1_g[0], ln1_b[0], seq)
    tabs = _route(hb, peer_wq[0], peer_subkeys[0])
    y = _experts(hb, hf, peer_u[0].astype(BF16), peer_v[0].T.astype(BF16), tabs, ln2_g[0], ln2_b[0])
    return y.reshape(batch, seq, D_MODEL)
```

```python
import functools
import math

import numpy as np
import jax
import jax.numpy as jnp
from jax import lax
from jax.experimental import pallas as pl
from jax.experimental.pallas import tpu as pltpu

F32 = jnp.float32
BF16 = jnp.bfloat16
I32 = jnp.int32

D_MODEL = 1024
NSA_HEADS = 8
KV_HEADS = 2
GQA = NSA_HEADS // KV_HEADS
HEAD_DIM = 64
NSA_WIDTH = NSA_HEADS * HEAD_DIM
KV_WIDTH = KV_HEADS * HEAD_DIM
POOL_WIDTH = D_MODEL - NSA_WIDTH
POOL_WINDOWS = (2, 4, 8, 16)
POOL_GROUP_DIM = POOL_WIDTH // len(POOL_WINDOWS)
N_GATES = 3 * NSA_HEADS
GATE_ROWS = 32
CMP_BLOCK = 32
CMP_STRIDE = 16
CMP_HIDDEN = 256
SLC_BLOCK = 64
SLC_COUNT = 16
WINDOW = 512
Q_BLOCK = 128
NEG = -1e30
FORCE = 1e6
N_BUCKETS = 32
MAX_DISTANCE = 128
PEER_HEADS = 8
PEER_KEYS = 128
PEER_EXPERTS = PEER_KEYS * PEER_KEYS
PEER_QDIM = 256
PEER_TOPK = 16
DEPTH = 1
ALPHA = (2 * DEPTH) ** 0.25
LN_EPS = 1e-5
SQRT_HALF = 0.7071067811865476

LANES = 128
SUBLANES = 8
BF16_ROWS = SUBLANES
EXPERT_PARTS = 2
VMEM_LIMIT = 52 * 1024 * 1024

_FM_Q, _FM_V, _FM_G, _FM_Z, _FM_END = 0, 512, 768, 800, 1312
_TM_WIDTH = 640


def _gelu(x):
    return 0.5 * x * (1.0 + lax.erf(x * SQRT_HALF))


def _dot(a, b):
    return jnp.dot(a, b, preferred_element_type=F32)


def _dot_nt(a, b):
    return lax.dot_general(a, b, (((1,), (1,)), ((), ())), preferred_element_type=F32)


def _bucket_np(dist):
    dist = np.maximum(dist, 0)
    max_exact = N_BUCKETS // 2
    large = max_exact + np.floor(
        np.log(np.maximum(dist, 1) / max_exact) / math.log(MAX_DISTANCE / max_exact)
        * (N_BUCKETS - max_exact)).astype(np.int64)
    large = np.minimum(large, N_BUCKETS - 1)
    return np.where(dist < max_exact, dist, large)


ET_ROWS = 512
XS_ROWS = 896
XS_MASKED_ROW = 640
XW_ROWS = 1152
_CODE_MASKED = N_BUCKETS


def _bias_index_tables():
    qi = np.arange(Q_BLOCK)[None, :]

    def codes(dist, visible):
        return np.where(visible, _bucket_np(dist), _CODE_MASKED).astype(np.int32)

    d_cmp = qi - CMP_STRIDE * (np.arange(ET_ROWS)[:, None] - 256) - (CMP_BLOCK - 1)
    rho = np.arange(XS_ROWS)[:, None]
    d_sel = qi - (rho - 384)
    d_win = qi - (np.arange(XW_ROWS)[:, None] - 512)
    return np.concatenate([
        codes(d_cmp, d_cmp >= 0),
        codes(d_sel, (d_sel >= 0) & (rho < XS_MASKED_ROW)),
        codes(d_win, (d_win >= 0) & (d_win < WINDOW)),
    ], axis=0)


def _overlap_t(n_cmp_rows, n_slc):
    n = np.arange(n_cmp_rows)[None, :]
    j = np.arange(n_slc)[:, None]
    start = n * CMP_STRIDE
    end = start + CMP_BLOCK - 1
    return ((start <= j * SLC_BLOCK + SLC_BLOCK - 1) & (end >= j * SLC_BLOCK)).astype(np.float32)


def _bias_kernel(tab_ref, idx_ref, et_ref, xs_ref, xw_ref):
    g = pl.program_id(0)
    chunk = Q_BLOCK
    for r in range(GQA):
        h = g * GQA + r
        far = tab_ref[N_BUCKETS - 1, h]
        cols = slice(r * LANES, (r + 1) * LANES)
        for c in range((ET_ROWS + XS_ROWS + XW_ROWS) // chunk):
            idx = idx_ref[c * chunk:(c + 1) * chunk, :]
            acc = jnp.where(idx == _CODE_MASKED, NEG, 0.0)
            for b in range(N_BUCKETS - 1):
                acc = jnp.where(idx == b, tab_ref[b, h] - far, acc)
            row = c * chunk
            if row < ET_ROWS:
                et_ref[r, row:row + chunk, :] = acc
            elif row < ET_ROWS + XS_ROWS:
                xs_ref[0, row - ET_ROWS:row - ET_ROWS + chunk, cols] = acc
            else:
                row -= ET_ROWS + XS_ROWS
                xw_ref[0, row:row + chunk, cols] = acc


def _bias_tables(rel_table):
    idx = _bias_index_tables()
    qw = GQA * LANES
    return pl.pallas_call(
        _bias_kernel,
        grid=(KV_HEADS,),
        in_specs=[
            pl.BlockSpec(memory_space=pltpu.SMEM),
            pl.BlockSpec(idx.shape, lambda g: (0, 0)),
        ],
        out_specs=[
            pl.BlockSpec((GQA, ET_ROWS, Q_BLOCK), lambda g: (g, 0, 0)),
            pl.BlockSpec((1, XS_ROWS, qw), lambda g: (g, 0, 0)),
            pl.BlockSpec((1, XW_ROWS, qw), lambda g: (g, 0, 0)),
        ],
        out_shape=[
            jax.ShapeDtypeStruct((NSA_HEADS, ET_ROWS, Q_BLOCK), F32),
            jax.ShapeDtypeStruct((KV_HEADS, XS_ROWS, qw), F32),
            jax.ShapeDtypeStruct((KV_HEADS, XW_ROWS, qw), F32),
        ],
        name="bias_tables",
    )(rel_table, jnp.asarray(idx))


def _inproj_kernel(x_ref, wfm_ref, wtm_ref, qt_ref, vt_ref, gt_ref, zt_ref, kvc_ref,
                   kslc_ref, kwin_ref, *, tn, seq):
    xb = x_ref[...].astype(BF16)
    fm = _dot_nt(wfm_ref[...], xb)
    tm = _dot(xb, wtm_ref[...])
    qt_ref[...] = (fm[_FM_Q:_FM_V] * (HEAD_DIM ** -0.5)).astype(BF16)
    for c in range(2 * KV_HEADS):
        for s in range(tn // LANES):
            vt_ref[c, s] = fm[_FM_V + c * HEAD_DIM:_FM_V + (c + 1) * HEAD_DIM,
                              s * LANES:(s + 1) * LANES].astype(BF16)
    gt_ref[...] = jax.nn.sigmoid(fm[_FM_G:_FM_Z])
    zt_ref[...] = fm[_FM_Z:_FM_END]
    kvc_ref[0] = tm[:, 0:KV_WIDTH]
    kvc_ref[1] = tm[:, KV_WIDTH:2 * KV_WIDTH]
    t0 = (pl.program_id(0) * tn) % seq
    row = lax.broadcasted_iota(I32, (tn, LANES), 0) + t0
    lane = lax.broadcasted_iota(I32, (tn, LANES), 1)
    onehot = jnp.where(lane - HEAD_DIM == (row >> 6), 1.0, 0.0)
    for g in range(KV_HEADS):
        k = tm[:, 256 + g * LANES:256 + (g + 1) * LANES]
        kslc_ref[g] = jnp.where(lane < HEAD_DIM, k, onehot).astype(BF16)
    kwin_ref[...] = tm[:, 512:640].astype(BF16)


def _inproj(x2, w_in, seq):
    n = x2.shape[0]
    tn = 512
    w = w_in
    o = NSA_WIDTH
    w_q = w[:, 0:o]
    w_kc, w_vc, w_ks, w_vs, w_kw, w_vw = [w[:, o + i * KV_WIDTH:o + (i + 1) * KV_WIDTH] for i in range(6)]
    o += 6 * KV_WIDTH
    w_g = w[:, o:o + N_GATES]
    w_z = w[:, o + N_GATES:]
    zpad = jnp.zeros((D_MODEL, HEAD_DIM), w.dtype)
    wfm = jnp.concatenate(
        [w_q, w_vs, w_vw, w_g, jnp.zeros((D_MODEL, GATE_ROWS - N_GATES), w.dtype), w_z], axis=1).T.astype(BF16)
    wtm = jnp.concatenate(
        [w_kc, w_vc, w_ks[:, :HEAD_DIM], zpad, w_ks[:, HEAD_DIM:], zpad, w_kw], axis=1).astype(BF16)
    nt = n // LANES
    return pl.pallas_call(
        functools.partial(_inproj_kernel, tn=tn, seq=seq),
        grid=(n // tn,),
        in_specs=[
            pl.BlockSpec((tn, D_MODEL), lambda i: (i, 0)),
            pl.BlockSpec((_FM_END, D_MODEL), lambda i: (0, 0)),
            pl.BlockSpec((D_MODEL, _TM_WIDTH), lambda i: (0, 0)),
        ],
        out_specs=[
            pl.BlockSpec((NSA_WIDTH, tn), lambda i: (0, i)),
            pl.BlockSpec((2 * KV_HEADS, tn // LANES, HEAD_DIM, LANES), lambda i: (0, i, 0, 0)),
            pl.BlockSpec((GATE_ROWS, tn), lambda i: (0, i)),
            pl.BlockSpec((POOL_WIDTH, tn), lambda i: (0, i)),
            pl.BlockSpec((2, tn, KV_WIDTH), lambda i: (0, i, 0)),
            pl.BlockSpec((KV_HEADS, tn, LANES), lambda i: (0, i, 0)),
            pl.BlockSpec((tn, LANES), lambda i: (i, 0)),
        ],
        out_shape=[
            jax.ShapeDtypeStruct((NSA_WIDTH, n), BF16),
            jax.ShapeDtypeStruct((2 * KV_HEADS, nt, HEAD_DIM, LANES), BF16),
            jax.ShapeDtypeStruct((GATE_ROWS, n), F32),
            jax.ShapeDtypeStruct((POOL_WIDTH, n), F32),
            jax.ShapeDtypeStruct((2, n, KV_WIDTH), F32),
            jax.ShapeDtypeStruct((KV_HEADS, n, LANES), BF16),
            jax.ShapeDtypeStruct((n, LANES), BF16),
        ],
        compiler_params=pltpu.CompilerParams(vmem_limit_bytes=VMEM_LIMIT),
        name="inproj",
    )(x2, wfm, wtm)


def _compress_kernel(kc_ref, vc_ref, w2k_ref, w2v_ref, pek_ref, pev_ref, w1k_ref, w1v_ref,
                     wok0_ref, wok1_ref, wovt_ref, kcmp_ref, vcmpt_ref, *, n_rows):
    half = CMP_BLOCK // 2
    acc_k = jnp.zeros((n_rows, 4 * CMP_HIDDEN), F32)
    acc_v = jnp.zeros((n_rows, 4 * CMP_HIDDEN), F32)
    for l in range(half):
        rows = pl.ds(l, n_rows, stride=CMP_STRIDE)
        acc_k = acc_k + _dot(kc_ref[rows, :].astype(BF16), w2k_ref[l])
        acc_v = acc_v + _dot(vc_ref[rows, :].astype(BF16), w2v_ref[l])
    ck = _dot(pek_ref[...], w1k_ref[...])[0:1]
    cv = _dot(pev_ref[...], w1v_ref[...])[0:1]

    def hidden(acc, c, g):
        a = acc[:, g * CMP_HIDDEN:(g + 1) * CMP_HIDDEN]
        b = acc[:, (2 + g) * CMP_HIDDEN:(3 + g) * CMP_HIDDEN]
        return _gelu(a + pltpu.roll(b, n_rows - 1, axis=0) + c).astype(BF16)

    kcmp_ref[...] = (_dot(hidden(acc_k, ck, 0), wok0_ref[...])
                     + _dot(hidden(acc_k, ck, 1), wok1_ref[...])).astype(BF16)
    for g in range(KV_HEADS):
        vcmpt_ref[g] = _dot_nt(wovt_ref[...], hidden(acc_v, cv, g)).astype(BF16)


def _compress(kvc, batch, seq, pe_k, pe_v, k_w1, k_w2, v_w1, v_w2):
    n_rows = seq // CMP_STRIDE
    half = CMP_BLOCK // 2

    def blockdiag(w1):
        w = w1.reshape(CMP_BLOCK, HEAD_DIM, CMP_HIDDEN)
        z = jnp.zeros((half, HEAD_DIM, CMP_HIDDEN), w1.dtype)
        top = jnp.concatenate([w[:half], z, w[half:], z], axis=2)
        bot = jnp.concatenate([z, w[:half], z, w[half:]], axis=2)
        return jnp.concatenate([top, bot], axis=1).astype(BF16)

    def pe_rows(pe):
        flat = pe.reshape(1, CMP_BLOCK * HEAD_DIM)
        return jnp.concatenate([flat, jnp.zeros((7, flat.shape[1]), pe.dtype)], axis=0).astype(BF16)

    zo = jnp.zeros((CMP_HIDDEN, HEAD_DIM), k_w2.dtype)
    wok0 = jnp.concatenate([k_w2, zo], axis=1).astype(BF16)
    wok1 = jnp.concatenate([zo, k_w2], axis=1).astype(BF16)
    full = lambda *s: pl.BlockSpec(s, lambda b: (0,) * len(s))
    return pl.pallas_call(
        functools.partial(_compress_kernel, n_rows=n_rows),
        grid=(batch,),
        in_specs=[
            pl.BlockSpec((None, seq, KV_WIDTH), lambda b: (0, b, 0)),
            pl.BlockSpec((None, seq, KV_WIDTH), lambda b: (1, b, 0)),
            full(half, KV_WIDTH, 4 * CMP_HIDDEN), full(half, KV_WIDTH, 4 * CMP_HIDDEN),
            full(8, CMP_BLOCK * HEAD_DIM), full(8, CMP_BLOCK * HEAD_DIM),
            full(CMP_BLOCK * HEAD_DIM, CMP_HIDDEN), full(CMP_BLOCK * HEAD_DIM, CMP_HIDDEN),
            full(CMP_HIDDEN, KV_WIDTH), full(CMP_HIDDEN, KV_WIDTH), full(HEAD_DIM, CMP_HIDDEN),
        ],
        out_specs=[
            pl.BlockSpec((None, n_rows, KV_WIDTH), lambda b: (b, 0, 0)),
            pl.BlockSpec((None, KV_HEADS, HEAD_DIM, n_rows), lambda b: (b, 0, 0, 0)),
        ],
        out_shape=[
            jax.ShapeDtypeStruct((batch, n_rows, KV_WIDTH), BF16),
            jax.ShapeDtypeStruct((batch, KV_HEADS, HEAD_DIM, n_rows), BF16),
        ],
        compiler_params=pltpu.CompilerParams(vmem_limit_bytes=VMEM_LIMIT),
        name="compress",
    )(kvc, kvc, blockdiag(k_w1), blockdiag(v_w1), pe_rows(pe_k), pe_rows(pe_v),
      k_w1.astype(BF16), v_w1.astype(BF16), wok0, wok1, v_w2.T.astype(BF16))


SUPER_TILE = 2 * Q_BLOCK
WIN_KEYS = WINDOW + Q_BLOCK


def _softmax_init(s, vt):
    m = jnp.max(s, axis=0, keepdims=True)
    p = jnp.exp(s - m)
    return m, jnp.sum(p, axis=0, keepdims=True), _dot(vt, p.astype(BF16))


def _softmax_step(s, vt, state):
    m, l, acc = state
    m_new = jnp.maximum(m, jnp.max(s, axis=0, keepdims=True))
    alpha = jnp.exp(m - m_new)
    p = jnp.exp(s - m_new)
    return m_new, alpha * l + jnp.sum(p, axis=0, keepdims=True), alpha * acc + _dot(vt, p.astype(BF16))


def _softmax_merge(a, b):
    m = jnp.maximum(a[0], b[0])
    wa = jnp.exp(a[0] - m)
    wb = jnp.exp(b[0] - m)
    return (wa * a[2] + wb * b[2]) * (1.0 / (wa * a[1] + wb * b[1]))


def _nsa_kernel(qt_ref, kcmp_ref, vcmpt_ref, kslc_ref, kwin_ref, vslct_ref, vwint_ref, gt_ref,
                et_ref, xs_ref, xw_ref, ovt_ref, out_ref, *, n_cmp_rows, n_slc):
    qb = pl.program_id(1)
    qw = GQA * Q_BLOCK
    odd = qb & 1
    st_diag = qb >> 1
    st_prev = jnp.maximum(st_diag - 1, 0)
    n_far = st_prev
    row_a = pl.multiple_of(384 - Q_BLOCK * odd, Q_BLOCK)
    row_b = pl.multiple_of(jnp.where(st_diag >= 1, Q_BLOCK - Q_BLOCK * odd, XS_MASKED_ROW), Q_BLOCK)
    k0 = jnp.maximum(qb * Q_BLOCK - WINDOW, 0)
    row_w = pl.multiple_of(WINDOW - qb * Q_BLOCK + k0, Q_BLOCK)
    kt0 = k0 >> 7
    off = pl.multiple_of(256 - 8 * qb, 8)
    j_i = lax.broadcasted_iota(I32, (n_slc, Q_BLOCK), 0)
    cur = (qb * Q_BLOCK + lax.broadcasted_iota(I32, (n_slc, Q_BLOCK), 1)) >> 6
    forced = (j_i == 0) | (j_i == cur) | (j_i == cur - 1)

    def sel_scores(g, qsel, st):
        k = kslc_ref[g, pl.ds(pl.multiple_of(st * SUPER_TILE, SUPER_TILE), SUPER_TILE), :]
        vt = jnp.concatenate([vslct_ref[g, 2 * st], vslct_ref[g, 2 * st + 1]], axis=1)
        return _dot(k, qsel), vt

    def near(g):
        q4 = qt_ref[g * GQA * HEAD_DIM:(g + 1) * GQA * HEAD_DIM, :]
        qcat = jnp.concatenate([q4[r * HEAD_DIM:(r + 1) * HEAD_DIM] for r in range(GQA)], axis=1)
        zq = jnp.zeros_like(qcat)
        qg = jnp.concatenate([qcat, zq] if g == 0 else [zq, qcat], axis=0)

        s = _dot(kcmp_ref[...], qg)
        s = s + jnp.concatenate([et_ref[g * GQA + r, pl.ds(off, n_cmp_rows), :] for r in range(GQA)], axis=1)
        m = jnp.max(s, axis=0, keepdims=True)
        p = jnp.exp(s - m)
        l = jnp.sum(p, axis=0, keepdims=True)
        p = p * jnp.where(m > 0.5 * NEG, 1.0 / l, 0.0)
        o_cmp = _dot(vcmpt_ref[g], p.astype(BF16))
        p_sum = p[:, 0:Q_BLOCK]
        for r in range(1, GQA):
            p_sum = p_sum + p[:, r * Q_BLOCK:(r + 1) * Q_BLOCK]
        imp = jnp.dot(ovt_ref[...], p_sum, preferred_element_type=F32,
                      precision=lax.Precision.HIGHEST)

        s_win = _dot(kwin_ref[pl.ds(pl.multiple_of(k0, Q_BLOCK), WIN_KEYS), :], qg)
        vt = jnp.concatenate([vwint_ref[g, kt0 + i] for i in range(WIN_KEYS // Q_BLOCK)], axis=1)
        m, l, acc = _softmax_init(s_win + xw_ref[g, pl.ds(row_w, WIN_KEYS), :], vt)
        o_win = acc * (1.0 / l)

        imp = jnp.where(forced, FORCE, imp)
        imp = jnp.where(j_i <= cur, imp, NEG)
        cnt = jnp.zeros((n_slc, Q_BLOCK), F32)
        for jp in range(n_slc):
            row = imp[jp:jp + 1, :]
            beats = (row > imp) | ((row == imp) & (j_i > jp))
            cnt = cnt + jnp.where(beats, 1.0, 0.0)
        selneg = jnp.where(cnt < float(SLC_COUNT), 0.0, NEG).astype(BF16)
        qsel = [qcat, jnp.concatenate([selneg] * GQA, axis=1)]
        if n_slc < LANES - HEAD_DIM:
            qsel.append(jnp.zeros((LANES - HEAD_DIM - n_slc, qw), BF16))
        qsel = jnp.concatenate(qsel, axis=0)

        s_a, vt_a = sel_scores(g, qsel, st_diag)
        s_b, vt_b = sel_scores(g, qsel, st_prev)
        chain_a = _softmax_init(s_a + xs_ref[g, pl.ds(row_a, SUPER_TILE), :], vt_a)
        chain_b = _softmax_init(s_b + xs_ref[g, pl.ds(row_b, SUPER_TILE), :], vt_b)
        return o_cmp, o_win, qsel, (chain_a, chain_b)

    heads = [near(g) for g in range(KV_HEADS)]
    qsels = [h[2] for h in heads]

    def far_steps(first, count, chains):
        scored = [[sel_scores(g, qsels[g], first + k) for k in range(count)] for g in range(KV_HEADS)]
        chains = [list(c) for c in chains]
        for k in range(count):
            for g in range(KV_HEADS):
                s, vt = scored[g][k]
                chains[g][k % 2] = _softmax_step(s, vt, chains[g][k % 2])
        return tuple(tuple(c) for c in chains)

    chains = tuple(h[3] for h in heads)
    chains = lax.fori_loop(0, n_far >> 2, lambda i, c: far_steps(4 * i, 4, c), chains)
    rest = n_far & ~3
    chains = lax.cond((n_far & 2) != 0, lambda c: far_steps(rest, 2, c), lambda c: c, chains)
    chains = lax.cond((n_far & 1) != 0, lambda c: far_steps(n_far - 1, 1, c), lambda c: c, chains)

    outs = []
    for g in range(KV_HEADS):
        o_cmp, o_win = heads[g][0], heads[g][1]
        o_slc = _softmax_merge(*chains[g])
        for r in range(GQA):
            cols = slice(r * Q_BLOCK, (r + 1) * Q_BLOCK)
            gate = [gt_ref[c * NSA_HEADS + g * GQA + r:c * NSA_HEADS + g * GQA + r + 1, :] for c in range(3)]
            outs.append(gate[0] * o_cmp[:, cols] + gate[1] * o_slc[:, cols] + gate[2] * o_win[:, cols])
    out_ref[...] = jnp.concatenate(outs, axis=0).astype(BF16)


def _nsa(qt, kcmp, vcmpt, kslc, kwin, vt, gt, et, xs, xw, batch, seq):
    assert seq % SUPER_TILE == 0 and seq >= WIN_KEYS and seq // CMP_STRIDE <= 256 and seq // SLC_BLOCK <= HEAD_DIM
    nq = seq // Q_BLOCK
    n_cmp_rows = seq // CMP_STRIDE
    n_slc = seq // SLC_BLOCK
    n = batch * seq
    ovt = jnp.asarray(_overlap_t(n_cmp_rows, n_slc))
    qw = GQA * LANES
    return pl.pallas_call(
        functools.partial(_nsa_kernel, n_cmp_rows=n_cmp_rows, n_slc=n_slc),
        grid=(batch, nq),
        in_specs=[
            pl.BlockSpec((NSA_WIDTH, Q_BLOCK), lambda b, q: (0, b * nq + q)),
            pl.BlockSpec((None, n_cmp_rows, KV_WIDTH), lambda b, q: (b, 0, 0)),
            pl.BlockSpec((None, KV_HEADS, HEAD_DIM, n_cmp_rows), lambda b, q: (b, 0, 0, 0)),
            pl.BlockSpec((KV_HEADS, seq, LANES), lambda b, q: (0, b, 0)),
            pl.BlockSpec((seq, LANES), lambda b, q: (b, 0)),
            pl.BlockSpec((KV_HEADS, nq, HEAD_DIM, LANES), lambda b, q: (0, b, 0, 0)),
            pl.BlockSpec((KV_HEADS, nq, HEAD_DIM, LANES), lambda b, q: (1, b, 0, 0)),
            pl.BlockSpec((GATE_ROWS, Q_BLOCK), lambda b, q: (0, b * nq + q)),
            pl.BlockSpec((NSA_HEADS, ET_ROWS, Q_BLOCK), lambda b, q: (0, 0, 0)),
            pl.BlockSpec((KV_HEADS, XS_ROWS, qw), lambda b, q: (0, 0, 0)),
            pl.BlockSpec((KV_HEADS, XW_ROWS, qw), lambda b, q: (0, 0, 0)),
            pl.BlockSpec((n_slc, n_cmp_rows), lambda b, q: (0, 0)),
        ],
        out_specs=pl.BlockSpec((NSA_WIDTH, Q_BLOCK), lambda b, q: (0, b * nq + q)),
        out_shape=jax.ShapeDtypeStruct((NSA_WIDTH, n), BF16),
        compiler_params=pltpu.CompilerParams(vmem_limit_bytes=VMEM_LIMIT),
        name="nsa",
    )(qt, kcmp, vcmpt, kslc, kwin, vt, vt, gt, et, xs, xw, ovt)


def _layer_norm_fm(h, g_ref, b_ref, reps):
    mu = jnp.mean(h, axis=0, keepdims=True)
    hc = h - mu
    var = jnp.mean(hc * hc, axis=0, keepdims=True)
    gain = jnp.concatenate([g_ref[...]] * reps, axis=1)
    bias = jnp.concatenate([b_ref[...]] * reps, axis=1)
    return hc * lax.rsqrt(var + LN_EPS) * gain + bias


def _mix_kernel(nsat_ref, zt_ref, halo_ref, x_ref, won_ref, wop_ref, pwt_ref, ps_ref, g_ref, b_ref,
                h_ref, hb_ref, *, tn, seq):
    t0 = (pl.program_id(0) * tn) % seq
    reps = tn // LANES
    zc = jnp.concatenate([halo_ref[...], zt_ref[...]], axis=1)
    t_ext = t0 - LANES + lax.broadcasted_iota(I32, (1, LANES + tn), 1)
    zc = jnp.where(t_ext >= 0, zc, 0.0)
    pooled = []
    for gi, win in enumerate(POOL_WINDOWS):
        zg = zc[gi * POOL_GROUP_DIM:(gi + 1) * POOL_GROUP_DIM]
        s = zg
        sh = 1
        while sh < win:
            s = s + pltpu.roll(s, sh, axis=1)
            sh *= 2
        cnt = jnp.clip(t_ext + 1, 1, win).astype(F32)
        y = (s / cnt - zg)[:, LANES:]
        scale = jnp.concatenate([ps_ref[gi * POOL_GROUP_DIM:(gi + 1) * POOL_GROUP_DIM]] * reps, axis=1)
        pooled.append((_dot(pwt_ref[gi], y.astype(BF16)) * scale).astype(BF16))
    pool_t = jnp.concatenate(pooled, axis=0)
    mix = _dot(won_ref[...], nsat_ref[...]) + _dot(wop_ref[...], pool_t)
    h = ALPHA * x_ref[...].T + mix
    h = _layer_norm_fm(h, g_ref, b_ref, reps)
    h_ref[...] = h
    hb_ref[...] = h.astype(BF16)


def _lane_bcast(v):
    return jnp.broadcast_to(v.reshape(-1, 1), (v.shape[0], LANES)).astype(F32)


def _mix(nsat, zt, x2, w_out, pool_w, pool_scale, ln_g, ln_b, seq):
    n = x2.shape[0]
    tn = 256
    wot = w_out.T.astype(BF16)
    full = lambda *s: pl.BlockSpec(s, lambda i: (0,) * len(s))
    return pl.pallas_call(
        functools.partial(_mix_kernel, tn=tn, seq=seq),
        grid=(n // tn,),
        in_specs=[
            pl.BlockSpec((NSA_WIDTH, tn), lambda i: (0, i)),
            pl.BlockSpec((POOL_WIDTH, tn), lambda i: (0, i)),
            pl.BlockSpec((POOL_WIDTH, LANES), lambda i: (0, jnp.maximum(i * (tn // LANES) - 1, 0))),
            pl.BlockSpec((tn, D_MODEL), lambda i: (i, 0)),
            full(D_MODEL, NSA_WIDTH), full(D_MODEL, POOL_WIDTH),
            full(len(POOL_WINDOWS), POOL_GROUP_DIM, POOL_GROUP_DIM),
            full(POOL_WIDTH, LANES), full(D_MODEL, LANES), full(D_MODEL, LANES),
        ],
        out_specs=[pl.BlockSpec((D_MODEL, tn), lambda i: (0, i))] * 2,
        out_shape=[jax.ShapeDtypeStruct((D_MODEL, n), F32), jax.ShapeDtypeStruct((D_MODEL, n), BF16)],
        compiler_params=pltpu.CompilerParams(vmem_limit_bytes=VMEM_LIMIT),
        name="mix",
    )(nsat, zt, zt, x2, wot[:, :NSA_WIDTH], wot[:, NSA_WIDTH:],
      jnp.swapaxes(pool_w, 1, 2).astype(BF16), _lane_bcast(pool_scale), _lane_bcast(ln_g), _lane_bcast(ln_b))


ROUTE_HEAD_GROUP = 4
_REMOVED = -(2.0 ** 100)
_PAD_SCORE = -3e38


def _top16_by_index(s):
    rows = lax.broadcasted_iota(I32, s.shape, 0).astype(F32)
    row16 = lax.broadcasted_iota(I32, (PEER_TOPK, s.shape[1]), 0)
    cur = s
    rank = jnp.full(s.shape, float(PEER_TOPK), F32)
    vals = jnp.zeros((PEER_TOPK, s.shape[1]), F32)
    for r in range(PEER_TOPK):
        mx = jnp.max(cur, axis=0, keepdims=True)
        first = jnp.min(jnp.where(cur == mx, rows, float(s.shape[0])), axis=0, keepdims=True)
        hit = rows == first
        vals = jnp.where(row16 == r, mx, vals)
        rank = jnp.where(hit, float(r), rank)
        cur = jnp.where(hit, -jnp.inf, cur)
    return vals, rank, jnp.full((1, s.shape[1]), float(PEER_TOPK), F32)


def _top16_by_value(scores):
    width = scores[0].shape[1]
    row16 = lax.broadcasted_iota(I32, (PEER_TOPK, width), 0)
    cur = list(scores)
    vals = [jnp.zeros((PEER_TOPK, width), F32) for _ in scores]
    for r in range(PEER_TOPK):
        mx = [jnp.max(c, axis=0, keepdims=True) for c in cur]
        vals = [jnp.where(row16 == r, m, v) for m, v in zip(mx, vals)]
        cur = [jnp.where(c == m, _REMOVED * (r + 1), c) for c, m in zip(cur, mx)]
    out = []
    for c, v in zip(cur, vals):
        member = c <= _REMOVED
        rank = jnp.where(member, c * (1.0 / _REMOVED) - 1.0, float(PEER_TOPK))
        out.append((v, rank, jnp.sum(jnp.where(member, 1.0, 0.0), axis=0, keepdims=True)))
    return out


def _candidate_table(v1, v2):
    tn = v1.shape[1]
    row8 = lax.broadcasted_iota(I32, (SUBLANES, tn), 0)
    groups = [v1[0:1, :] + v2]
    real = [PEER_TOPK]
    for a in range(1, SUBLANES):
        nb = PEER_TOPK // (a + 1)
        groups.append(jnp.where(row8 < nb, v1[a:a + 1, :] + v2[0:SUBLANES, :], _PAD_SCORE))
        real.append(nb)
    groups.append(v1[SUBLANES:, :] + v2[0:1, :])
    real.append(PEER_TOPK - SUBLANES)
    return jnp.concatenate(groups, axis=0), real


def _peer_scores(h, hb_ref, wqt_ref, sk1_ref, sk2_ref):
    half = PEER_QDIM // 2
    q = _dot(wqt_ref[h * PEER_QDIM:(h + 1) * PEER_QDIM, :], hb_ref[...])
    return _dot(sk1_ref[...], q[:half].astype(BF16)), _dot(sk2_ref[...], q[half:].astype(BF16))


def _route_heads(heads, scores, outs, *, tn, by_value):
    n_ref, a_ref, r2_ref, b_ref = outs
    want = float(PEER_TOPK)
    miscounts = []
    per_head = []
    for h, (s1, s2) in zip(heads, scores):
        if by_value:
            (v1, rank1, count1), (v2, rank2, count2) = _top16_by_value([s1, s2])
        else:
            (v1, rank1, count1), (v2, rank2, count2) = _top16_by_index(s1), _top16_by_index(s2)
        miscounts.append(jnp.maximum(jnp.abs(count1 - want), jnp.abs(count2 - want)))
        cs, real = _candidate_table(v1, v2)
        per_head.append((h, s1, s2, v1, v2, rank1, rank2, cs))

    if by_value:
        cur = [ph[-1] for ph in per_head]
        for _ in range(PEER_TOPK):
            cur = [jnp.where(c == jnp.max(c, axis=0, keepdims=True), -jnp.inf, c) for c in cur]
        sels = [c == -jnp.inf for c in cur]
    else:
        sels = []
        for ph in per_head:
            cs = ph[-1]
            c_i = lax.broadcasted_iota(I32, cs.shape, 0)
            cnt = jnp.zeros(cs.shape, F32)
            base = 0
            for g, n_real in enumerate(real):
                for c in range(base, base + n_real):
                    row = cs[c:c + 1, :]
                    beats = (row > cs) | ((row == cs) & (c_i > c))
                    cnt = cnt + jnp.where(beats, 1.0, 0.0)
                base += PEER_TOPK if g == 0 else SUBLANES
            sels.append(cnt < want)

    for i, ((h, s1, s2, v1, v2, rank1, rank2, cs), sel) in enumerate(zip(per_head, sels)):
        self_f = jnp.where(sel, 1.0, 0.0)
        miscounts[i] = jnp.maximum(miscounts[i], jnp.abs(jnp.sum(self_f, axis=0, keepdims=True) - want))
        top = v1[0:1, :] + v2[0:1, :]
        z = jnp.sum(jnp.where(sel, jnp.exp(cs - top), 0.0), axis=0, keepdims=True)
        nmap = jnp.zeros((PEER_KEYS, tn), F32)
        base = 0
        for a in range(PEER_TOPK):
            if a < SUBLANES:
                size = PEER_TOPK if a == 0 else SUBLANES
                n_a = jnp.sum(self_f[base:base + size, :], axis=0, keepdims=True)
                base += size
            else:
                n_a = self_f[base + a - SUBLANES:base + a - SUBLANES + 1, :]
            nmap = jnp.where(rank1 == float(a), n_a, nmap)
        n_ref[h] = nmap
        a_ref[h] = jnp.where(rank1 < want, jnp.exp(s1 - v1[0:1, :]), 0.0)
        r2_ref[h] = rank2.astype(BF16)
        b_ref[h] = (jnp.where(rank2 < want, jnp.exp(s2 - v2[0:1, :]), 0.0) / z).astype(BF16)
    return miscounts


def _route_kernel(hb_ref, wqt_ref, sk1_ref, sk2_ref, n_ref, a_ref, r2_ref, b_ref, *, tn):
    scores = functools.partial(_peer_scores, hb_ref=hb_ref, wqt_ref=wqt_ref, sk1_ref=sk1_ref, sk2_ref=sk2_ref)
    outs = (n_ref, a_ref, r2_ref, b_ref)
    groups = [range(h0, h0 + ROUTE_HEAD_GROUP) for h0 in range(0, PEER_HEADS, ROUTE_HEAD_GROUP)]
    miscounts = []
    ready = [scores(h) for h in groups[0]]
    for gi, heads in enumerate(groups):
        ahead = [scores(h) for h in groups[gi + 1]] if gi + 1 < len(groups) else None
        miscounts += _route_heads(heads, ready, outs, tn=tn, by_value=True)
        ready = ahead

    any_miscount = functools.reduce(jnp.maximum, miscounts)

    @pl.when(jnp.max(any_miscount) > 0.0)
    def _():
        for h in range(PEER_HEADS):
            @pl.when(jnp.max(miscounts[h]) > 0.0)
            def _(h=h):
                _route_heads([h], [scores(h)], outs, tn=tn, by_value=False)


def _route(hb, wq, subkeys):
    n = hb.shape[1]
    tn = LANES
    full = lambda *s: pl.BlockSpec(s, lambda i: (0,) * len(s))
    tab = lambda: pl.BlockSpec((PEER_HEADS, PEER_KEYS, tn), lambda i: (0, 0, i))
    shape = (PEER_HEADS, PEER_KEYS, n)
    return pl.pallas_call(
        functools.partial(_route_kernel, tn=tn),
        grid=(n // tn,),
        in_specs=[
            pl.BlockSpec((D_MODEL, tn), lambda i: (0, i)),
            full(PEER_HEADS * PEER_QDIM, D_MODEL), full(PEER_KEYS, PEER_QDIM // 2), full(PEER_KEYS, PEER_QDIM // 2),
        ],
        out_specs=[tab(), tab(), tab(), tab()],
        out_shape=[
            jax.ShapeDtypeStruct(shape, F32),
            jax.ShapeDtypeStruct(shape, F32),
            jax.ShapeDtypeStruct(shape, BF16),
            jax.ShapeDtypeStruct(shape, BF16),
        ],
        compiler_params=pltpu.CompilerParams(vmem_limit_bytes=VMEM_LIMIT),
        name="peer_route",
    )(hb, wq.T.astype(BF16), subkeys[0].astype(BF16), subkeys[1].astype(BF16))


def _experts_kernel(hb_ref, u_ref, vt_ref, n_ref, a_ref, r2_ref, b_ref, h_ref, g_ref, bb_ref,
                    out_ref, acc_ref, rows_ref, *, tn, te):
    j = pl.program_id(1)
    keys_per_step = te // PEER_KEYS

    def stage_rows(tile, slot):
        for c in range(keys_per_step):
            for h in range(PEER_HEADS):
                for k, ref in enumerate((n_ref, a_ref)):
                    row = ref[h, pl.ds(tile * keys_per_step + c, 1), :]
                    rows_ref[slot, k, c, h] = jnp.broadcast_to(row, (BF16_ROWS, tn)).astype(BF16)

    @pl.when(j == 0)
    def _():
        acc_ref[...] = jnp.zeros_like(acc_ref)
        stage_rows(0, 0)

    slot = j & 1
    hb = hb_ref[...]
    part = te // EXPERT_PARTS
    acts = [_dot(u_ref[i * part:(i + 1) * part, :], hb) for i in range(EXPERT_PARTS)]
    chunks = part // PEER_KEYS
    tiles = PEER_KEYS // BF16_ROWS
    zero = jnp.zeros((tiles, BF16_ROWS, tn), BF16)

    for i in range(EXPERT_PARTS):
        act = _gelu(acts[i])
        ps = []
        for c in range(chunks):
            key = i * chunks + c
            w = zero
            for h in range(PEER_HEADS):
                picked = jnp.where(r2_ref[h] < rows_ref[slot, 0, key, h][None], b_ref[h], zero)
                w = w + rows_ref[slot, 1, key, h][None] * picked
            ps.append(w.reshape(PEER_KEYS, tn) * act[c * PEER_KEYS:(c + 1) * PEER_KEYS].astype(BF16))
        acc_ref[...] += _dot(vt_ref[:, i * part:(i + 1) * part], jnp.concatenate(ps, axis=0))

    stage_rows(jnp.minimum(j + 1, pl.num_programs(1) - 1), 1 - slot)

    @pl.when(j == pl.num_programs(1) - 1)
    def _():
        h = ALPHA * h_ref[...] + acc_ref[...]
        out_ref[...] = _layer_norm_fm(h, g_ref, bb_ref, tn // LANES).T


def _experts(hb, hf, u_b, vt_b, tabs, ln_g, ln_b):
    n = hb.shape[1]
    tn, te = 512, 1024
    tab = lambda: pl.BlockSpec((PEER_HEADS, PEER_KEYS, tn), lambda i, j: (0, 0, i))
    tiles = PEER_KEYS // BF16_ROWS
    tiled = lambda: pl.BlockSpec((PEER_HEADS, tiles, BF16_ROWS, tn), lambda i, j: (0, 0, 0, i))
    n_tab, a_tab, r2_tab, b_tab = tabs
    tabs = (n_tab, a_tab, r2_tab.reshape(PEER_HEADS, tiles, BF16_ROWS, n), b_tab.reshape(PEER_HEADS, tiles, BF16_ROWS, n))
    return pl.pallas_call(
        functools.partial(_experts_kernel, tn=tn, te=te),
        grid=(n // tn, PEER_EXPERTS // te),
        in_specs=[
            pl.BlockSpec((D_MODEL, tn), lambda i, j: (0, i)),
            pl.BlockSpec((te, D_MODEL), lambda i, j: (j, 0)),
            pl.BlockSpec((D_MODEL, te), lambda i, j: (0, j)),
            tab(), tab(), tiled(), tiled(),
            pl.BlockSpec((D_MODEL, tn), lambda i, j: (0, i)),
            pl.BlockSpec((D_MODEL, LANES), lambda i, j: (0, 0)),
            pl.BlockSpec((D_MODEL, LANES), lambda i, j: (0, 0)),
        ],
        out_specs=pl.BlockSpec((tn, D_MODEL), lambda i, j: (i, 0)),
        out_shape=jax.ShapeDtypeStruct((n, D_MODEL), F32),
        scratch_shapes=[pltpu.VMEM((D_MODEL, tn), F32),
                        pltpu.VMEM((2, 2, te // PEER_KEYS, PEER_HEADS, BF16_ROWS, tn), BF16)],
        compiler_params=pltpu.CompilerParams(
            dimension_semantics=("arbitrary", "arbitrary"), vmem_limit_bytes=VMEM_LIMIT),
        name="peer_experts",
    )(hb, u_b, vt_b, *tabs, hf, _lane_bcast(ln_g), _lane_bcast(ln_b))


def kernel(x, w_in, rel_table, cmp_pe_k, cmp_pe_v, cmp_k_w1, cmp_k_w2, cmp_v_w1, cmp_v_w2, pool_w, pool_scale,
           w_out, ln1_g, ln1_b, peer_wq, peer_subkeys, peer_u, peer_v, ln2_g, ln2_b):
    batch, seq, _ = x.shape
    assert w_in.shape[0] == DEPTH == 1 and seq % 512 == 0 and seq // CMP_STRIDE <= 256
    x2 = x.reshape(batch * seq, D_MODEL)
    et, xs, xw = _bias_tables(rel_table)
    qt, vt, gt, zt, kvc, kslc, kwin = _inproj(x2, w_in[0], seq)
    kcmp, vcmpt = _compress(kvc, batch, seq, cmp_pe_k[0], cmp_pe_v[0], cmp_k_w1[0], cmp_k_w2[0],
                            cmp_v_w1[0], cmp_v_w2[0])
    nsat = _nsa(qt, kcmp, vcmpt, kslc, kwin, vt, gt, et, xs, xw, batch, seq)
    hf, hb = _mix(nsat, zt, x2, w_out[0], pool_w[0], pool_scale[0], ln1_g[0], ln1_b[0], seq)
    tabs = _route(hb, peer_wq[0], peer_subkeys[0])
    y = _experts(hb, hf, peer_u[0].astype(BF16), peer_v[0].T.astype(BF16), tabs, ln2_g[0], ln2_b[0])
    return y.reshape(batch, seq, D_MODEL)
```

```python
import functools
import math

import numpy as np
import jax
import jax.numpy as jnp
from jax import lax
from jax.experimental import pallas as pl
from jax.experimental.pallas import tpu as pltpu

F32 = jnp.float32
BF16 = jnp.bfloat16
I32 = jnp.int32

D_MODEL = 1024
NSA_HEADS = 8
KV_HEADS = 2
GQA = NSA_HEADS // KV_HEADS
HEAD_DIM = 64
NSA_WIDTH = NSA_HEADS * HEAD_DIM
KV_WIDTH = KV_HEADS * HEAD_DIM
POOL_WIDTH = D_MODEL - NSA_WIDTH
POOL_WINDOWS = (2, 4, 8, 16)
POOL_GROUP_DIM = POOL_WIDTH // len(POOL_WINDOWS)
N_GATES = 3 * NSA_HEADS
GATE_ROWS = 32
CMP_BLOCK = 32
CMP_STRIDE = 16
CMP_HIDDEN = 256
SLC_BLOCK = 64
SLC_COUNT = 16
WINDOW = 512
Q_BLOCK = 128
NEG = -1e30
FORCE = 1e6
N_BUCKETS = 32
MAX_DISTANCE = 128
PEER_HEADS = 8
PEER_KEYS = 128
PEER_EXPERTS = PEER_KEYS * PEER_KEYS
PEER_QDIM = 256
PEER_TOPK = 16
DEPTH = 1
ALPHA = (2 * DEPTH) ** 0.25
LN_EPS = 1e-5
SQRT_HALF = 0.7071067811865476

LANES = 128
SUBLANES = 8
BF16_ROWS = SUBLANES
EXPERT_PARTS = 4
VMEM_LIMIT = 52 * 1024 * 1024

_FM_Q, _FM_V, _FM_G, _FM_Z, _FM_END = 0, 512, 768, 800, 1312
_TM_WIDTH = 640


def _gelu(x):
    return 0.5 * x * (1.0 + lax.erf(x * SQRT_HALF))


def _dot(a, b):
    return jnp.dot(a, b, preferred_element_type=F32)


def _dot_nt(a, b):
    return lax.dot_general(a, b, (((1,), (1,)), ((), ())), preferred_element_type=F32)


def _bucket_np(dist):
    dist = np.maximum(dist, 0)
    max_exact = N_BUCKETS // 2
    large = max_exact + np.floor(
        np.log(np.maximum(dist, 1) / max_exact) / math.log(MAX_DISTANCE / max_exact)
        * (N_BUCKETS - max_exact)).astype(np.int64)
    large = np.minimum(large, N_BUCKETS - 1)
    return np.where(dist < max_exact, dist, large)


ET_ROWS = 512
XS_ROWS = 896
XS_MASKED_ROW = 640
XW_ROWS = 1152
_CODE_MASKED = N_BUCKETS


def _bias_index_tables():
    qi = np.arange(Q_BLOCK)[None, :]

    def codes(dist, visible):
        return np.where(visible, _bucket_np(dist), _CODE_MASKED).astype(np.int32)

    d_cmp = qi - CMP_STRIDE * (np.arange(ET_ROWS)[:, None] - 256) - (CMP_BLOCK - 1)
    rho = np.arange(XS_ROWS)[:, None]
    d_sel = qi - (rho - 384)
    d_win = qi - (np.arange(XW_ROWS)[:, None] - 512)
    return np.concatenate([
        codes(d_cmp, d_cmp >= 0),
        codes(d_sel, (d_sel >= 0) & (rho < XS_MASKED_ROW)),
        codes(d_win, (d_win >= 0) & (d_win < WINDOW)),
    ], axis=0)


def _overlap_t(n_cmp_rows, n_slc):
    n = np.arange(n_cmp_rows)[None, :]
    j = np.arange(n_slc)[:, None]
    start = n * CMP_STRIDE
    end = start + CMP_BLOCK - 1
    return ((start <= j * SLC_BLOCK + SLC_BLOCK - 1) & (end >= j * SLC_BLOCK)).astype(np.float32)


def _bias_kernel(tab_ref, idx_ref, et_ref, xs_ref, xw_ref):
    g = pl.program_id(0)
    chunk = Q_BLOCK
    for r in range(GQA):
        h = g * GQA + r
        far = tab_ref[N_BUCKETS - 1, h]
        cols = slice(r * LANES, (r + 1) * LANES)
        for c in range((ET_ROWS + XS_ROWS + XW_ROWS) // chunk):
            idx = idx_ref[c * chunk:(c + 1) * chunk, :]
            acc = jnp.where(idx == _CODE_MASKED, NEG, 0.0)
            for b in range(N_BUCKETS - 1):
                acc = jnp.where(idx == b, tab_ref[b, h] - far, acc)
            row = c * chunk
            if row < ET_ROWS:
                et_ref[r, row:row + chunk, :] = acc
            elif row < ET_ROWS + XS_ROWS:
                xs_ref[0, row - ET_ROWS:row - ET_ROWS + chunk, cols] = acc
            else:
                row -= ET_ROWS + XS_ROWS
                xw_ref[0, row:row + chunk, cols] = acc


def _bias_tables(rel_table):
    idx = _bias_index_tables()
    qw = GQA * LANES
    return pl.pallas_call(
        _bias_kernel,
        grid=(KV_HEADS,),
        in_specs=[
            pl.BlockSpec(memory_space=pltpu.SMEM),
            pl.BlockSpec(idx.shape, lambda g: (0, 0)),
        ],
        out_specs=[
            pl.BlockSpec((GQA, ET_ROWS, Q_BLOCK), lambda g: (g, 0, 0)),
            pl.BlockSpec((1, XS_ROWS, qw), lambda g: (g, 0, 0)),
            pl.BlockSpec((1, XW_ROWS, qw), lambda g: (g, 0, 0)),
        ],
        out_shape=[
            jax.ShapeDtypeStruct((NSA_HEADS, ET_ROWS, Q_BLOCK), F32),
            jax.ShapeDtypeStruct((KV_HEADS, XS_ROWS, qw), F32),
            jax.ShapeDtypeStruct((KV_HEADS, XW_ROWS, qw), F32),
        ],
        name="bias_tables",
    )(rel_table, jnp.asarray(idx))


def _inproj_kernel(x_ref, wfm_ref, wtm_ref, qt_ref, vt_ref, gt_ref, zt_ref, kvc_ref,
                   kslc_ref, kwin_ref, *, tn, seq):
    xb = x_ref[...].astype(BF16)
    fm = _dot_nt(wfm_ref[...], xb)
    tm = _dot(xb, wtm_ref[...])
    qt_ref[...] = (fm[_FM_Q:_FM_V] * (HEAD_DIM ** -0.5)).astype(BF16)
    for c in range(2 * KV_HEADS):
        for s in range(tn // LANES):
            vt_ref[c, s] = fm[_FM_V + c * HEAD_DIM:_FM_V + (c + 1) * HEAD_DIM,
                              s * LANES:(s + 1) * LANES].astype(BF16)
    gt_ref[...] = jax.nn.sigmoid(fm[_FM_G:_FM_Z])
    zt_ref[...] = fm[_FM_Z:_FM_END]
    kvc_ref[0] = tm[:, 0:KV_WIDTH]
    kvc_ref[1] = tm[:, KV_WIDTH:2 * KV_WIDTH]
    t0 = (pl.program_id(0) * tn) % seq
    row = lax.broadcasted_iota(I32, (tn, LANES), 0) + t0
    lane = lax.broadcasted_iota(I32, (tn, LANES), 1)
    onehot = jnp.where(lane - HEAD_DIM == (row >> 6), 1.0, 0.0)
    for g in range(KV_HEADS):
        k = tm[:, 256 + g * LANES:256 + (g + 1) * LANES]
        kslc_ref[g] = jnp.where(lane < HEAD_DIM, k, onehot).astype(BF16)
    kwin_ref[...] = tm[:, 512:640].astype(BF16)


def _inproj(x2, w_in, seq):
    n = x2.shape[0]
    tn = 512
    w = w_in
    o = NSA_WIDTH
    w_q = w[:, 0:o]
    w_kc, w_vc, w_ks, w_vs, w_kw, w_vw = [w[:, o + i * KV_WIDTH:o + (i + 1) * KV_WIDTH] for i in range(6)]
    o += 6 * KV_WIDTH
    w_g = w[:, o:o + N_GATES]
    w_z = w[:, o + N_GATES:]
    zpad = jnp.zeros((D_MODEL, HEAD_DIM), w.dtype)
    wfm = jnp.concatenate(
        [w_q, w_vs, w_vw, w_g, jnp.zeros((D_MODEL, GATE_ROWS - N_GATES), w.dtype), w_z], axis=1).T.astype(BF16)
    wtm = jnp.concatenate(
        [w_kc, w_vc, w_ks[:, :HEAD_DIM], zpad, w_ks[:, HEAD_DIM:], zpad, w_kw], axis=1).astype(BF16)
    nt = n // LANES
    return pl.pallas_call(
        functools.partial(_inproj_kernel, tn=tn, seq=seq),
        grid=(n // tn,),
        in_specs=[
            pl.BlockSpec((tn, D_MODEL), lambda i: (i, 0)),
            pl.BlockSpec((_FM_END, D_MODEL), lambda i: (0, 0)),
            pl.BlockSpec((D_MODEL, _TM_WIDTH), lambda i: (0, 0)),
        ],
        out_specs=[
            pl.BlockSpec((NSA_WIDTH, tn), lambda i: (0, i)),
            pl.BlockSpec((2 * KV_HEADS, tn // LANES, HEAD_DIM, LANES), lambda i: (0, i, 0, 0)),
            pl.BlockSpec((GATE_ROWS, tn), lambda i: (0, i)),
            pl.BlockSpec((POOL_WIDTH, tn), lambda i: (0, i)),
            pl.BlockSpec((2, tn, KV_WIDTH), lambda i: (0, i, 0)),
            pl.BlockSpec((KV_HEADS, tn, LANES), lambda i: (0, i, 0)),
            pl.BlockSpec((tn, LANES), lambda i: (i, 0)),
        ],
        out_shape=[
            jax.ShapeDtypeStruct((NSA_WIDTH, n), BF16),
            jax.ShapeDtypeStruct((2 * KV_HEADS, nt, HEAD_DIM, LANES), BF16),
            jax.ShapeDtypeStruct((GATE_ROWS, n), F32),
            jax.ShapeDtypeStruct((POOL_WIDTH, n), F32),
            jax.ShapeDtypeStruct((2, n, KV_WIDTH), F32),
            jax.ShapeDtypeStruct((KV_HEADS, n, LANES), BF16),
            jax.ShapeDtypeStruct((n, LANES), BF16),
        ],
        compiler_params=pltpu.CompilerParams(vmem_limit_bytes=VMEM_LIMIT),
        name="inproj",
    )(x2, wfm, wtm)


def _compress_kernel(kc_ref, vc_ref, w2k_ref, w2v_ref, pek_ref, pev_ref, w1k_ref, w1v_ref,
                     wok0_ref, wok1_ref, wovt_ref, kcmp_ref, vcmpt_ref, *, n_rows):
    half = CMP_BLOCK // 2
    acc_k = jnp.zeros((n_rows, 4 * CMP_HIDDEN), F32)
    acc_v = jnp.zeros((n_rows, 4 * CMP_HIDDEN), F32)
    for l in range(half):
        rows = pl.ds(l, n_rows, stride=CMP_STRIDE)
        acc_k = acc_k + _dot(kc_ref[rows, :].astype(BF16), w2k_ref[l])
        acc_v = acc_v + _dot(vc_ref[rows, :].astype(BF16), w2v_ref[l])
    ck = _dot(pek_ref[...], w1k_ref[...])[0:1]
    cv = _dot(pev_ref[...], w1v_ref[...])[0:1]

    def hidden(acc, c, g):
        a = acc[:, g * CMP_HIDDEN:(g + 1) * CMP_HIDDEN]
        b = acc[:, (2 + g) * CMP_HIDDEN:(3 + g) * CMP_HIDDEN]
        return _gelu(a + pltpu.roll(b, n_rows - 1, axis=0) + c).astype(BF16)

    kcmp_ref[...] = (_dot(hidden(acc_k, ck, 0), wok0_ref[...])
                     + _dot(hidden(acc_k, ck, 1), wok1_ref[...])).astype(BF16)
    for g in range(KV_HEADS):
        vcmpt_ref[g] = _dot_nt(wovt_ref[...], hidden(acc_v, cv, g)).astype(BF16)


def _compress(kvc, batch, seq, pe_k, pe_v, k_w1, k_w2, v_w1, v_w2):
    n_rows = seq // CMP_STRIDE
    half = CMP_BLOCK // 2

    def blockdiag(w1):
        w = w1.reshape(CMP_BLOCK, HEAD_DIM, CMP_HIDDEN)
        z = jnp.zeros((half, HEAD_DIM, CMP_HIDDEN), w1.dtype)
        top = jnp.concatenate([w[:half], z, w[half:], z], axis=2)
        bot = jnp.concatenate([z, w[:half], z, w[half:]], axis=2)
        return jnp.concatenate([top, bot], axis=1).astype(BF16)

    def pe_rows(pe):
        flat = pe.reshape(1, CMP_BLOCK * HEAD_DIM)
        return jnp.concatenate([flat, jnp.zeros((7, flat.shape[1]), pe.dtype)], axis=0).astype(BF16)

    zo = jnp.zeros((CMP_HIDDEN, HEAD_DIM), k_w2.dtype)
    wok0 = jnp.concatenate([k_w2, zo], axis=1).astype(BF16)
    wok1 = jnp.concatenate([zo, k_w2], axis=1).astype(BF16)
    full = lambda *s: pl.BlockSpec(s, lambda b: (0,) * len(s))
    return pl.pallas_call(
        functools.partial(_compress_kernel, n_rows=n_rows),
        grid=(batch,),
        in_specs=[
            pl.BlockSpec((None, seq, KV_WIDTH), lambda b: (0, b, 0)),
            pl.BlockSpec((None, seq, KV_WIDTH), lambda b: (1, b, 0)),
            full(half, KV_WIDTH, 4 * CMP_HIDDEN), full(half, KV_WIDTH, 4 * CMP_HIDDEN),
            full(8, CMP_BLOCK * HEAD_DIM), full(8, CMP_BLOCK * HEAD_DIM),
            full(CMP_BLOCK * HEAD_DIM, CMP_HIDDEN), full(CMP_BLOCK * HEAD_DIM, CMP_HIDDEN),
            full(CMP_HIDDEN, KV_WIDTH), full(CMP_HIDDEN, KV_WIDTH), full(HEAD_DIM, CMP_HIDDEN),
        ],
        out_specs=[
            pl.BlockSpec((None, n_rows, KV_WIDTH), lambda b: (b, 0, 0)),
            pl.BlockSpec((None, KV_HEADS, HEAD_DIM, n_rows), lambda b: (b, 0, 0, 0)),
        ],
        out_shape=[
            jax.ShapeDtypeStruct((batch, n_rows, KV_WIDTH), BF16),
            jax.ShapeDtypeStruct((batch, KV_HEADS, HEAD_DIM, n_rows), BF16),
        ],
        compiler_params=pltpu.CompilerParams(vmem_limit_bytes=VMEM_LIMIT),
        name="compress",
    )(kvc, kvc, blockdiag(k_w1), blockdiag(v_w1), pe_rows(pe_k), pe_rows(pe_v),
      k_w1.astype(BF16), v_w1.astype(BF16), wok0, wok1, v_w2.T.astype(BF16))


SUPER_TILE = 2 * Q_BLOCK
WIN_KEYS = WINDOW + Q_BLOCK


def _softmax_init(s, vt):
    m = jnp.max(s, axis=0, keepdims=True)
    p = jnp.exp(s - m)
    return m, jnp.sum(p, axis=0, keepdims=True), _dot(vt, p.astype(BF16))


def _softmax_step(s, vt, state):
    m, l, acc = state
    m_new = jnp.maximum(m, jnp.max(s, axis=0, keepdims=True))
    alpha = jnp.exp(m - m_new)
    p = jnp.exp(s - m_new)
    return m_new, alpha * l + jnp.sum(p, axis=0, keepdims=True), alpha * acc + _dot(vt, p.astype(BF16))


def _softmax_merge(a, b):
    m = jnp.maximum(a[0], b[0])
    wa = jnp.exp(a[0] - m)
    wb = jnp.exp(b[0] - m)
    return (wa * a[2] + wb * b[2]) * (1.0 / (wa * a[1] + wb * b[1]))


def _nsa_kernel(qt_ref, kcmp_ref, vcmpt_ref, kslc_ref, kwin_ref, vslct_ref, vwint_ref, gt_ref,
                et_ref, xs_ref, xw_ref, ovt_ref, out_ref, *, n_cmp_rows, n_slc):
    qb = pl.program_id(1)
    qw = GQA * Q_BLOCK
    odd = qb & 1
    st_diag = qb >> 1
    st_prev = jnp.maximum(st_diag - 1, 0)
    n_far = st_prev
    row_a = pl.multiple_of(384 - Q_BLOCK * odd, Q_BLOCK)
    row_b = pl.multiple_of(jnp.where(st_diag >= 1, Q_BLOCK - Q_BLOCK * odd, XS_MASKED_ROW), Q_BLOCK)
    k0 = jnp.maximum(qb * Q_BLOCK - WINDOW, 0)
    row_w = pl.multiple_of(WINDOW - qb * Q_BLOCK + k0, Q_BLOCK)
    kt0 = k0 >> 7
    off = pl.multiple_of(256 - 8 * qb, 8)
    j_i = lax.broadcasted_iota(I32, (n_slc, Q_BLOCK), 0)
    cur = (qb * Q_BLOCK + lax.broadcasted_iota(I32, (n_slc, Q_BLOCK), 1)) >> 6
    forced = (j_i == 0) | (j_i == cur) | (j_i == cur - 1)

    def sel_scores(g, qsel, st):
        k = kslc_ref[g, pl.ds(pl.multiple_of(st * SUPER_TILE, SUPER_TILE), SUPER_TILE), :]
        vt = jnp.concatenate([vslct_ref[g, 2 * st], vslct_ref[g, 2 * st + 1]], axis=1)
        return _dot(k, qsel), vt

    def near(g):
        q4 = qt_ref[g * GQA * HEAD_DIM:(g + 1) * GQA * HEAD_DIM, :]
        qcat = jnp.concatenate([q4[r * HEAD_DIM:(r + 1) * HEAD_DIM] for r in range(GQA)], axis=1)
        zq = jnp.zeros_like(qcat)
        qg = jnp.concatenate([qcat, zq] if g == 0 else [zq, qcat], axis=0)

        s = _dot(kcmp_ref[...], qg)
        s = s + jnp.concatenate([et_ref[g * GQA + r, pl.ds(off, n_cmp_rows), :] for r in range(GQA)], axis=1)
        m = jnp.max(s, axis=0, keepdims=True)
        p = jnp.exp(s - m)
        l = jnp.sum(p, axis=0, keepdims=True)
        p = p * jnp.where(m > 0.5 * NEG, 1.0 / l, 0.0)
        o_cmp = _dot(vcmpt_ref[g], p.astype(BF16))
        p_sum = p[:, 0:Q_BLOCK]
        for r in range(1, GQA):
            p_sum = p_sum + p[:, r * Q_BLOCK:(r + 1) * Q_BLOCK]
        imp = jnp.dot(ovt_ref[...], p_sum, preferred_element_type=F32,
                      precision=lax.Precision.HIGHEST)

        s_win = _dot(kwin_ref[pl.ds(pl.multiple_of(k0, Q_BLOCK), WIN_KEYS), :], qg)
        vt = jnp.concatenate([vwint_ref[g, kt0 + i] for i in range(WIN_KEYS // Q_BLOCK)], axis=1)
        m, l, acc = _softmax_init(s_win + xw_ref[g, pl.ds(row_w, WIN_KEYS), :], vt)
        o_win = acc * (1.0 / l)

        imp = jnp.where(forced, FORCE, imp)
        imp = jnp.where(j_i <= cur, imp, NEG)
        cnt = jnp.zeros((n_slc, Q_BLOCK), F32)
        for jp in range(n_slc):
            row = imp[jp:jp + 1, :]
            beats = (row > imp) | ((row == imp) & (j_i > jp))
            cnt = cnt + jnp.where(beats, 1.0, 0.0)
        selneg = jnp.where(cnt < float(SLC_COUNT), 0.0, NEG).astype(BF16)
        qsel = [qcat, jnp.concatenate([selneg] * GQA, axis=1)]
        if n_slc < LANES - HEAD_DIM:
            qsel.append(jnp.zeros((LANES - HEAD_DIM - n_slc, qw), BF16))
        qsel = jnp.concatenate(qsel, axis=0)

        s_a, vt_a = sel_scores(g, qsel, st_diag)
        s_b, vt_b = sel_scores(g, qsel, st_prev)
        chain_a = _softmax_init(s_a + xs_ref[g, pl.ds(row_a, SUPER_TILE), :], vt_a)
        chain_b = _softmax_init(s_b + xs_ref[g, pl.ds(row_b, SUPER_TILE), :], vt_b)
        return o_cmp, o_win, qsel, (chain_a, chain_b)

    heads = [near(g) for g in range(KV_HEADS)]
    qsels = [h[2] for h in heads]

    def far_steps(first, count, chains):
        scored = [[sel_scores(g, qsels[g], first + k) for k in range(count)] for g in range(KV_HEADS)]
        chains = [list(c) for c in chains]
        for k in range(count):
            for g in range(KV_HEADS):
                s, vt = scored[g][k]
                chains[g][k % 2] = _softmax_step(s, vt, chains[g][k % 2])
        return tuple(tuple(c) for c in chains)

    chains = tuple(h[3] for h in heads)
    chains = lax.fori_loop(0, n_far >> 2, lambda i, c: far_steps(4 * i, 4, c), chains)
    rest = n_far & ~3
    chains = lax.cond((n_far & 2) != 0, lambda c: far_steps(rest, 2, c), lambda c: c, chains)
    chains = lax.cond((n_far & 1) != 0, lambda c: far_steps(n_far - 1, 1, c), lambda c: c, chains)

    outs = []
    for g in range(KV_HEADS):
        o_cmp, o_win = heads[g][0], heads[g][1]
        o_slc = _softmax_merge(*chains[g])
        for r in range(GQA):
            cols = slice(r * Q_BLOCK, (r + 1) * Q_BLOCK)
            gate = [gt_ref[c * NSA_HEADS + g * GQA + r:c * NSA_HEADS + g * GQA + r + 1, :] for c in range(3)]
            outs.append(gate[0] * o_cmp[:, cols] + gate[1] * o_slc[:, cols] + gate[2] * o_win[:, cols])
    out_ref[...] = jnp.concatenate(outs, axis=0).astype(BF16)


def _nsa(qt, kcmp, vcmpt, kslc, kwin, vt, gt, et, xs, xw, batch, seq):
    assert seq % SUPER_TILE == 0 and seq >= WIN_KEYS and seq // CMP_STRIDE <= 256 and seq // SLC_BLOCK <= HEAD_DIM
    nq = seq // Q_BLOCK
    n_cmp_rows = seq // CMP_STRIDE
    n_slc = seq // SLC_BLOCK
    n = batch * seq
    ovt = jnp.asarray(_overlap_t(n_cmp_rows, n_slc))
    qw = GQA * LANES
    return pl.pallas_call(
        functools.partial(_nsa_kernel, n_cmp_rows=n_cmp_rows, n_slc=n_slc),
        grid=(batch, nq),
        in_specs=[
            pl.BlockSpec((NSA_WIDTH, Q_BLOCK), lambda b, q: (0, b * nq + q)),
            pl.BlockSpec((None, n_cmp_rows, KV_WIDTH), lambda b, q: (b, 0, 0)),
            pl.BlockSpec((None, KV_HEADS, HEAD_DIM, n_cmp_rows), lambda b, q: (b, 0, 0, 0)),
            pl.BlockSpec((KV_HEADS, seq, LANES), lambda b, q: (0, b, 0)),
            pl.BlockSpec((seq, LANES), lambda b, q: (b, 0)),
            pl.BlockSpec((KV_HEADS, nq, HEAD_DIM, LANES), lambda b, q: (0, b, 0, 0)),
            pl.BlockSpec((KV_HEADS, nq, HEAD_DIM, LANES), lambda b, q: (1, b, 0, 0)),
            pl.BlockSpec((GATE_ROWS, Q_BLOCK), lambda b, q: (0, b * nq + q)),
            pl.BlockSpec((NSA_HEADS, ET_ROWS, Q_BLOCK), lambda b, q: (0, 0, 0)),
            pl.BlockSpec((KV_HEADS, XS_ROWS, qw), lambda b, q: (0, 0, 0)),
            pl.BlockSpec((KV_HEADS, XW_ROWS, qw), lambda b, q: (0, 0, 0)),
            pl.BlockSpec((n_slc, n_cmp_rows), lambda b, q: (0, 0)),
        ],
        out_specs=pl.BlockSpec((NSA_WIDTH, Q_BLOCK), lambda b, q: (0, b * nq + q)),
        out_shape=jax.ShapeDtypeStruct((NSA_WIDTH, n), BF16),
        compiler_params=pltpu.CompilerParams(vmem_limit_bytes=VMEM_LIMIT),
        name="nsa",
    )(qt, kcmp, vcmpt, kslc, kwin, vt, vt, gt, et, xs, xw, ovt)


def _layer_norm_fm(h, g_ref, b_ref, reps):
    mu = jnp.mean(h, axis=0, keepdims=True)
    hc = h - mu
    var = jnp.mean(hc * hc, axis=0, keepdims=True)
    gain = jnp.concatenate([g_ref[...]] * reps, axis=1)
    bias = jnp.concatenate([b_ref[...]] * reps, axis=1)
    return hc * lax.rsqrt(var + LN_EPS) * gain + bias


def _mix_kernel(nsat_ref, zt_ref, halo_ref, x_ref, won_ref, wop_ref, pwt_ref, ps_ref, g_ref, b_ref,
                h_ref, hb_ref, *, tn, seq):
    t0 = (pl.program_id(0) * tn) % seq
    reps = tn // LANES
    zc = jnp.concatenate([halo_ref[...], zt_ref[...]], axis=1)
    t_ext = t0 - LANES + lax.broadcasted_iota(I32, (1, LANES + tn), 1)
    zc = jnp.where(t_ext >= 0, zc, 0.0)
    pooled = []
    for gi, win in enumerate(POOL_WINDOWS):
        zg = zc[gi * POOL_GROUP_DIM:(gi + 1) * POOL_GROUP_DIM]
        s = zg
        sh = 1
        while sh < win:
            s = s + pltpu.roll(s, sh, axis=1)
            sh *= 2
        cnt = jnp.clip(t_ext + 1, 1, win).astype(F32)
        y = (s / cnt - zg)[:, LANES:]
        scale = jnp.concatenate([ps_ref[gi * POOL_GROUP_DIM:(gi + 1) * POOL_GROUP_DIM]] * reps, axis=1)
        pooled.append((_dot(pwt_ref[gi], y.astype(BF16)) * scale).astype(BF16))
    pool_t = jnp.concatenate(pooled, axis=0)
    mix = _dot(won_ref[...], nsat_ref[...]) + _dot(wop_ref[...], pool_t)
    h = ALPHA * x_ref[...].T + mix
    h = _layer_norm_fm(h, g_ref, b_ref, reps)
    h_ref[...] = h
    hb_ref[...] = h.astype(BF16)


def _lane_bcast(v):
    return jnp.broadcast_to(v.reshape(-1, 1), (v.shape[0], LANES)).astype(F32)


def _mix(nsat, zt, x2, w_out, pool_w, pool_scale, ln_g, ln_b, seq):
    n = x2.shape[0]
    tn = 256
    wot = w_out.T.astype(BF16)
    full = lambda *s: pl.BlockSpec(s, lambda i: (0,) * len(s))
    return pl.pallas_call(
        functools.partial(_mix_kernel, tn=tn, seq=seq),
        grid=(n // tn,),
        in_specs=[
            pl.BlockSpec((NSA_WIDTH, tn), lambda i: (0, i)),
            pl.BlockSpec((POOL_WIDTH, tn), lambda i: (0, i)),
            pl.BlockSpec((POOL_WIDTH, LANES), lambda i: (0, jnp.maximum(i * (tn // LANES) - 1, 0))),
            pl.BlockSpec((tn, D_MODEL), lambda i: (i, 0)),
            full(D_MODEL, NSA_WIDTH), full(D_MODEL, POOL_WIDTH),
            full(len(POOL_WINDOWS), POOL_GROUP_DIM, POOL_GROUP_DIM),
            full(POOL_WIDTH, LANES), full(D_MODEL, LANES), full(D_MODEL, LANES),
        ],
        out_specs=[pl.BlockSpec((D_MODEL, tn), lambda i: (0, i))] * 2,
        out_shape=[jax.ShapeDtypeStruct((D_MODEL, n), F32), jax.ShapeDtypeStruct((D_MODEL, n), BF16)],
        compiler_params=pltpu.CompilerParams(vmem_limit_bytes=VMEM_LIMIT),
        name="mix",
    )(nsat, zt, zt, x2, wot[:, :NSA_WIDTH], wot[:, NSA_WIDTH:],
      jnp.swapaxes(pool_w, 1, 2).astype(BF16), _lane_bcast(pool_scale), _lane_bcast(ln_g), _lane_bcast(ln_b))


ROUTE_HEAD_GROUP = 4
_REMOVED = -(2.0 ** 100)
_PAD_SCORE = -3e38


def _top16_by_index(s):
    rows = lax.broadcasted_iota(I32, s.shape, 0).astype(F32)
    row16 = lax.broadcasted_iota(I32, (PEER_TOPK, s.shape[1]), 0)
    cur = s
    rank = jnp.full(s.shape, float(PEER_TOPK), F32)
    vals = jnp.zeros((PEER_TOPK, s.shape[1]), F32)
    for r in range(PEER_TOPK):
        mx = jnp.max(cur, axis=0, keepdims=True)
        first = jnp.min(jnp.where(cur == mx, rows, float(s.shape[0])), axis=0, keepdims=True)
        hit = rows == first
        vals = jnp.where(row16 == r, mx, vals)
        rank = jnp.where(hit, float(r), rank)
        cur = jnp.where(hit, -jnp.inf, cur)
    return vals, rank, jnp.full((1, s.shape[1]), float(PEER_TOPK), F32)


def _top16_by_value(scores):
    width = scores[0].shape[1]
    row16 = lax.broadcasted_iota(I32, (PEER_TOPK, width), 0)
    cur = list(scores)
    vals = [jnp.zeros((PEER_TOPK, width), F32) for _ in scores]
    for r in range(PEER_TOPK):
        mx = [jnp.max(c, axis=0, keepdims=True) for c in cur]
        vals = [jnp.where(row16 == r, m, v) for m, v in zip(mx, vals)]
        cur = [jnp.where(c == m, _REMOVED * (r + 1), c) for c, m in zip(cur, mx)]
    out = []
    for c, v in zip(cur, vals):
        member = c <= _REMOVED
        rank = jnp.where(member, c * (1.0 / _REMOVED) - 1.0, float(PEER_TOPK))
        out.append((v, rank, jnp.sum(jnp.where(member, 1.0, 0.0), axis=0, keepdims=True)))
    return out


def _candidate_table(v1, v2):
    tn = v1.shape[1]
    row8 = lax.broadcasted_iota(I32, (SUBLANES, tn), 0)
    groups = [v1[0:1, :] + v2]
    real = [PEER_TOPK]
    for a in range(1, SUBLANES):
        nb = PEER_TOPK // (a + 1)
        groups.append(jnp.where(row8 < nb, v1[a:a + 1, :] + v2[0:SUBLANES, :], _PAD_SCORE))
        real.append(nb)
    groups.append(v1[SUBLANES:, :] + v2[0:1, :])
    real.append(PEER_TOPK - SUBLANES)
    return jnp.concatenate(groups, axis=0), real


def _peer_scores(h, hb_ref, wqt_ref, sk1_ref, sk2_ref):
    half = PEER_QDIM // 2
    q = _dot(wqt_ref[h * PEER_QDIM:(h + 1) * PEER_QDIM, :], hb_ref[...])
    return _dot(sk1_ref[...], q[:half].astype(BF16)), _dot(sk2_ref[...], q[half:].astype(BF16))


def _route_heads(heads, scores, outs, *, tn, by_value):
    n_ref, a_ref, r2_ref, b_ref = outs
    want = float(PEER_TOPK)
    miscounts = []
    per_head = []
    for h, (s1, s2) in zip(heads, scores):
        if by_value:
            (v1, rank1, count1), (v2, rank2, count2) = _top16_by_value([s1, s2])
        else:
            (v1, rank1, count1), (v2, rank2, count2) = _top16_by_index(s1), _top16_by_index(s2)
        miscounts.append(jnp.maximum(jnp.abs(count1 - want), jnp.abs(count2 - want)))
        cs, real = _candidate_table(v1, v2)
        per_head.append((h, s1, s2, v1, v2, rank1, rank2, cs))

    if by_value:
        cur = [ph[-1] for ph in per_head]
        for _ in range(PEER_TOPK):
            cur = [jnp.where(c == jnp.max(c, axis=0, keepdims=True), -jnp.inf, c) for c in cur]
        sels = [c == -jnp.inf for c in cur]
    else:
        sels = []
        for ph in per_head:
            cs = ph[-1]
            c_i = lax.broadcasted_iota(I32, cs.shape, 0)
            cnt = jnp.zeros(cs.shape, F32)
            base = 0
            for g, n_real in enumerate(real):
                for c in range(base, base + n_real):
                    row = cs[c:c + 1, :]
                    beats = (row > cs) | ((row == cs) & (c_i > c))
                    cnt = cnt + jnp.where(beats, 1.0, 0.0)
                base += PEER_TOPK if g == 0 else SUBLANES
            sels.append(cnt < want)

    for i, ((h, s1, s2, v1, v2, rank1, rank2, cs), sel) in enumerate(zip(per_head, sels)):
        self_f = jnp.where(sel, 1.0, 0.0)
        miscounts[i] = jnp.maximum(miscounts[i], jnp.abs(jnp.sum(self_f, axis=0, keepdims=True) - want))
        top = v1[0:1, :] + v2[0:1, :]
        z = jnp.sum(jnp.where(sel, jnp.exp(cs - top), 0.0), axis=0, keepdims=True)
        nmap = jnp.zeros((PEER_KEYS, tn), F32)
        base = 0
        for a in range(PEER_TOPK):
            if a < SUBLANES:
                size = PEER_TOPK if a == 0 else SUBLANES
                n_a = jnp.sum(self_f[base:base + size, :], axis=0, keepdims=True)
                base += size
            else:
                n_a = self_f[base + a - SUBLANES:base + a - SUBLANES + 1, :]
            nmap = jnp.where(rank1 == float(a), n_a, nmap)
        n_ref[h] = nmap
        a_ref[h] = jnp.where(rank1 < want, jnp.exp(s1 - v1[0:1, :]), 0.0)
        r2_ref[h] = rank2.astype(BF16)
        b_ref[h] = (jnp.where(rank2 < want, jnp.exp(s2 - v2[0:1, :]), 0.0) / z).astype(BF16)
    return miscounts


def _route_kernel(hb_ref, wqt_ref, sk1_ref, sk2_ref, n_ref, a_ref, r2_ref, b_ref, *, tn):
    scores = functools.partial(_peer_scores, hb_ref=hb_ref, wqt_ref=wqt_ref, sk1_ref=sk1_ref, sk2_ref=sk2_ref)
    outs = (n_ref, a_ref, r2_ref, b_ref)
    groups = [range(h0, h0 + ROUTE_HEAD_GROUP) for h0 in range(0, PEER_HEADS, ROUTE_HEAD_GROUP)]
    miscounts = []
    ready = [scores(h) for h in groups[0]]
    for gi, heads in enumerate(groups):
        ahead = [scores(h) for h in groups[gi + 1]] if gi + 1 < len(groups) else None
        miscounts += _route_heads(heads, ready, outs, tn=tn, by_value=True)
        ready = ahead

    any_miscount = functools.reduce(jnp.maximum, miscounts)

    @pl.when(jnp.max(any_miscount) > 0.0)
    def _():
        for h in range(PEER_HEADS):
            @pl.when(jnp.max(miscounts[h]) > 0.0)
            def _(h=h):
                _route_heads([h], [scores(h)], outs, tn=tn, by_value=False)


def _route(hb, wq, subkeys):
    n = hb.shape[1]
    tn = LANES
    full = lambda *s: pl.BlockSpec(s, lambda i: (0,) * len(s))
    tab = lambda: pl.BlockSpec((PEER_HEADS, PEER_KEYS, tn), lambda i: (0, 0, i))
    shape = (PEER_HEADS, PEER_KEYS, n)
    return pl.pallas_call(
        functools.partial(_route_kernel, tn=tn),
        grid=(n // tn,),
        in_specs=[
            pl.BlockSpec((D_MODEL, tn), lambda i: (0, i)),
            full(PEER_HEADS * PEER_QDIM, D_MODEL), full(PEER_KEYS, PEER_QDIM // 2), full(PEER_KEYS, PEER_QDIM // 2),
        ],
        out_specs=[tab(), tab(), tab(), tab()],
        out_shape=[
            jax.ShapeDtypeStruct(shape, F32),
            jax.ShapeDtypeStruct(shape, F32),
            jax.ShapeDtypeStruct(shape, BF16),
            jax.ShapeDtypeStruct(shape, BF16),
        ],
        compiler_params=pltpu.CompilerParams(vmem_limit_bytes=VMEM_LIMIT),
        name="peer_route",
    )(hb, wq.T.astype(BF16), subkeys[0].astype(BF16), subkeys[1].astype(BF16))


def _experts_kernel(hb_ref, u_ref, vt_ref, n_ref, a_ref, r2_ref, b_ref, h_ref, g_ref, bb_ref,
                    out_ref, acc_ref, rows_ref, *, tn, te):
    j = pl.program_id(1)
    keys_per_step = te // PEER_KEYS

    def stage_rows(tile, slot):
        for c in range(keys_per_step):
            for h in range(PEER_HEADS):
                for k, ref in enumerate((n_ref, a_ref)):
                    row = ref[h, pl.ds(tile * keys_per_step + c, 1), :]
                    rows_ref[slot, k, c, h] = jnp.broadcast_to(row, (BF16_ROWS, tn)).astype(BF16)

    @pl.when(j == 0)
    def _():
        acc_ref[...] = jnp.zeros_like(acc_ref)
        stage_rows(0, 0)

    slot = j & 1
    hb = hb_ref[...]
    part = te // EXPERT_PARTS
    acts = [_dot(u_ref[i * part:(i + 1) * part, :], hb) for i in range(EXPERT_PARTS)]
    chunks = part // PEER_KEYS
    tiles = PEER_KEYS // BF16_ROWS
    zero = jnp.zeros((tiles, BF16_ROWS, tn), BF16)

    for i in range(EXPERT_PARTS):
        act = _gelu(acts[i])
        ps = []
        for c in range(chunks):
            key = i * chunks + c
            w = zero
            for h in range(PEER_HEADS):
                picked = jnp.where(r2_ref[h] < rows_ref[slot, 0, key, h][None], b_ref[h], zero)
                w = w + rows_ref[slot, 1, key, h][None] * picked
            ps.append(w.reshape(PEER_KEYS, tn) * act[c * PEER_KEYS:(c + 1) * PEER_KEYS].astype(BF16))
        acc_ref[...] += _dot(vt_ref[:, i * part:(i + 1) * part], jnp.concatenate(ps, axis=0))

    stage_rows(jnp.minimum(j + 1, pl.num_programs(1) - 1), 1 - slot)

    @pl.when(j == pl.num_programs(1) - 1)
    def _():
        h = ALPHA * h_ref[...] + acc_ref[...]
        out_ref[...] = _layer_norm_fm(h, g_ref, bb_ref, tn // LANES).T


def _experts(hb, hf, u_b, vt_b, tabs, ln_g, ln_b):
    n = hb.shape[1]
    tn, te = 512, 1024
    tab = lambda: pl.BlockSpec((PEER_HEADS, PEER_KEYS, tn), lambda i, j: (0, 0, i))
    tiles = PEER_KEYS // BF16_ROWS
    tiled = lambda: pl.BlockSpec((PEER_HEADS, tiles, BF16_ROWS, tn), lambda i, j: (0, 0, 0, i))
    n_tab, a_tab, r2_tab, b_tab = tabs
    tabs = (n_tab, a_tab, r2_tab.reshape(PEER_HEADS, tiles, BF16_ROWS, n), b_tab.reshape(PEER_HEADS, tiles, BF16_ROWS, n))
    return pl.pallas_call(
        functools.partial(_experts_kernel, tn=tn, te=te),
        grid=(n // tn, PEER_EXPERTS // te),
        in_specs=[
            pl.BlockSpec((D_MODEL, tn), lambda i, j: (0, i)),
            pl.BlockSpec((te, D_MODEL), lambda i, j: (j, 0)),
            pl.BlockSpec((D_MODEL, te), lambda i, j: (0, j)),
            tab(), tab(), tiled(), tiled(),
            pl.BlockSpec((D_MODEL, tn), lambda i, j: (0, i)),
            pl.BlockSpec((D_MODEL, LANES), lambda i, j: (0, 0)),
            pl.BlockSpec((D_MODEL, LANES), lambda i, j: (0, 0)),
        ],
        out_specs=pl.BlockSpec((tn, D_MODEL), lambda i, j: (i, 0)),
        out_shape=jax.ShapeDtypeStruct((n, D_MODEL), F32),
        scratch_shapes=[pltpu.VMEM((D_MODEL, tn), F32),
                        pltpu.VMEM((2, 2, te // PEER_KEYS, PEER_HEADS, BF16_ROWS, tn), BF16)],
        compiler_params=pltpu.CompilerParams(
            dimension_semantics=("arbitrary", "arbitrary"), vmem_limit_bytes=VMEM_LIMIT),
        name="peer_experts",
    )(hb, u_b, vt_b, *tabs, hf, _lane_bcast(ln_g), _lane_bcast(ln_b))


def kernel(x, w_in, rel_table, cmp_pe_k, cmp_pe_v, cmp_k_w1, cmp_k_w2, cmp_v_w1, cmp_v_w2, pool_w, pool_scale,
           w_out, ln1_g, ln1_b, peer_wq, peer_subkeys, peer_u, peer_v, ln2_g, ln2_b):
    batch, seq, _ = x.shape
    assert w_in.shape[0] == DEPTH == 1 and seq % 512 == 0 and seq // CMP_STRIDE <= 256
    x2 = x.reshape(batch * seq, D_MODEL)
    et, xs, xw = _bias_tables(rel_table)
    qt, vt, gt, zt, kvc, kslc, kwin = _inproj(x2, w_in[0], seq)
    kcmp, vcmpt = _compress(kvc, batch, seq, cmp_pe_k[0], cmp_pe_v[0], cmp_k_w1[0], cmp_k_w2[0],
                            cmp_v_w1[0], cmp_v_w2[0])
    nsat = _nsa(qt, kcmp, vcmpt, kslc, kwin, vt, gt, et, xs, xw, batch, seq)
    hf, hb = _mix(nsat, zt, x2, w_out[0], pool_w[0], pool_scale[0], ln1_g[0], ln1_b[0], seq)
    tabs = _route(hb, peer_wq[0], peer_subkeys[0])
    y = _experts(hb, hf, peer_u[0].astype(BF16), peer_v[0].T.astype(BF16), tabs, ln2_g[0], ln2_b[0])
    return y.reshape(batch, seq, D_MODEL)
```

```python
import functools
import math

import numpy as np
import jax
import jax.numpy as jnp
from jax import lax
from jax.experimental import pallas as pl
from jax.experimental.pallas import tpu as pltpu

F32 = jnp.float32
BF16 = jnp.bfloat16
I32 = jnp.int32

D_MODEL = 1024
NSA_HEADS = 8
KV_HEADS = 2
GQA = NSA_HEADS // KV_HEADS
HEAD_DIM = 64
NSA_WIDTH = NSA_HEADS * HEAD_DIM
KV_WIDTH = KV_HEADS * HEAD_DIM
POOL_WIDTH = D_MODEL - NSA_WIDTH
POOL_WINDOWS = (2, 4, 8, 16)
POOL_GROUP_DIM = POOL_WIDTH // len(POOL_WINDOWS)
N_GATES = 3 * NSA_HEADS
GATE_ROWS = 32
CMP_BLOCK = 32
CMP_STRIDE = 16
CMP_HIDDEN = 256
SLC_BLOCK = 64
SLC_COUNT = 16
WINDOW = 512
Q_BLOCK = 128
NEG = -1e30
FORCE = 1e6
N_BUCKETS = 32
MAX_DISTANCE = 128
PEER_HEADS = 8
PEER_KEYS = 128
PEER_EXPERTS = PEER_KEYS * PEER_KEYS
PEER_QDIM = 256
PEER_TOPK = 16
DEPTH = 1
ALPHA = (2 * DEPTH) ** 0.25
LN_EPS = 1e-5
SQRT_HALF = 0.7071067811865476

LANES = 128
SUBLANES = 8
BF16_ROWS = SUBLANES
EXPERT_PARTS = 2
VMEM_LIMIT = 52 * 1024 * 1024

_FM_Q, _FM_V, _FM_G, _FM_Z, _FM_END = 0, 512, 768, 800, 1312
_TM_WIDTH = 640


def _gelu(x):
    return 0.5 * x * (1.0 + lax.erf(x * SQRT_HALF))


def _dot(a, b):
    return jnp.dot(a, b, preferred_element_type=F32)


def _dot_nt(a, b):
    return lax.dot_general(a, b, (((1,), (1,)), ((), ())), preferred_element_type=F32)


def _bucket_np(dist):
    dist = np.maximum(dist, 0)
    max_exact = N_BUCKETS // 2
    large = max_exact + np.floor(
        np.log(np.maximum(dist, 1) / max_exact) / math.log(MAX_DISTANCE / max_exact)
        * (N_BUCKETS - max_exact)).astype(np.int64)
    large = np.minimum(large, N_BUCKETS - 1)
    return np.where(dist < max_exact, dist, large)


ET_ROWS = 512
XS_ROWS = 896
XS_MASKED_ROW = 640
XW_ROWS = 1152
_CODE_MASKED = N_BUCKETS


def _bias_index_tables():
    qi = np.arange(Q_BLOCK)[None, :]

    def codes(dist, visible):
        return np.where(visible, _bucket_np(dist), _CODE_MASKED).astype(np.int32)

    d_cmp = qi - CMP_STRIDE * (np.arange(ET_ROWS)[:, None] - 256) - (CMP_BLOCK - 1)
    rho = np.arange(XS_ROWS)[:, None]
    d_sel = qi - (rho - 384)
    d_win = qi - (np.arange(XW_ROWS)[:, None] - 512)
    return np.concatenate([
        codes(d_cmp, d_cmp >= 0),
        codes(d_sel, (d_sel >= 0) & (rho < XS_MASKED_ROW)),
        codes(d_win, (d_win >= 0) & (d_win < WINDOW)),
    ], axis=0)


def _overlap_t(n_cmp_rows, n_slc):
    n = np.arange(n_cmp_rows)[None, :]
    j = np.arange(n_slc)[:, None]
    start = n * CMP_STRIDE
    end = start + CMP_BLOCK - 1
    return ((start <= j * SLC_BLOCK + SLC_BLOCK - 1) & (end >= j * SLC_BLOCK)).astype(np.float32)


def _bias_kernel(tab_ref, idx_ref, et_ref, xs_ref, xw_ref):
    g = pl.program_id(0)
    chunk = Q_BLOCK
    for r in range(GQA):
        h = g * GQA + r
        far = tab_ref[N_BUCKETS - 1, h]
        cols = slice(r * LANES, (r + 1) * LANES)
        for c in range((ET_ROWS + XS_ROWS + XW_ROWS) // chunk):
            idx = idx_ref[c * chunk:(c + 1) * chunk, :]
            acc = jnp.where(idx == _CODE_MASKED, NEG, 0.0)
            for b in range(N_BUCKETS - 1):
                acc = jnp.where(idx == b, tab_ref[b, h] - far, acc)
            row = c * chunk
            if row < ET_ROWS:
                et_ref[r, row:row + chunk, :] = acc
            elif row < ET_ROWS + XS_ROWS:
                xs_ref[0, row - ET_ROWS:row - ET_ROWS + chunk, cols] = acc
            else:
                row -= ET_ROWS + XS_ROWS
                xw_ref[0, row:row + chunk, cols] = acc


def _bias_tables(rel_table):
    idx = _bias_index_tables()
    qw = GQA * LANES
    return pl.pallas_call(
        _bias_kernel,
        grid=(KV_HEADS,),
        in_specs=[
            pl.BlockSpec(memory_space=pltpu.SMEM),
            pl.BlockSpec(idx.shape, lambda g: (0, 0)),
        ],
        out_specs=[
            pl.BlockSpec((GQA, ET_ROWS, Q_BLOCK), lambda g: (g, 0, 0)),
            pl.BlockSpec((1, XS_ROWS, qw), lambda g: (g, 0, 0)),
            pl.BlockSpec((1, XW_ROWS, qw), lambda g: (g, 0, 0)),
        ],
        out_shape=[
            jax.ShapeDtypeStruct((NSA_HEADS, ET_ROWS, Q_BLOCK), F32),
            jax.ShapeDtypeStruct((KV_HEADS, XS_ROWS, qw), F32),
            jax.ShapeDtypeStruct((KV_HEADS, XW_ROWS, qw), F32),
        ],
        name="bias_tables",
    )(rel_table, jnp.asarray(idx))


def _inproj_kernel(x_ref, wfm_ref, wtm_ref, qt_ref, vt_ref, gt_ref, zt_ref, kvc_ref,
                   kslc_ref, kwin_ref, *, tn, seq):
    xb = x_ref[...].astype(BF16)
    fm = _dot_nt(wfm_ref[...], xb)
    tm = _dot(xb, wtm_ref[...])
    qt_ref[...] = (fm[_FM_Q:_FM_V] * (HEAD_DIM ** -0.5)).astype(BF16)
    for c in range(2 * KV_HEADS):
        for s in range(tn // LANES):
            vt_ref[c, s] = fm[_FM_V + c * HEAD_DIM:_FM_V + (c + 1) * HEAD_DIM,
                              s * LANES:(s + 1) * LANES].astype(BF16)
    gt_ref[...] = jax.nn.sigmoid(fm[_FM_G:_FM_Z])
    zt_ref[...] = fm[_FM_Z:_FM_END]
    kvc_ref[0] = tm[:, 0:KV_WIDTH]
    kvc_ref[1] = tm[:, KV_WIDTH:2 * KV_WIDTH]
    t0 = (pl.program_id(0) * tn) % seq
    row = lax.broadcasted_iota(I32, (tn, LANES), 0) + t0
    lane = lax.broadcasted_iota(I32, (tn, LANES), 1)
    onehot = jnp.where(lane - HEAD_DIM == (row >> 6), 1.0, 0.0)
    for g in range(KV_HEADS):
        k = tm[:, 256 + g * LANES:256 + (g + 1) * LANES]
        kslc_ref[g] = jnp.where(lane < HEAD_DIM, k, onehot).astype(BF16)
    kwin_ref[...] = tm[:, 512:640].astype(BF16)


def _inproj(x2, w_in, seq):
    n = x2.shape[0]
    tn = 512
    w = w_in
    o = NSA_WIDTH
    w_q = w[:, 0:o]
    w_kc, w_vc, w_ks, w_vs, w_kw, w_vw = [w[:, o + i * KV_WIDTH:o + (i + 1) * KV_WIDTH] for i in range(6)]
    o += 6 * KV_WIDTH
    w_g = w[:, o:o + N_GATES]
    w_z = w[:, o + N_GATES:]
    zpad = jnp.zeros((D_MODEL, HEAD_DIM), w.dtype)
    wfm = jnp.concatenate(
        [w_q, w_vs, w_vw, w_g, jnp.zeros((D_MODEL, GATE_ROWS - N_GATES), w.dtype), w_z], axis=1).T.astype(BF16)
    wtm = jnp.concatenate(
        [w_kc, w_vc, w_ks[:, :HEAD_DIM], zpad, w_ks[:, HEAD_DIM:], zpad, w_kw], axis=1).astype(BF16)
    nt = n // LANES
    return pl.pallas_call(
        functools.partial(_inproj_kernel, tn=tn, seq=seq),
        grid=(n // tn,),
        in_specs=[
            pl.BlockSpec((tn, D_MODEL), lambda i: (i, 0)),
            pl.BlockSpec((_FM_END, D_MODEL), lambda i: (0, 0)),
            pl.BlockSpec((D_MODEL, _TM_WIDTH), lambda i: (0, 0)),
        ],
        out_specs=[
            pl.BlockSpec((NSA_WIDTH, tn), lambda i: (0, i)),
            pl.BlockSpec((2 * KV_HEADS, tn // LANES, HEAD_DIM, LANES), lambda i: (0, i, 0, 0)),
            pl.BlockSpec((GATE_ROWS, tn), lambda i: (0, i)),
            pl.BlockSpec((POOL_WIDTH, tn), lambda i: (0, i)),
            pl.BlockSpec((2, tn, KV_WIDTH), lambda i: (0, i, 0)),
            pl.BlockSpec((KV_HEADS, tn, LANES), lambda i: (0, i, 0)),
            pl.BlockSpec((tn, LANES), lambda i: (i, 0)),
        ],
        out_shape=[
            jax.ShapeDtypeStruct((NSA_WIDTH, n), BF16),
            jax.ShapeDtypeStruct((2 * KV_HEADS, nt, HEAD_DIM, LANES), BF16),
            jax.ShapeDtypeStruct((GATE_ROWS, n), F32),
            jax.ShapeDtypeStruct((POOL_WIDTH, n), F32),
            jax.ShapeDtypeStruct((2, n, KV_WIDTH), F32),
            jax.ShapeDtypeStruct((KV_HEADS, n, LANES), BF16),
            jax.ShapeDtypeStruct((n, LANES), BF16),
        ],
        compiler_params=pltpu.CompilerParams(vmem_limit_bytes=VMEM_LIMIT),
        name="inproj",
    )(x2, wfm, wtm)


def _compress_kernel(kc_ref, vc_ref, w2k_ref, w2v_ref, pek_ref, pev_ref, w1k_ref, w1v_ref,
                     wok0_ref, wok1_ref, wovt_ref, kcmp_ref, vcmpt_ref, *, n_rows):
    half = CMP_BLOCK // 2
    acc_k = jnp.zeros((n_rows, 4 * CMP_HIDDEN), F32)
    acc_v = jnp.zeros((n_rows, 4 * CMP_HIDDEN), F32)
    for l in range(half):
        rows = pl.ds(l, n_rows, stride=CMP_STRIDE)
        acc_k = acc_k + _dot(kc_ref[rows, :].astype(BF16), w2k_ref[l])
        acc_v = acc_v + _dot(vc_ref[rows, :].astype(BF16), w2v_ref[l])
    ck = _dot(pek_ref[...], w1k_ref[...])[0:1]
    cv = _dot(pev_ref[...], w1v_ref[...])[0:1]

    def hidden(acc, c, g):
        a = acc[:, g * CMP_HIDDEN:(g + 1) * CMP_HIDDEN]
        b = acc[:, (2 + g) * CMP_HIDDEN:(3 + g) * CMP_HIDDEN]
        return _gelu(a + pltpu.roll(b, n_rows - 1, axis=0) + c).astype(BF16)

    kcmp_ref[...] = (_dot(hidden(acc_k, ck, 0), wok0_ref[...])
                     + _dot(hidden(acc_k, ck, 1), wok1_ref[...])).astype(BF16)
    for g in range(KV_HEADS):
        vcmpt_ref[g] = _dot_nt(wovt_ref[...], hidden(acc_v, cv, g)).astype(BF16)


def _compress(kvc, batch, seq, pe_k, pe_v, k_w1, k_w2, v_w1, v_w2):
    n_rows = seq // CMP_STRIDE
    half = CMP_BLOCK // 2

    def blockdiag(w1):
        w = w1.reshape(CMP_BLOCK, HEAD_DIM, CMP_HIDDEN)
        z = jnp.zeros((half, HEAD_DIM, CMP_HIDDEN), w1.dtype)
        top = jnp.concatenate([w[:half], z, w[half:], z], axis=2)
        bot = jnp.concatenate([z, w[:half], z, w[half:]], axis=2)
        return jnp.concatenate([top, bot], axis=1).astype(BF16)

    def pe_rows(pe):
        flat = pe.reshape(1, CMP_BLOCK * HEAD_DIM)
        return jnp.concatenate([flat, jnp.zeros((7, flat.shape[1]), pe.dtype)], axis=0).astype(BF16)

    zo = jnp.zeros((CMP_HIDDEN, HEAD_DIM), k_w2.dtype)
    wok0 = jnp.concatenate([k_w2, zo], axis=1).astype(BF16)
    wok1 = jnp.concatenate([zo, k_w2], axis=1).astype(BF16)
    full = lambda *s: pl.BlockSpec(s, lambda b: (0,) * len(s))
    return pl.pallas_call(
        functools.partial(_compress_kernel, n_rows=n_rows),
        grid=(batch,),
        in_specs=[
            pl.BlockSpec((None, seq, KV_WIDTH), lambda b: (0, b, 0)),
            pl.BlockSpec((None, seq, KV_WIDTH), lambda b: (1, b, 0)),
            full(half, KV_WIDTH, 4 * CMP_HIDDEN), full(half, KV_WIDTH, 4 * CMP_HIDDEN),
            full(8, CMP_BLOCK * HEAD_DIM), full(8, CMP_BLOCK * HEAD_DIM),
            full(CMP_BLOCK * HEAD_DIM, CMP_HIDDEN), full(CMP_BLOCK * HEAD_DIM, CMP_HIDDEN),
            full(CMP_HIDDEN, KV_WIDTH), full(CMP_HIDDEN, KV_WIDTH), full(HEAD_DIM, CMP_HIDDEN),
        ],
        out_specs=[
            pl.BlockSpec((None, n_rows, KV_WIDTH), lambda b: (b, 0, 0)),
            pl.BlockSpec((None, KV_HEADS, HEAD_DIM, n_rows), lambda b: (b, 0, 0, 0)),
        ],
        out_shape=[
            jax.ShapeDtypeStruct((batch, n_rows, KV_WIDTH), BF16),
            jax.ShapeDtypeStruct((batch, KV_HEADS, HEAD_DIM, n_rows), BF16),
        ],
        compiler_params=pltpu.CompilerParams(vmem_limit_bytes=VMEM_LIMIT),
        name="compress",
    )(kvc, kvc, blockdiag(k_w1), blockdiag(v_w1), pe_rows(pe_k), pe_rows(pe_v),
      k_w1.astype(BF16), v_w1.astype(BF16), wok0, wok1, v_w2.T.astype(BF16))


SUPER_TILE = 2 * Q_BLOCK
WIN_KEYS = WINDOW + Q_BLOCK


def _softmax_init(s, vt):
    m = jnp.max(s, axis=0, keepdims=True)
    p = jnp.exp(s - m)
    return m, jnp.sum(p, axis=0, keepdims=True), _dot(vt, p.astype(BF16))


def _softmax_step(s, vt, state):
    m, l, acc = state
    m_new = jnp.maximum(m, jnp.max(s, axis=0, keepdims=True))
    alpha = jnp.exp(m - m_new)
    p = jnp.exp(s - m_new)
    return m_new, alpha * l + jnp.sum(p, axis=0, keepdims=True), alpha * acc + _dot(vt, p.astype(BF16))


def _softmax_merge(a, b):
    m = jnp.maximum(a[0], b[0])
    wa = jnp.exp(a[0] - m)
    wb = jnp.exp(b[0] - m)
    return (wa * a[2] + wb * b[2]) * (1.0 / (wa * a[1] + wb * b[1]))


def _nsa_kernel(qt_ref, kcmp_ref, vcmpt_ref, kslc_ref, kwin_ref, vslct_ref, vwint_ref, gt_ref,
                et_ref, xs_ref, xw_ref, ovt_ref, out_ref, *, n_cmp_rows, n_slc):
    qb = pl.program_id(1)
    qw = GQA * Q_BLOCK
    odd = qb & 1
    st_diag = qb >> 1
    st_prev = jnp.maximum(st_diag - 1, 0)
    n_far = st_prev
    row_a = pl.multiple_of(384 - Q_BLOCK * odd, Q_BLOCK)
    row_b = pl.multiple_of(jnp.where(st_diag >= 1, Q_BLOCK - Q_BLOCK * odd, XS_MASKED_ROW), Q_BLOCK)
    k0 = jnp.maximum(qb * Q_BLOCK - WINDOW, 0)
    row_w = pl.multiple_of(WINDOW - qb * Q_BLOCK + k0, Q_BLOCK)
    kt0 = k0 >> 7
    off = pl.multiple_of(256 - 8 * qb, 8)
    j_i = lax.broadcasted_iota(I32, (n_slc, Q_BLOCK), 0)
    cur = (qb * Q_BLOCK + lax.broadcasted_iota(I32, (n_slc, Q_BLOCK), 1)) >> 6
    forced = (j_i == 0) | (j_i == cur) | (j_i == cur - 1)

    def sel_scores(g, qsel, st):
        k = kslc_ref[g, pl.ds(pl.multiple_of(st * SUPER_TILE, SUPER_TILE), SUPER_TILE), :]
        vt = jnp.concatenate([vslct_ref[g, 2 * st], vslct_ref[g, 2 * st + 1]], axis=1)
        return _dot(k, qsel), vt

    def near(g):
        q4 = qt_ref[g * GQA * HEAD_DIM:(g + 1) * GQA * HEAD_DIM, :]
        qcat = jnp.concatenate([q4[r * HEAD_DIM:(r + 1) * HEAD_DIM] for r in range(GQA)], axis=1)
        zq = jnp.zeros_like(qcat)
        qg = jnp.concatenate([qcat, zq] if g == 0 else [zq, qcat], axis=0)

        s = _dot(kcmp_ref[...], qg)
        s = s + jnp.concatenate([et_ref[g * GQA + r, pl.ds(off, n_cmp_rows), :] for r in range(GQA)], axis=1)
        m = jnp.max(s, axis=0, keepdims=True)
        p = jnp.exp(s - m)
        l = jnp.sum(p, axis=0, keepdims=True)
        p = p * jnp.where(m > 0.5 * NEG, 1.0 / l, 0.0)
        o_cmp = _dot(vcmpt_ref[g], p.astype(BF16))
        p_sum = p[:, 0:Q_BLOCK]
        for r in range(1, GQA):
            p_sum = p_sum + p[:, r * Q_BLOCK:(r + 1) * Q_BLOCK]
        imp = jnp.dot(ovt_ref[...], p_sum, preferred_element_type=F32,
                      precision=lax.Precision.HIGHEST)

        s_win = _dot(kwin_ref[pl.ds(pl.multiple_of(k0, Q_BLOCK), WIN_KEYS), :], qg)
        vt = jnp.concatenate([vwint_ref[g, kt0 + i] for i in range(WIN_KEYS // Q_BLOCK)], axis=1)
        m, l, acc = _softmax_init(s_win + xw_ref[g, pl.ds(row_w, WIN_KEYS), :], vt)
        o_win = acc * (1.0 / l)

        imp = jnp.where(forced, FORCE, imp)
        imp = jnp.where(j_i <= cur, imp, NEG)
        cnt = jnp.zeros((n_slc, Q_BLOCK), F32)
        for jp in range(n_slc):
            row = imp[jp:jp + 1, :]
            beats = (row > imp) | ((row == imp) & (j_i > jp))
            cnt = cnt + jnp.where(beats, 1.0, 0.0)
        selneg = jnp.where(cnt < float(SLC_COUNT), 0.0, NEG).astype(BF16)
        qsel = [qcat, jnp.concatenate([selneg] * GQA, axis=1)]
        if n_slc < LANES - HEAD_DIM:
            qsel.append(jnp.zeros((LANES - HEAD_DIM - n_slc, qw), BF16))
        qsel = jnp.concatenate(qsel, axis=0)

        s_a, vt_a = sel_scores(g, qsel, st_diag)
        s_b, vt_b = sel_scores(g, qsel, st_prev)
        chain_a = _softmax_init(s_a + xs_ref[g, pl.ds(row_a, SUPER_TILE), :], vt_a)
        chain_b = _softmax_init(s_b + xs_ref[g, pl.ds(row_b, SUPER_TILE), :], vt_b)
        return o_cmp, o_win, qsel, (chain_a, chain_b)

    heads = [near(g) for g in range(KV_HEADS)]
    qsels = [h[2] for h in heads]

    def far_steps(first, count, chains):
        scored = [[sel_scores(g, qsels[g], first + k) for k in range(count)] for g in range(KV_HEADS)]
        chains = [list(c) for c in chains]
        for k in range(count):
            for g in range(KV_HEADS):
                s, vt = scored[g][k]
                chains[g][k % 2] = _softmax_step(s, vt, chains[g][k % 2])
        return tuple(tuple(c) for c in chains)

    chains = tuple(h[3] for h in heads)
    chains = lax.fori_loop(0, n_far >> 2, lambda i, c: far_steps(4 * i, 4, c), chains)
    rest = n_far & ~3
    chains = lax.cond((n_far & 2) != 0, lambda c: far_steps(rest, 2, c), lambda c: c, chains)
    chains = lax.cond((n_far & 1) != 0, lambda c: far_steps(n_far - 1, 1, c), lambda c: c, chains)

    outs = []
    for g in range(KV_HEADS):
        o_cmp, o_win = heads[g][0], heads[g][1]
        o_slc = _softmax_merge(*chains[g])
        for r in range(GQA):
            cols = slice(r * Q_BLOCK, (r + 1) * Q_BLOCK)
            gate = [gt_ref[c * NSA_HEADS + g * GQA + r:c * NSA_HEADS + g * GQA + r + 1, :] for c in range(3)]
            outs.append(gate[0] * o_cmp[:, cols] + gate[1] * o_slc[:, cols] + gate[2] * o_win[:, cols])
    out_ref[...] = jnp.concatenate(outs, axis=0).astype(BF16)


def _nsa(qt, kcmp, vcmpt, kslc, kwin, vt, gt, et, xs, xw, batch, seq):
    assert seq % SUPER_TILE == 0 and seq >= WIN_KEYS and seq // CMP_STRIDE <= 256 and seq // SLC_BLOCK <= HEAD_DIM
    nq = seq // Q_BLOCK
    n_cmp_rows = seq // CMP_STRIDE
    n_slc = seq // SLC_BLOCK
    n = batch * seq
    ovt = jnp.asarray(_overlap_t(n_cmp_rows, n_slc))
    qw = GQA * LANES
    return pl.pallas_call(
        functools.partial(_nsa_kernel, n_cmp_rows=n_cmp_rows, n_slc=n_slc),
        grid=(batch, nq),
        in_specs=[
            pl.BlockSpec((NSA_WIDTH, Q_BLOCK), lambda b, q: (0, b * nq + q)),
            pl.BlockSpec((None, n_cmp_rows, KV_WIDTH), lambda b, q: (b, 0, 0)),
            pl.BlockSpec((None, KV_HEADS, HEAD_DIM, n_cmp_rows), lambda b, q: (b, 0, 0, 0)),
            pl.BlockSpec((KV_HEADS, seq, LANES), lambda b, q: (0, b, 0)),
            pl.BlockSpec((seq, LANES), lambda b, q: (b, 0)),
            pl.BlockSpec((KV_HEADS, nq, HEAD_DIM, LANES), lambda b, q: (0, b, 0, 0)),
            pl.BlockSpec((KV_HEADS, nq, HEAD_DIM, LANES), lambda b, q: (1, b, 0, 0)),
            pl.BlockSpec((GATE_ROWS, Q_BLOCK), lambda b, q: (0, b * nq + q)),
            pl.BlockSpec((NSA_HEADS, ET_ROWS, Q_BLOCK), lambda b, q: (0, 0, 0)),
            pl.BlockSpec((KV_HEADS, XS_ROWS, qw), lambda b, q: (0, 0, 0)),
            pl.BlockSpec((KV_HEADS, XW_ROWS, qw), lambda b, q: (0, 0, 0)),
            pl.BlockSpec((n_slc, n_cmp_rows), lambda b, q: (0, 0)),
        ],
        out_specs=pl.BlockSpec((NSA_WIDTH, Q_BLOCK), lambda b, q: (0, b * nq + q)),
        out_shape=jax.ShapeDtypeStruct((NSA_WIDTH, n), BF16),
        compiler_params=pltpu.CompilerParams(vmem_limit_bytes=VMEM_LIMIT),
        name="nsa",
    )(qt, kcmp, vcmpt, kslc, kwin, vt, vt, gt, et, xs, xw, ovt)


def _layer_norm_fm(h, g_ref, b_ref, reps):
    mu = jnp.mean(h, axis=0, keepdims=True)
    hc = h - mu
    var = jnp.mean(hc * hc, axis=0, keepdims=True)
    gain = jnp.concatenate([g_ref[...]] * reps, axis=1)
    bias = jnp.concatenate([b_ref[...]] * reps, axis=1)
    return hc * lax.rsqrt(var + LN_EPS) * gain + bias


def _mix_kernel(nsat_ref, zt_ref, halo_ref, x_ref, won_ref, wop_ref, pwt_ref, ps_ref, g_ref, b_ref,
                h_ref, hb_ref, *, tn, seq):
    t0 = (pl.program_id(0) * tn) % seq
    reps = tn // LANES
    zc = jnp.concatenate([halo_ref[...], zt_ref[...]], axis=1)
    t_ext = t0 - LANES + lax.broadcasted_iota(I32, (1, LANES + tn), 1)
    zc = jnp.where(t_ext >= 0, zc, 0.0)
    pooled = []
    for gi, win in enumerate(POOL_WINDOWS):
        zg = zc[gi * POOL_GROUP_DIM:(gi + 1) * POOL_GROUP_DIM]
        s = zg
        sh = 1
        while sh < win:
            s = s + pltpu.roll(s, sh, axis=1)
            sh *= 2
        cnt = jnp.clip(t_ext + 1, 1, win).astype(F32)
        y = (s / cnt - zg)[:, LANES:]
        scale = jnp.concatenate([ps_ref[gi * POOL_GROUP_DIM:(gi + 1) * POOL_GROUP_DIM]] * reps, axis=1)
        pooled.append((_dot(pwt_ref[gi], y.astype(BF16)) * scale).astype(BF16))
    pool_t = jnp.concatenate(pooled, axis=0)
    mix = _dot(won_ref[...], nsat_ref[...]) + _dot(wop_ref[...], pool_t)
    h = ALPHA * x_ref[...].T + mix
    h = _layer_norm_fm(h, g_ref, b_ref, reps)
    h_ref[...] = h
    hb_ref[...] = h.astype(BF16)


def _lane_bcast(v):
    return jnp.broadcast_to(v.reshape(-1, 1), (v.shape[0], LANES)).astype(F32)


def _mix(nsat, zt, x2, w_out, pool_w, pool_scale, ln_g, ln_b, seq):
    n = x2.shape[0]
    tn = 256
    wot = w_out.T.astype(BF16)
    full = lambda *s: pl.BlockSpec(s, lambda i: (0,) * len(s))
    return pl.pallas_call(
        functools.partial(_mix_kernel, tn=tn, seq=seq),
        grid=(n // tn,),
        in_specs=[
            pl.BlockSpec((NSA_WIDTH, tn), lambda i: (0, i)),
            pl.BlockSpec((POOL_WIDTH, tn), lambda i: (0, i)),
            pl.BlockSpec((POOL_WIDTH, LANES), lambda i: (0, jnp.maximum(i * (tn // LANES) - 1, 0))),
            pl.BlockSpec((tn, D_MODEL), lambda i: (i, 0)),
            full(D_MODEL, NSA_WIDTH), full(D_MODEL, POOL_WIDTH),
            full(len(POOL_WINDOWS), POOL_GROUP_DIM, POOL_GROUP_DIM),
            full(POOL_WIDTH, LANES), full(D_MODEL, LANES), full(D_MODEL, LANES),
        ],
        out_specs=[pl.BlockSpec((D_MODEL, tn), lambda i: (0, i))] * 2,
        out_shape=[jax.ShapeDtypeStruct((D_MODEL, n), F32), jax.ShapeDtypeStruct((D_MODEL, n), BF16)],
        compiler_params=pltpu.CompilerParams(vmem_limit_bytes=VMEM_LIMIT),
        name="mix",
    )(nsat, zt, zt, x2, wot[:, :NSA_WIDTH], wot[:, NSA_WIDTH:],
      jnp.swapaxes(pool_w, 1, 2).astype(BF16), _lane_bcast(pool_scale), _lane_bcast(ln_g), _lane_bcast(ln_b))


ROUTE_HEAD_GROUP = 4
_REMOVED = -(2.0 ** 100)
_PAD_SCORE = -3e38


def _top16_by_index(s):
    rows = lax.broadcasted_iota(I32, s.shape, 0).astype(F32)
    row16 = lax.broadcasted_iota(I32, (PEER_TOPK, s.shape[1]), 0)
    cur = s
    rank = jnp.full(s.shape, float(PEER_TOPK), F32)
    vals = jnp.zeros((PEER_TOPK, s.shape[1]), F32)
    for r in range(PEER_TOPK):
        mx = jnp.max(cur, axis=0, keepdims=True)
        first = jnp.min(jnp.where(cur == mx, rows, float(s.shape[0])), axis=0, keepdims=True)
        hit = rows == first
        vals = jnp.where(row16 == r, mx, vals)
        rank = jnp.where(hit, float(r), rank)
        cur = jnp.where(hit, -jnp.inf, cur)
    return vals, rank, jnp.full((1, s.shape[1]), float(PEER_TOPK), F32)


def _top16_by_value(scores):
    width = scores[0].shape[1]
    row16 = lax.broadcasted_iota(I32, (PEER_TOPK, width), 0)
    cur = list(scores)
    vals = [jnp.zeros((PEER_TOPK, width), F32) for _ in scores]
    for r in range(PEER_TOPK):
        mx = [jnp.max(c, axis=0, keepdims=True) for c in cur]
        vals = [jnp.where(row16 == r, m, v) for m, v in zip(mx, vals)]
        cur = [jnp.where(c == m, _REMOVED * (r + 1), c) for c, m in zip(cur, mx)]
    out = []
    for c, v in zip(cur, vals):
        member = c <= _REMOVED
        rank = jnp.where(member, c * (1.0 / _REMOVED) - 1.0, float(PEER_TOPK))
        out.append((v, rank, jnp.sum(jnp.where(member, 1.0, 0.0), axis=0, keepdims=True)))
    return out


def _candidate_table(v1, v2):
    tn = v1.shape[1]
    row8 = lax.broadcasted_iota(I32, (SUBLANES, tn), 0)
    groups = [v1[0:1, :] + v2]
    real = [PEER_TOPK]
    for a in range(1, SUBLANES):
        nb = PEER_TOPK // (a + 1)
        groups.append(jnp.where(row8 < nb, v1[a:a + 1, :] + v2[0:SUBLANES, :], _PAD_SCORE))
        real.append(nb)
    groups.append(v1[SUBLANES:, :] + v2[0:1, :])
    real.append(PEER_TOPK - SUBLANES)
    return jnp.concatenate(groups, axis=0), real


def _peer_scores(h, cols, hb_ref, wqt_ref, sk1_ref, sk2_ref):
    half = PEER_QDIM // 2
    q = _dot(wqt_ref[h * PEER_QDIM:(h + 1) * PEER_QDIM, :], hb_ref[:, cols])
    return _dot(sk1_ref[...], q[:half].astype(BF16)), _dot(sk2_ref[...], q[half:].astype(BF16))


def _route_heads(heads, cols, scores, outs, *, by_value):
    n_ref, a_ref, r2_ref, b_ref = outs
    tn = LANES
    want = float(PEER_TOPK)
    miscounts = []
    per_head = []
    for h, (s1, s2) in zip(heads, scores):
        if by_value:
            (v1, rank1, count1), (v2, rank2, count2) = _top16_by_value([s1, s2])
        else:
            (v1, rank1, count1), (v2, rank2, count2) = _top16_by_index(s1), _top16_by_index(s2)
        miscounts.append(jnp.maximum(jnp.abs(count1 - want), jnp.abs(count2 - want)))
        cs, real = _candidate_table(v1, v2)
        per_head.append((h, s1, s2, v1, v2, rank1, rank2, cs))

    if by_value:
        cur = [ph[-1] for ph in per_head]
        for _ in range(PEER_TOPK):
            cur = [jnp.where(c == jnp.max(c, axis=0, keepdims=True), -jnp.inf, c) for c in cur]
        sels = [c == -jnp.inf for c in cur]
    else:
        sels = []
        for ph in per_head:
            cs = ph[-1]
            c_i = lax.broadcasted_iota(I32, cs.shape, 0)
            cnt = jnp.zeros(cs.shape, F32)
            base = 0
            for g, n_real in enumerate(real):
                for c in range(base, base + n_real):
                    row = cs[c:c + 1, :]
                    beats = (row > cs) | ((row == cs) & (c_i > c))
                    cnt = cnt + jnp.where(beats, 1.0, 0.0)
                base += PEER_TOPK if g == 0 else SUBLANES
            sels.append(cnt < want)

    for i, ((h, s1, s2, v1, v2, rank1, rank2, cs), sel) in enumerate(zip(per_head, sels)):
        self_f = jnp.where(sel, 1.0, 0.0)
        miscounts[i] = jnp.maximum(miscounts[i], jnp.abs(jnp.sum(self_f, axis=0, keepdims=True) - want))
        top = v1[0:1, :] + v2[0:1, :]
        z = jnp.sum(jnp.where(sel, jnp.exp(cs - top), 0.0), axis=0, keepdims=True)
        nmap = jnp.zeros((PEER_KEYS, tn), F32)
        base = 0
        for a in range(PEER_TOPK):
            if a < SUBLANES:
                size = PEER_TOPK if a == 0 else SUBLANES
                n_a = jnp.sum(self_f[base:base + size, :], axis=0, keepdims=True)
                base += size
            else:
                n_a = self_f[base + a - SUBLANES:base + a - SUBLANES + 1, :]
            nmap = jnp.where(rank1 == float(a), n_a, nmap)
        n_ref[h, :, cols] = nmap
        a_ref[h, :, cols] = jnp.where(rank1 < want, jnp.exp(s1 - v1[0:1, :]), 0.0)
        r2_ref[h, :, cols] = rank2.astype(BF16)
        b_ref[h, :, cols] = (jnp.where(rank2 < want, jnp.exp(s2 - v2[0:1, :]), 0.0) / z).astype(BF16)
    return miscounts


def _route_kernel(hb_ref, wqt_ref, sk1_ref, sk2_ref, n_ref, a_ref, r2_ref, b_ref, *, tn):
    scores = functools.partial(_peer_scores, hb_ref=hb_ref, wqt_ref=wqt_ref, sk1_ref=sk1_ref, sk2_ref=sk2_ref)
    outs = (n_ref, a_ref, r2_ref, b_ref)
    blocks = [slice(c * LANES, (c + 1) * LANES) for c in range(tn // LANES)]
    items = [(cols, range(h0, h0 + ROUTE_HEAD_GROUP)) for cols in blocks
             for h0 in range(0, PEER_HEADS, ROUTE_HEAD_GROUP)]
    miscounts = []
    ready = [scores(h, items[0][0]) for h in items[0][1]]
    for k, (cols, heads) in enumerate(items):
        ahead = [scores(h, items[k + 1][0]) for h in items[k + 1][1]] if k + 1 < len(items) else None
        miscounts += [(cols, h, m) for h, m in zip(heads, _route_heads(heads, cols, ready, outs, by_value=True))]
        ready = ahead

    any_miscount = functools.reduce(jnp.maximum, [m for _, _, m in miscounts])

    @pl.when(jnp.max(any_miscount) > 0.0)
    def _():
        for cols, h, m in miscounts:
            @pl.when(jnp.max(m) > 0.0)
            def _(cols=cols, h=h):
                _route_heads([h], cols, [scores(h, cols)], outs, by_value=False)


def _route(hb, wq, subkeys):
    n = hb.shape[1]
    tn = 2 * LANES
    full = lambda *s: pl.BlockSpec(s, lambda i: (0,) * len(s))
    tab = lambda: pl.BlockSpec((PEER_HEADS, PEER_KEYS, tn), lambda i: (0, 0, i))
    shape = (PEER_HEADS, PEER_KEYS, n)
    return pl.pallas_call(
        functools.partial(_route_kernel, tn=tn),
        grid=(n // tn,),
        in_specs=[
            pl.BlockSpec((D_MODEL, tn), lambda i: (0, i)),
            full(PEER_HEADS * PEER_QDIM, D_MODEL), full(PEER_KEYS, PEER_QDIM // 2), full(PEER_KEYS, PEER_QDIM // 2),
        ],
        out_specs=[tab(), tab(), tab(), tab()],
        out_shape=[
            jax.ShapeDtypeStruct(shape, F32),
            jax.ShapeDtypeStruct(shape, F32),
            jax.ShapeDtypeStruct(shape, BF16),
            jax.ShapeDtypeStruct(shape, BF16),
        ],
        compiler_params=pltpu.CompilerParams(vmem_limit_bytes=VMEM_LIMIT),
        name="peer_route",
    )(hb, wq.T.astype(BF16), subkeys[0].astype(BF16), subkeys[1].astype(BF16))


def _experts_kernel(hb_ref, u_ref, vt_ref, n_ref, a_ref, r2_ref, b_ref, h_ref, g_ref, bb_ref,
                    out_ref, acc_ref, rows_ref, *, tn, te):
    j = pl.program_id(1)
    keys_per_step = te // PEER_KEYS

    def stage_rows(tile, slot):
        for c in range(keys_per_step):
            for h in range(PEER_HEADS):
                for k, ref in enumerate((n_ref, a_ref)):
                    row = ref[h, pl.ds(tile * keys_per_step + c, 1), :]
                    rows_ref[slot, k, c, h] = jnp.broadcast_to(row, (BF16_ROWS, tn)).astype(BF16)

    @pl.when(j == 0)
    def _():
        acc_ref[...] = jnp.zeros_like(acc_ref)
        stage_rows(0, 0)

    slot = j & 1
    hb = hb_ref[...]
    part = te // EXPERT_PARTS
    acts = [_dot(u_ref[i * part:(i + 1) * part, :], hb) for i in range(EXPERT_PARTS)]
    chunks = part // PEER_KEYS
    tiles = PEER_KEYS // BF16_ROWS
    zero = jnp.zeros((tiles, BF16_ROWS, tn), BF16)

    for i in range(EXPERT_PARTS):
        act = _gelu(acts[i])
        ps = []
        for c in range(chunks):
            key = i * chunks + c
            w = zero
            for h in range(PEER_HEADS):
                picked = jnp.where(r2_ref[h] < rows_ref[slot, 0, key, h][None], b_ref[h], zero)
                w = w + rows_ref[slot, 1, key, h][None] * picked
            ps.append(w.reshape(PEER_KEYS, tn) * act[c * PEER_KEYS:(c + 1) * PEER_KEYS].astype(BF16))
        acc_ref[...] += _dot(vt_ref[:, i * part:(i + 1) * part], jnp.concatenate(ps, axis=0))

    stage_rows(jnp.minimum(j + 1, pl.num_programs(1) - 1), 1 - slot)

    @pl.when(j == pl.num_programs(1) - 1)
    def _():
        h = ALPHA * h_ref[...] + acc_ref[...]
        out_ref[...] = _layer_norm_fm(h, g_ref, bb_ref, tn // LANES).T


def _experts(hb, hf, u_b, vt_b, tabs, ln_g, ln_b):
    n = hb.shape[1]
    tn, te = 512, 1024
    tab = lambda: pl.BlockSpec((PEER_HEADS, PEER_KEYS, tn), lambda i, j: (0, 0, i))
    tiles = PEER_KEYS // BF16_ROWS
    tiled = lambda: pl.BlockSpec((PEER_HEADS, tiles, BF16_ROWS, tn), lambda i, j: (0, 0, 0, i))
    n_tab, a_tab, r2_tab, b_tab = tabs
    tabs = (n_tab, a_tab, r2_tab.reshape(PEER_HEADS, tiles, BF16_ROWS, n), b_tab.reshape(PEER_HEADS, tiles, BF16_ROWS, n))
    return pl.pallas_call(
        functools.partial(_experts_kernel, tn=tn, te=te),
        grid=(n // tn, PEER_EXPERTS // te),
        in_specs=[
            pl.BlockSpec((D_MODEL, tn), lambda i, j: (0, i)),
            pl.BlockSpec((te, D_MODEL), lambda i, j: (j, 0)),
            pl.BlockSpec((D_MODEL, te), lambda i, j: (0, j)),
            tab(), tab(), tiled(), tiled(),
            pl.BlockSpec((D_MODEL, tn), lambda i, j: (0, i)),
            pl.BlockSpec((D_MODEL, LANES), lambda i, j: (0, 0)),
            pl.BlockSpec((D_MODEL, LANES), lambda i, j: (0, 0)),
        ],
        out_specs=pl.BlockSpec((tn, D_MODEL), lambda i, j: (i, 0)),
        out_shape=jax.ShapeDtypeStruct((n, D_MODEL), F32),
        scratch_shapes=[pltpu.VMEM((D_MODEL, tn), F32),
                        pltpu.VMEM((2, 2, te // PEER_KEYS, PEER_HEADS, BF16_ROWS, tn), BF16)],
        compiler_params=pltpu.CompilerParams(
            dimension_semantics=("arbitrary", "arbitrary"), vmem_limit_bytes=VMEM_LIMIT),
        name="peer_experts",
    )(hb, u_b, vt_b, *tabs, hf, _lane_bcast(ln_g), _lane_bcast(ln_b))


def kernel(x, w_in, rel_table, cmp_pe_k, cmp_pe_v, cmp_k_w1, cmp_k_w2, cmp_v_w1, cmp_v_w2, pool_w, pool_scale,
           w_out, ln1_g, ln1_b, peer_wq, peer_subkeys, peer_u, peer_v, ln2_g, ln2_b):
    batch, seq, _ = x.shape
    assert w_in.shape[0] == DEPTH == 1 and seq % 512 == 0 and seq // CMP_STRIDE <= 256
    x2 = x.reshape(batch * seq, D_MODEL)
    et, xs, xw = _bias_tables(rel_table)
    qt, vt, gt, zt, kvc, kslc, kwin = _inproj(x2, w_in[0], seq)
    kcmp, vcmpt = _compress(kvc, batch, seq, cmp_pe_k[0], cmp_pe_v[0], cmp_k_w1[0], cmp_k_w2[0],
                            cmp_v_w1[0], cmp_v_w2[0])
    nsat = _nsa(qt, kcmp, vcmpt, kslc, kwin, vt, gt, et, xs, xw, batch, seq)
    hf, hb = _mix(nsat, zt, x2, w_out[0], pool_w[0], pool_scale[0], ln1_g[0], ln1_b[0], seq)
    tabs = _route(hb, peer_wq[0], peer_subkeys[0])
    y = _experts(hb, hf, peer_u[0].astype(BF16), peer_v[0].T.astype(BF16), tabs, ln2_g[0], ln2_b[0])
    return y.reshape(batch, seq, D_MODEL)
```
